```python
import math
import jax, jax.numpy as jnp
from jax import lax
import numpy as np

D_MODEL = 1024
BATCH = 8
SEQ = 4096
DEPTH = 4

N_EVEN = (DEPTH + 1) // 2
N_ODD = DEPTH // 2
D_FF = 2816
GRID_W = 64
Q_BLOCK = 128
ROPE_THETA = 10000.0
EPS = 1e-6

MLA_HEADS = 8
MLA_Q_RANK = 256
MLA_KV_RANK = 128
MLA_NOPE = 64
MLA_ROPE = 32
MLA_V = 64
DIFF_HEADS = 8
DIFF_HEAD = 32
AB_SIZES = (MLA_Q_RANK, MLA_KV_RANK, MLA_ROPE,
            DIFF_HEADS * 2 * DIFF_HEAD, DIFF_HEADS * 2 * DIFF_HEAD, DIFF_HEADS * 2 * DIFF_HEAD)
AB_SPLITS = [int(v) for v in np.cumsum(AB_SIZES)[:-1]]
AB_IN = int(sum(AB_SIZES))
AB_CAT = MLA_HEADS * MLA_V + DIFF_HEADS * 2 * DIFF_HEAD
C_HEADS = 16
C_KV_HEADS = 4
C_GROUPS = C_HEADS // C_KV_HEADS
C_HEAD = 64
C_SIZES = (C_HEADS * C_HEAD, C_KV_HEADS * C_HEAD, C_KV_HEADS * C_HEAD)
C_SPLITS = [int(v) for v in np.cumsum(C_SIZES)[:-1]]
C_IN = int(sum(C_SIZES))

kernel_name = "hybrid_mla_diff_axialgqa_macaron"


def rms_norm(x, g):
    xf = x.astype(jnp.float32)
    y = xf * lax.rsqrt(jnp.mean(xf * xf, axis=-1, keepdims=True) + EPS)
    return (y * g.astype(jnp.float32)).astype(x.dtype)


def swiglu(h, w_gate, w_up, w_down):
    return (jax.nn.silu(h @ w_gate) * (h @ w_up)) @ w_down


def rope_tables(pos, dim):
    inv = ROPE_THETA ** (-jnp.arange(0, dim, 2, dtype=jnp.float32) / dim)
    ang = pos.astype(jnp.float32)[:, None] * inv[None, :]
    return jnp.cos(ang), jnp.sin(ang)


def apply_rope(x, cos, sin):
    s, half = cos.shape
    shp = (s,) + (1,) * (x.ndim - 3) + (half,)
    c, sn = cos.reshape(shp), sin.reshape(shp)
    xf = x.astype(jnp.float32)
    x1, x2 = xf[..., :half], xf[..., half:]
    return jnp.concatenate([x1 * c - x2 * sn, x1 * sn + x2 * c], axis=-1).astype(x.dtype)


def axial_rope(x, row_cs, col_cs):
    half = x.shape[-1] // 2
    return jnp.concatenate([apply_rope(x[..., :half], *row_cs),
                            apply_rope(x[..., half:], *col_cs)], axis=-1)


def _blocks(x):
    b, s = x.shape[:2]
    return jnp.moveaxis(x.reshape((b, s // Q_BLOCK, Q_BLOCK) + x.shape[2:]), 1, 0)


def _unblocks(y):
    nb, b, qb = y.shape[:3]
    return jnp.moveaxis(y, 0, 1).reshape((b, nb * qb) + y.shape[3:])


def softmax_attention(q, k, v, scale):
    def one_block(qb):
        sc = jnp.einsum('bqhgd,bkhd->bhgqk', qb, k, preferred_element_type=jnp.float32) * scale
        p = jax.nn.softmax(sc, axis=-1).astype(v.dtype)
        return jnp.einsum('bhgqk,bkhd->bqhgd', p, v)
    return _unblocks(lax.map(one_block, _blocks(q)))


def differential_attention(q1, q2, k1, k2, v, lam, scale):
    def one_block(qs):
        qb1, qb2 = qs
        a1 = jax.nn.softmax(jnp.einsum('bqhd,bkhd->bhqk', qb1, k1, preferred_element_type=jnp.float32) * scale, axis=-1)
        a2 = jax.nn.softmax(jnp.einsum('bqhd,bkhd->bhqk', qb2, k2, preferred_element_type=jnp.float32) * scale, axis=-1)
        w = (a1 - lam * a2).astype(v.dtype)
        return jnp.einsum('bhqk,bkhd->bqhd', w, v)
    return _unblocks(lax.map(one_block, (_blocks(q1), _blocks(q2))))


def even_mixer(h, w_in, q_lora_norm, w_uq, kv_lora_norm, w_ukv, mla_qn, mla_kn,
               diff_qn, diff_kn, lq1, lk1, lq2, lk2, subln, w_out, mla_cs, diff_cs, lam_init):
    b, s, _ = h.shape
    c_q, c_kv, k_r, dq, dk, dv = jnp.split(h @ w_in, AB_SPLITS, axis=-1)
    q = (rms_norm(c_q, q_lora_norm) @ w_uq).reshape(b, s, MLA_HEADS, MLA_NOPE + MLA_ROPE)
    kv = (rms_norm(c_kv, kv_lora_norm) @ w_ukv).reshape(b, s, MLA_HEADS, MLA_NOPE + MLA_V)
    k_nope, v_mla = kv[..., :MLA_NOPE], kv[..., MLA_NOPE:]
    k_r = jnp.broadcast_to(k_r[:, :, None, :], (b, s, MLA_HEADS, MLA_ROPE))
    k = jnp.concatenate([k_nope, k_r], axis=-1)
    q, k = rms_norm(q, mla_qn), rms_norm(k, mla_kn)
    q = jnp.concatenate([q[..., :MLA_NOPE], apply_rope(q[..., MLA_NOPE:], *mla_cs)], axis=-1)
    k = jnp.concatenate([k[..., :MLA_NOPE], apply_rope(k[..., MLA_NOPE:], *mla_cs)], axis=-1)
    o_mla = softmax_attention(q[:, :, :, None, :], k, v_mla, (MLA_NOPE + MLA_ROPE) ** -0.5)
    o_mla = o_mla.reshape(b, s, MLA_HEADS * MLA_V)
    dq = apply_rope(rms_norm(dq.reshape(b, s, DIFF_HEADS, 2, DIFF_HEAD), diff_qn), *diff_cs)
    dk = apply_rope(rms_norm(dk.reshape(b, s, DIFF_HEADS, 2, DIFF_HEAD), diff_kn), *diff_cs)
    dv = dv.reshape(b, s, DIFF_HEADS, 2 * DIFF_HEAD)
    lam = (jnp.exp(jnp.sum(lq1.astype(jnp.float32) * lk1.astype(jnp.float32)))
           - jnp.exp(jnp.sum(lq2.astype(jnp.float32) * lk2.astype(jnp.float32))) + lam_init)
    o_diff = differential_attention(dq[:, :, :, 0], dq[:, :, :, 1], dk[:, :, :, 0], dk[:, :, :, 1],
                                    dv, lam, DIFF_HEAD ** -0.5)
    o_diff = (rms_norm(o_diff, subln) * (1.0 - lam_init)).reshape(b, s, DIFF_HEADS * 2 * DIFF_HEAD)
    return jnp.concatenate([o_mla, o_diff], axis=-1) @ w_out


def odd_mixer(h, w_in, qn, kn, w_out, row_cs, col_cs):
    b, s, _ = h.shape
    q, k, v = jnp.split(h @ w_in, C_SPLITS, axis=-1)
    q = axial_rope(rms_norm(q.reshape(b, s, C_KV_HEADS, C_GROUPS, C_HEAD), qn), row_cs, col_cs)
    k = axial_rope(rms_norm(k.reshape(b, s, C_KV_HEADS, C_HEAD), kn), row_cs, col_cs)
    v = v.reshape(b, s, C_KV_HEADS, C_HEAD)
    o = softmax_attention(q, k, v, C_HEAD ** -0.5).reshape(b, s, C_HEADS * C_HEAD)
    return o @ w_out


def setup_inputs(seed: int = 0) -> dict:
    key = jax.random.key(seed)
    ks = iter(jax.random.split(key, 64))

    def w(shape, fan_in):
        return jax.random.normal(next(ks), shape, jnp.float32) * fan_in ** -0.5

    def gain(shape):
        return 1.0 + 0.02 * jax.random.normal(next(ks), shape, jnp.float32)

    def small(shape, sc):
        return sc * jax.random.normal(next(ks), shape, jnp.float32)

    return {
        "x": jax.random.normal(next(ks), (BATCH, SEQ, D_MODEL), jnp.float32),
        "ffn1_norm": gain((DEPTH, D_MODEL)),
        "ffn1_w_gate": w((DEPTH, D_MODEL, D_FF), D_MODEL),
        "ffn1_w_up": w((DEPTH, D_MODEL, D_FF), D_MODEL),
        "ffn1_w_down": w((DEPTH, D_FF, D_MODEL), D_FF),
        "mix_norm": gain((DEPTH, D_MODEL)),
        "ffn2_norm": gain((DEPTH, D_MODEL)),
        "ffn2_w_gate": w((DEPTH, D_MODEL, D_FF), D_MODEL),
        "ffn2_w_up": w((DEPTH, D_MODEL, D_FF), D_MODEL),
        "ffn2_w_down": w((DEPTH, D_FF, D_MODEL), D_FF),
        "ab_w_in": w((N_EVEN, D_MODEL, AB_IN), D_MODEL),
        "mla_q_lora_norm": gain((N_EVEN, MLA_Q_RANK)),
        "mla_w_uq": w((N_EVEN, MLA_Q_RANK, MLA_HEADS * (MLA_NOPE + MLA_ROPE)), MLA_Q_RANK),
        "mla_kv_lora_norm": gain((N_EVEN, MLA_KV_RANK)),
        "mla_w_ukv": w((N_EVEN, MLA_KV_RANK, MLA_HEADS * (MLA_NOPE + MLA_V)), MLA_KV_RANK),
        "mla_q_norm": gain((N_EVEN, MLA_NOPE + MLA_ROPE)),
        "mla_k_norm": gain((N_EVEN, MLA_NOPE + MLA_ROPE)),
        "diff_q_norm": gain((N_EVEN, DIFF_HEAD)),
        "diff_k_norm": gain((N_EVEN, DIFF_HEAD)),
        "diff_lambda_q1": small((N_EVEN, DIFF_HEAD), 0.1),
        "diff_lambda_k1": small((N_EVEN, DIFF_HEAD), 0.1),
        "diff_lambda_q2": small((N_EVEN, DIFF_HEAD), 0.1),
        "diff_lambda_k2": small((N_EVEN, DIFF_HEAD), 0.1),
        "diff_subln": gain((N_EVEN, 2 * DIFF_HEAD)),
        "ab_w_out": w((N_EVEN, AB_CAT, D_MODEL), AB_CAT),
        "c_w_in": w((N_ODD, D_MODEL, C_IN), D_MODEL),
        "c_q_norm": gain((N_ODD, C_HEAD)),
        "c_k_norm": gain((N_ODD, C_HEAD)),
        "c_w_out": w((N_ODD, C_HEADS * C_HEAD, D_MODEL), C_HEADS * C_HEAD),
    }


def reference(x, ffn1_norm, ffn1_w_gate, ffn1_w_up, ffn1_w_down, mix_norm,
              ffn2_norm, ffn2_w_gate, ffn2_w_up, ffn2_w_down,
              ab_w_in, mla_q_lora_norm, mla_w_uq, mla_kv_lora_norm, mla_w_ukv,
              mla_q_norm, mla_k_norm, diff_q_norm, diff_k_norm,
              diff_lambda_q1, diff_lambda_k1, diff_lambda_q2, diff_lambda_k2,
              diff_subln, ab_w_out, c_w_in, c_q_norm, c_k_norm, c_w_out):
    b, s, _ = x.shape
    n_rows = s // GRID_W
    pos = jnp.arange(s, dtype=jnp.int32)
    row_pos = jnp.repeat(jnp.arange(n_rows, dtype=jnp.int32), GRID_W)
    col_pos = jnp.tile(jnp.arange(GRID_W, dtype=jnp.int32), n_rows)
    mla_cs = rope_tables(pos, MLA_ROPE)
    diff_cs = rope_tables(pos, DIFF_HEAD)
    row_cs = rope_tables(row_pos, C_HEAD // 2)
    col_cs = rope_tables(col_pos, C_HEAD // 2)

    for i in range(DEPTH):
        x = x + 0.5 * swiglu(rms_norm(x, ffn1_norm[i]), ffn1_w_gate[i], ffn1_w_up[i], ffn1_w_down[i])
        h = rms_norm(x, mix_norm[i])
        j = i // 2
        if i % 2 == 0:
            lam_init = 0.8 - 0.6 * math.exp(-0.3 * i)
            x = x + even_mixer(h, ab_w_in[j], mla_q_lora_norm[j], mla_w_uq[j], mla_kv_lora_norm[j],
                               mla_w_ukv[j], mla_q_norm[j], mla_k_norm[j], diff_q_norm[j],
                               diff_k_norm[j], diff_lambda_q1[j], diff_lambda_k1[j],
                               diff_lambda_q2[j], diff_lambda_k2[j], diff_subln[j], ab_w_out[j],
                               mla_cs, diff_cs, lam_init)
        else:
            x = x + odd_mixer(h, c_w_in[j], c_q_norm[j], c_k_norm[j], c_w_out[j], row_cs, col_cs)
        x = x + 0.5 * swiglu(rms_norm(x, ffn2_norm[i]), ffn2_w_gate[i], ffn2_w_up[i], ffn2_w_down[i])
    return x
```

```python
import functools
import math

import jax
import jax.numpy as jnp
import numpy as np
from jax import lax
from jax.experimental import pallas as pl
from jax.experimental.pallas import tpu as pltpu

F32 = jnp.float32
BF16 = jnp.bfloat16

LANES = 128
EPS = 1e-6
ROPE_THETA = 10000.0
GRID_W = 64
LOG2E = math.log2(math.e)

D_MODEL = 1024
D_FF = 2816
MLA_HEADS, MLA_Q_RANK, MLA_KV_RANK = 8, 256, 128
MLA_NOPE, MLA_ROPE, MLA_V = 64, 32, 64
DIFF_HEADS, DIFF_HEAD = 8, 32
C_HEADS, C_KV_HEADS, C_HEAD = 16, 4, 64
C_GROUPS = C_HEADS // C_KV_HEADS
ROPE_DIM = 32

VMEM_LIMIT = 56 * 1024 * 1024


def _rms(x, g):
    ms = jnp.mean(x * x, axis=-1, keepdims=True)
    return x * lax.rsqrt(ms + EPS) * g


def _bdot(a, b):
    return jnp.dot(a.astype(BF16), b, preferred_element_type=F32)


def _swiglu_half(x, g, wg, wu, wd):
    h = _rms(x, g).astype(BF16)
    gate = jnp.dot(h, wg[...], preferred_element_type=F32)
    up = jnp.dot(h, wu[...], preferred_element_type=F32)
    act = (gate / (1.0 + jnp.exp(-gate))) * up
    return x + 0.5 * jnp.dot(act.astype(BF16), wd[...], preferred_element_type=F32)


def _lane_iota(rows):
    return lax.broadcasted_iota(jnp.int32, (rows, LANES), 1)


def _norm_rope_block(x, group, gain, cos, sin, post_scale):
    rows = x.shape[0]
    lane = _lane_iota(rows)
    x2 = x * x
    if group == 96:
        r = lax.rsqrt(jnp.sum(x2, axis=-1, keepdims=True) * (1.0 / 96.0) + EPS)
    else:
        r = jnp.zeros_like(x)
        for s in range(0, LANES, group):
            m = (lane >= s) & (lane < s + group)
            ms = jnp.sum(jnp.where(m, x2, 0.0), axis=-1, keepdims=True) * (1.0 / group)
            r = jnp.where(m, lax.rsqrt(ms + EPS), r)
    y = x * r * gain
    up = pltpu.roll(y, LANES - ROPE_DIM // 2, axis=1)
    dn = pltpu.roll(y, ROPE_DIM // 2, axis=1)
    partner = jnp.where((lane % ROPE_DIM) < ROPE_DIM // 2, up, dn)
    y = y * cos + partner * sin
    if post_scale != 1.0:
        y = y * post_scale
    return y


def _softmax_pv(qm, k, v):
    s = lax.dot_general(qm, k, (((1,), (1,)), ((), ())), preferred_element_type=F32)
    m = jnp.max(s, axis=-1, keepdims=True)
    p = jnp.exp2(s - m)
    l = jnp.sum(p, axis=-1, keepdims=True)
    o = jnp.dot(p.astype(BF16), v, preferred_element_type=F32)
    return o / l


def _even_in_kernel(x_ref, g1_ref, wg_ref, wu_ref, wd_ref, gm_ref, win_ref,
                    gq_ref, wuq_ref, gkv_ref, wukv_ref, qn_ref, kn_ref, dqn_ref, dkn_ref,
                    cm_ref, sm_ref, cd_ref, sd_ref,
                    xo_ref, q_ref, k_ref, v_ref, dq_ref, dk_ref, dv_ref):
    x = _swiglu_half(x_ref[...], g1_ref[...], wg_ref, wu_ref, wd_ref)
    xo_ref[...] = x
    h = _rms(x, gm_ref[...]).astype(BF16)
    z = jnp.dot(h, win_ref[...], preferred_element_type=F32)
    c_q = z[:, 0:256]
    c_kv = z[:, 256:384]
    kr = z[:, 384:512]
    cm, sm, cd, sd = cm_ref[...], sm_ref[...], cd_ref[...], sd_ref[...]
    qf = _bdot(_rms(c_q, gq_ref[...]), wuq_ref[...])
    kvf = _bdot(_rms(c_kv, gkv_ref[...]), wukv_ref[...])
    q_scale = (MLA_NOPE + MLA_ROPE) ** -0.5 * LOG2E
    for hd in range(MLA_HEADS):
        sl = slice(hd * LANES, (hd + 1) * LANES)
        q_ref[:, sl] = _norm_rope_block(qf[:, sl], 96, qn_ref[...], cm, sm, q_scale).astype(BF16)
        k_ref[:, sl] = _norm_rope_block(kvf[:, sl] + kr, 96, kn_ref[...], cm, sm, 1.0).astype(BF16)
    v_ref[...] = kvf[:, MLA_HEADS * LANES:].astype(BF16)
    d_scale = DIFF_HEAD ** -0.5 * LOG2E
    for b in range(4):
        sl = slice(b * LANES, (b + 1) * LANES)
        dq_ref[:, sl] = _norm_rope_block(z[:, 512 + b * LANES:512 + (b + 1) * LANES], 32,
                                         dqn_ref[...], cd, sd, d_scale).astype(BF16)
        dk_ref[:, sl] = _norm_rope_block(z[:, 1024 + b * LANES:1024 + (b + 1) * LANES], 32,
                                         dkn_ref[...], cd, sd, 1.0).astype(BF16)
    dv_ref[...] = z[:, 1536:2048].astype(BF16)


def _odd_in_kernel(x_ref, g1_ref, wg_ref, wu_ref, wd_ref, gm_ref, win_ref,
                   qn_ref, kn_ref, ca_ref, sa_ref,
                   xo_ref, q_ref, k_ref, v_ref):
    x = _swiglu_half(x_ref[...], g1_ref[...], wg_ref, wu_ref, wd_ref)
    xo_ref[...] = x
    h = _rms(x, gm_ref[...]).astype(BF16)
    z = jnp.dot(h, win_ref[...], preferred_element_type=F32)
    ca, sa = ca_ref[...], sa_ref[...]
    q_scale = C_HEAD ** -0.5 * LOG2E
    for b in range(8):
        sl = slice(b * LANES, (b + 1) * LANES)
        q_ref[:, sl] = _norm_rope_block(z[:, sl], 64, qn_ref[...], ca, sa, q_scale).astype(BF16)
    for b in range(2):
        sl = slice(b * LANES, (b + 1) * LANES)
        k_ref[:, sl] = _norm_rope_block(z[:, 1024 + b * LANES:1024 + (b + 1) * LANES], 64,
                                        kn_ref[...], ca, sa, 1.0).astype(BF16)
    v_ref[...] = z[:, 1280:1536].astype(BF16)


def _even_out_kernel(x_ref, oa_ref, ob_ref, woa_ref, wob_ref, g2_ref, wg_ref, wu_ref, wd_ref, xo_ref):
    x = x_ref[...]
    x = x + jnp.dot(oa_ref[...], woa_ref[...], preferred_element_type=F32)
    x = x + jnp.dot(ob_ref[...], wob_ref[...], preferred_element_type=F32)
    xo_ref[...] = _swiglu_half(x, g2_ref[...], wg_ref, wu_ref, wd_ref)


def _odd_out_kernel(x_ref, o_ref, wo_ref, g2_ref, wg_ref, wu_ref, wd_ref, xo_ref):
    x = x_ref[...] + jnp.dot(o_ref[...], wo_ref[...], preferred_element_type=F32)
    xo_ref[...] = _swiglu_half(x, g2_ref[...], wg_ref, wu_ref, wd_ref)


def _mla_attn_kernel(q_ref, k_ref, v_ref, o_ref):
    v = v_ref[...]
    o0 = _softmax_pv(q_ref[:, 0:LANES], k_ref[:, 0:LANES], v)
    o1 = _softmax_pv(q_ref[:, LANES:2 * LANES], k_ref[:, LANES:2 * LANES], v)
    lane = _lane_iota(o0.shape[0])
    o_ref[...] = jnp.where(lane < MLA_V, o0, o1).astype(BF16)


def _gqa_attn_kernel(q_ref, k_ref, v_ref, o_ref):
    q, k, v = q_ref[...], k_ref[...], v_ref[...]
    lane = _lane_iota(q.shape[0])
    zero = jnp.zeros_like(q)
    o0 = _softmax_pv(jnp.where(lane < C_HEAD, q, zero), k, v)
    o1 = _softmax_pv(jnp.where(lane >= C_HEAD, q, zero), k, v)
    o_ref[...] = jnp.where(lane < C_HEAD, o0, o1).astype(BF16)


def _diff_attn_kernel(lam_init, q_ref, k_ref, v_ref, lp_ref, sub_ref, o_ref):
    q, k, v = q_ref[...], k_ref[...], v_ref[...]
    lp = lp_ref[...]
    lam = (jnp.exp(jnp.sum(lp[0:1] * lp[1:2], axis=-1, keepdims=True))
           - jnp.exp(jnp.sum(lp[2:3] * lp[3:4], axis=-1, keepdims=True)) + lam_init)
    lane = _lane_iota(q.shape[0])
    zero = jnp.zeros_like(q)
    outs = []
    for hd in range(2):
        base = hd * 2 * DIFF_HEAD
        m1 = (lane >= base) & (lane < base + DIFF_HEAD)
        m2 = (lane >= base + DIFF_HEAD) & (lane < base + 2 * DIFF_HEAD)
        a1 = _softmax_pv(jnp.where(m1, q, zero), k, v)
        a2 = _softmax_pv(jnp.where(m2, q, zero), k, v)
        outs.append(a1 - lam * a2)
    lo = lane < 2 * DIFF_HEAD
    o = jnp.where(lo, outs[0], outs[1])
    o2 = o * o
    inv = 1.0 / (2 * DIFF_HEAD)
    r0 = lax.rsqrt(jnp.sum(jnp.where(lo, o2, 0.0), axis=-1, keepdims=True) * inv + EPS)
    r1 = lax.rsqrt(jnp.sum(jnp.where(lo, 0.0, o2), axis=-1, keepdims=True) * inv + EPS)
    o = o * jnp.where(lo, r0, r1) * sub_ref[...] * (1.0 - lam_init)
    o_ref[...] = o.astype(BF16)


def _tile(n, want):
    return want if n % want == 0 else n


def _const_spec(shape):
    nd = len(shape)
    return pl.BlockSpec(shape, lambda *_: (0,) * nd, pipeline_mode=pl.Buffered(1))


def _tok_spec(tm, width):
    return pl.BlockSpec((None, tm, width), lambda b, t: (b, t, 0))


def _tab_spec(tm):
    return pl.BlockSpec((tm, LANES), lambda b, t: (t, 0))


def _token_call(body, name, x, consts, tabs, extra_tok, out_widths, out_dtypes, tm):
    bsz, seq, _ = x.shape
    tok_inputs = [x] + list(extra_tok)
    in_specs = ([_tok_spec(tm, a.shape[-1]) for a in tok_inputs]
                + [_const_spec(c.shape) for c in consts]
                + [_tab_spec(tm) for _ in tabs])
    out_shape = [jax.ShapeDtypeStruct((bsz, seq, w), d) for w, d in zip(out_widths, out_dtypes)]
    out_specs = [_tok_spec(tm, w) for w in out_widths]
    return pl.pallas_call(
        body, name=name,
        grid=(bsz, seq // tm),
        in_specs=in_specs, out_specs=out_specs, out_shape=out_shape,
        compiler_params=pltpu.CompilerParams(
            dimension_semantics=("arbitrary", "arbitrary"), vmem_limit_bytes=VMEM_LIMIT),
    )(*tok_inputs, *consts, *tabs)


def _attn_call(body, name, q, k, v, extra, q_blk, kv_blk, n_blocks, kv_map, tq):
    bsz, seq, _ = q.shape
    in_specs = [
        pl.BlockSpec((None, tq, q_blk), lambda b, h, t: (b, t, h)),
        pl.BlockSpec((None, seq, kv_blk), lambda b, h, t: (b, 0, kv_map(h))),
        pl.BlockSpec((None, seq, LANES), lambda b, h, t: (b, 0, kv_map(h))),
    ] + [pl.BlockSpec(e.shape, lambda b, h, t: (0, 0)) for e in extra]
    return pl.pallas_call(
        body, name=name,
        grid=(bsz, n_blocks, seq // tq),
        in_specs=in_specs,
        out_specs=pl.BlockSpec((None, tq, LANES), lambda b, h, t: (b, t, h)),
        out_shape=jax.ShapeDtypeStruct((bsz, seq, n_blocks * LANES), BF16),
        compiler_params=pltpu.CompilerParams(
            dimension_semantics=("arbitrary", "arbitrary", "arbitrary"), vmem_limit_bytes=VMEM_LIMIT),
    )(q, k, v, *extra)


def _rope_lane_tables(pos_a, pos_b):
    half = ROPE_DIM // 2
    inv = ROPE_THETA ** (-jnp.arange(0, ROPE_DIM, 2, dtype=F32) / ROPE_DIM)
    lane = np.arange(LANES)
    sign = jnp.asarray(np.where(lane % ROPE_DIM < half, -1.0, 1.0), F32)
    use_b = jnp.asarray((lane // ROPE_DIM) % 2 == 1)
    freq = inv[lane % half]
    ang_a = pos_a.astype(F32)[:, None] * freq[None, :]
    ang_b = pos_b.astype(F32)[:, None] * freq[None, :]
    cos = jnp.where(use_b[None, :], jnp.cos(ang_b), jnp.cos(ang_a))
    sin = jnp.where(use_b[None, :], jnp.sin(ang_b), jnp.sin(ang_a)) * sign[None, :]
    return cos, sin


def _row(v):
    return v.reshape(1, -1).astype(F32)


def kernel(x, ffn1_norm, ffn1_w_gate, ffn1_w_up, ffn1_w_down, mix_norm, ffn2_norm, ffn2_w_gate, ffn2_w_up, ffn2_w_down, ab_w_in, mla_q_lora_norm, mla_w_uq, mla_kv_lora_norm, mla_w_ukv, mla_q_norm, mla_k_norm, diff_q_norm, diff_k_norm, diff_lambda_q1, diff_lambda_k1, diff_lambda_q2, diff_lambda_k2, diff_subln, ab_w_out, c_w_in, c_q_norm, c_k_norm, c_w_out):
    bsz, seq, d = x.shape
    depth = ffn1_norm.shape[0]
    tm = _tile(seq, 512)
    tq = _tile(seq, 256)

    pos = jnp.arange(seq, dtype=jnp.int32)
    cos_t, sin_t = _rope_lane_tables(pos, pos)
    cos_ax, sin_ax = _rope_lane_tables(pos // GRID_W, pos % GRID_W)
    lane = np.arange(LANES)
    mla_rot = jnp.asarray((lane >= MLA_NOPE) & (lane < MLA_NOPE + MLA_ROPE))
    cos_m = jnp.where(mla_rot[None, :], cos_t, 1.0)
    sin_m = jnp.where(mla_rot[None, :], sin_t, 0.0)

    for i in range(depth):
        j = i // 2
        ffn1 = (_row(ffn1_norm[i]), ffn1_w_gate[i].astype(BF16), ffn1_w_up[i].astype(BF16),
                ffn1_w_down[i].astype(BF16))
        ffn2 = (_row(ffn2_norm[i]), ffn2_w_gate[i].astype(BF16), ffn2_w_up[i].astype(BF16),
                ffn2_w_down[i].astype(BF16))
        if i % 2 == 0:
            lam_init = 0.8 - 0.6 * math.exp(-0.3 * i)
            w = ab_w_in[j]
            kr_blk = jnp.pad(w[:, 384:416], ((0, 0), (MLA_NOPE, LANES - MLA_NOPE - MLA_ROPE)))
            w_in = jnp.concatenate([w[:, 0:384], kr_blk, w[:, 416:]], axis=1).astype(BF16)
            w_uq = jnp.pad(mla_w_uq[j].reshape(MLA_Q_RANK, MLA_HEADS, MLA_NOPE + MLA_ROPE),
                           ((0, 0), (0, 0), (0, LANES - MLA_NOPE - MLA_ROPE))
                           ).reshape(MLA_Q_RANK, MLA_HEADS * LANES).astype(BF16)
            ukv = mla_w_ukv[j].reshape(MLA_KV_RANK, MLA_HEADS, MLA_NOPE + MLA_V)
            w_uk = jnp.pad(ukv[:, :, :MLA_NOPE], ((0, 0), (0, 0), (0, LANES - MLA_NOPE))
                           ).reshape(MLA_KV_RANK, MLA_HEADS * LANES)
            w_uv = ukv[:, :, MLA_NOPE:].reshape(MLA_KV_RANK, MLA_HEADS * MLA_V)
            w_ukv = jnp.concatenate([w_uk, w_uv], axis=1).astype(BF16)
            pad96 = (0, LANES - MLA_NOPE - MLA_ROPE)
            consts = ffn1 + (_row(mix_norm[i]), w_in,
                             _row(mla_q_lora_norm[j]), w_uq, _row(mla_kv_lora_norm[j]), w_ukv,
                             _row(jnp.pad(mla_q_norm[j], pad96)), _row(jnp.pad(mla_k_norm[j], pad96)),
                             _row(jnp.tile(diff_q_norm[j], LANES // DIFF_HEAD)),
                             _row(jnp.tile(diff_k_norm[j], LANES // DIFF_HEAD)))
            x, q, k, v, dq, dk, dv = _token_call(
                _even_in_kernel, f"even_in_{i}", x, consts, (cos_m, sin_m, cos_t, sin_t), (),
                (d, 1024, 1024, 512, 512, 512, 512), (F32, BF16, BF16, BF16, BF16, BF16, BF16), tm)
            o_mla = _attn_call(_mla_attn_kernel, f"mla_attn_{i}", q, k, v, (),
                               2 * LANES, 2 * LANES, MLA_HEADS // 2, lambda h: h, tq)
            lam_p = jnp.stack([diff_lambda_q1[j], diff_lambda_k1[j],
                               diff_lambda_q2[j], diff_lambda_k2[j]]).astype(F32)
            sub = _row(jnp.tile(diff_subln[j], 2))
            o_diff = _attn_call(functools.partial(_diff_attn_kernel, lam_init), f"diff_attn_{i}",
                                dq, dk, dv, (lam_p, sub), LANES, LANES, DIFF_HEADS // 2, lambda h: h, tq)
            wo = ab_w_out[j].astype(BF16)
            consts = (wo[:MLA_HEADS * MLA_V], wo[MLA_HEADS * MLA_V:]) + ffn2
            (x,) = _token_call(_even_out_kernel, f"even_out_{i}", x, consts, (), (o_mla, o_diff),
                               (d,), (F32,), tm)
        else:
            w = c_w_in[j]
            wq = w[:, :C_HEADS * C_HEAD].reshape(d, 2, 2, C_GROUPS, C_HEAD)
            wq = wq.transpose(0, 1, 3, 2, 4).reshape(d, C_HEADS * C_HEAD)
            w_in = jnp.concatenate([wq, w[:, C_HEADS * C_HEAD:]], axis=1).astype(BF16)
            consts = ffn1 + (_row(mix_norm[i]), w_in,
                             _row(jnp.tile(c_q_norm[j], 2)), _row(jnp.tile(c_k_norm[j], 2)))
            x, q, k, v = _token_call(
                _odd_in_kernel, f"odd_in_{i}", x, consts, (cos_ax, sin_ax), (),
                (d, 1024, 256, 256), (F32, BF16, BF16, BF16), tm)
            o = _attn_call(_gqa_attn_kernel, f"gqa_attn_{i}", q, k, v, (),
                           LANES, LANES, C_HEADS // 2, lambda h: h // C_GROUPS, tq)
            wo = c_w_out[j].reshape(2, 2, C_GROUPS, C_HEAD, d).transpose(0, 2, 1, 3, 4)
            wo = wo.reshape(C_HEADS * C_HEAD, d).astype(BF16)
            consts = (wo,) + ffn2
            (x,) = _token_call(_odd_out_kernel, f"odd_out_{i}", x, consts, (), (o,), (d,), (F32,), tm)
    return x
```

```python
import functools
import math

import jax
import jax.numpy as jnp
import numpy as np
from jax import lax
from jax.experimental import pallas as pl
from jax.experimental.pallas import tpu as pltpu

F32 = jnp.float32
BF16 = jnp.bfloat16

LANES = 128
EPS = 1e-6
ROPE_THETA = 10000.0
GRID_W = 64
LOG2E = math.log2(math.e)

D_MODEL = 1024
D_FF = 2816
MLA_HEADS, MLA_Q_RANK, MLA_KV_RANK = 8, 256, 128
MLA_NOPE, MLA_ROPE, MLA_V = 64, 32, 64
DIFF_HEADS, DIFF_HEAD = 8, 32
C_HEADS, C_KV_HEADS, C_HEAD = 16, 4, 64
C_GROUPS = C_HEADS // C_KV_HEADS
ROPE_DIM = 32

VMEM_LIMIT = 56 * 1024 * 1024
TOKEN_TILE = 512
TOKEN_SPLIT = 2
QUERY_TILE = 512

NT_DIMS = (((1,), (1,)), ((), ()))


def _rms(x, g):
    ms = jnp.mean(x * x, axis=-1, keepdims=True)
    return x * lax.rsqrt(ms + EPS) * g


def _bdot(a, b):
    return jnp.dot(a.astype(BF16), b, preferred_element_type=F32)


def _swiglu_half(x, g, wg, wu, wd):
    h = _rms(x, g).astype(BF16)
    gate = jnp.dot(h, wg[...], preferred_element_type=F32)
    up = jnp.dot(h, wu[...], preferred_element_type=F32)
    act = (gate / (1.0 + jnp.exp(-gate))) * up
    return x + 0.5 * jnp.dot(act.astype(BF16), wd[...], preferred_element_type=F32)


def _lane_iota(rows):
    return lax.broadcasted_iota(jnp.int32, (rows, LANES), 1)


def _norm_rope_block(x, gsum, inv_count, gain, cos, sin, post_scale):
    lane = _lane_iota(x.shape[0])
    x2 = x * x
    hi = x2.astype(BF16)
    lo = (x2 - hi.astype(F32)).astype(BF16)
    ss = jnp.dot(jnp.concatenate([hi, lo], axis=1), gsum, preferred_element_type=F32)
    y = x * lax.rsqrt(ss * inv_count + EPS) * gain
    up = pltpu.roll(y, LANES - ROPE_DIM // 2, axis=1)
    dn = pltpu.roll(y, ROPE_DIM // 2, axis=1)
    partner = jnp.where((lane % ROPE_DIM) < ROPE_DIM // 2, up, dn)
    y = y * cos + partner * sin
    if post_scale != 1.0:
        y = y * post_scale
    return y


def _softmax_pv_t(qm, k, vt):
    st = lax.dot_general(k, qm, NT_DIMS, preferred_element_type=F32)
    m = jnp.max(st, axis=0, keepdims=True)
    p = jnp.exp2(st - m)
    l = jnp.sum(p, axis=0, keepdims=True)
    ot = jnp.dot(vt, p.astype(BF16), preferred_element_type=F32)
    return ot / l


def _row_groups(n_rows):
    step = n_rows // TOKEN_SPLIT
    return [slice(r * step, (r + 1) * step) for r in range(TOKEN_SPLIT)]


def _even_in_kernel(x_ref, g1_ref, wg_ref, wu_ref, wd_ref, gm_ref, win_ref, wvt_ref,
                    gq_ref, wuq_ref, gkv_ref, wuk_ref, wuvt_ref, qn_ref, kn_ref, dqn_ref, dkn_ref,
                    gs96_ref, gs32_ref, cm_ref, sm_ref, cd_ref, sd_ref,
                    xo_ref, q_ref, k_ref, vt_ref, dq_ref, dk_ref, dvt_ref):
    q_scale = (MLA_NOPE + MLA_ROPE) ** -0.5 * LOG2E
    d_scale = DIFF_HEAD ** -0.5 * LOG2E
    for rows in _row_groups(x_ref.shape[0]):
        x = _swiglu_half(x_ref[rows, :], g1_ref[...], wg_ref, wu_ref, wd_ref)
        xo_ref[rows, :] = x
        h = _rms(x, gm_ref[...]).astype(BF16)
        z = jnp.dot(h, win_ref[...], preferred_element_type=F32)
        dvt_ref[:, rows] = lax.dot_general(wvt_ref[...], h, NT_DIMS,
                                           preferred_element_type=F32).astype(BF16)
        kr = z[:, 384:512]
        cm, sm, cd, sd = cm_ref[rows, :], sm_ref[rows, :], cd_ref[rows, :], sd_ref[rows, :]
        ckv = _rms(z[:, 256:384], gkv_ref[...]).astype(BF16)
        qf = _bdot(_rms(z[:, 0:256], gq_ref[...]), wuq_ref[...])
        kf = jnp.dot(ckv, wuk_ref[...], preferred_element_type=F32)
        vt_ref[:, rows] = lax.dot_general(wuvt_ref[...], ckv, NT_DIMS,
                                          preferred_element_type=F32).astype(BF16)
        for hd in range(MLA_HEADS):
            sl = slice(hd * LANES, (hd + 1) * LANES)
            q_ref[rows, sl] = _norm_rope_block(qf[:, sl], gs96_ref[...], 1.0 / 96.0, qn_ref[...],
                                               cm, sm, q_scale).astype(BF16)
            k_ref[rows, sl] = _norm_rope_block(kf[:, sl] + kr, gs96_ref[...], 1.0 / 96.0, kn_ref[...],
                                               cm, sm, 1.0).astype(BF16)
        for b in range(4):
            sl = slice(b * LANES, (b + 1) * LANES)
            dq_ref[rows, sl] = _norm_rope_block(z[:, 512 + b * LANES:512 + (b + 1) * LANES],
                                                gs32_ref[...], 1.0 / DIFF_HEAD, dqn_ref[...],
                                                cd, sd, d_scale).astype(BF16)
            dk_ref[rows, sl] = _norm_rope_block(z[:, 1024 + b * LANES:1024 + (b + 1) * LANES],
                                                gs32_ref[...], 1.0 / DIFF_HEAD, dkn_ref[...],
                                                cd, sd, 1.0).astype(BF16)


def _odd_in_kernel(x_ref, g1_ref, wg_ref, wu_ref, wd_ref, gm_ref, win_ref, wvt_ref,
                   qn_ref, kn_ref, gs64_ref, ca_ref, sa_ref,
                   xo_ref, q_ref, k_ref, vt_ref):
    q_scale = C_HEAD ** -0.5 * LOG2E
    for rows in _row_groups(x_ref.shape[0]):
        x = _swiglu_half(x_ref[rows, :], g1_ref[...], wg_ref, wu_ref, wd_ref)
        xo_ref[rows, :] = x
        h = _rms(x, gm_ref[...]).astype(BF16)
        z = jnp.dot(h, win_ref[...], preferred_element_type=F32)
        vt_ref[:, rows] = lax.dot_general(wvt_ref[...], h, NT_DIMS,
                                          preferred_element_type=F32).astype(BF16)
        ca, sa = ca_ref[rows, :], sa_ref[rows, :]
        for b in range(8):
            sl = slice(b * LANES, (b + 1) * LANES)
            q_ref[rows, sl] = _norm_rope_block(z[:, sl], gs64_ref[...], 1.0 / C_HEAD, qn_ref[...],
                                               ca, sa, q_scale).astype(BF16)
        for b in range(2):
            sl = slice(b * LANES, (b + 1) * LANES)
            k_ref[rows, sl] = _norm_rope_block(z[:, 1024 + b * LANES:1024 + (b + 1) * LANES],
                                               gs64_ref[...], 1.0 / C_HEAD, kn_ref[...],
                                               ca, sa, 1.0).astype(BF16)


def _even_out_kernel(x_ref, oa_ref, ob_ref, woa_ref, wob_ref, g2_ref, wg_ref, wu_ref, wd_ref, xo_ref):
    for rows in _row_groups(x_ref.shape[0]):
        x = x_ref[rows, :]
        x = x + jnp.dot(oa_ref[rows, :], woa_ref[...], preferred_element_type=F32)
        x = x + jnp.dot(ob_ref[rows, :], wob_ref[...], preferred_element_type=F32)
        xo_ref[rows, :] = _swiglu_half(x, g2_ref[...], wg_ref, wu_ref, wd_ref)


def _odd_out_kernel(x_ref, o_ref, wo_ref, g2_ref, wg_ref, wu_ref, wd_ref, xo_ref):
    for rows in _row_groups(x_ref.shape[0]):
        x = x_ref[rows, :] + jnp.dot(o_ref[rows, :], wo_ref[...], preferred_element_type=F32)
        xo_ref[rows, :] = _swiglu_half(x, g2_ref[...], wg_ref, wu_ref, wd_ref)


def _pair_rows(ot0, ot1, split):
    return jnp.concatenate([ot0[:split], ot1[split:]], axis=0).T


def _mla_attn_kernel(q_ref, k_ref, vt_ref, o_ref):
    vt = vt_ref[...]
    ot0 = _softmax_pv_t(q_ref[:, 0:LANES], k_ref[:, 0:LANES], vt)
    ot1 = _softmax_pv_t(q_ref[:, LANES:2 * LANES], k_ref[:, LANES:2 * LANES], vt)
    o_ref[...] = _pair_rows(ot0, ot1, MLA_V).astype(BF16)


def _gqa_attn_kernel(q_ref, k_ref, vt_ref, o_ref):
    q, k, vt = q_ref[...], k_ref[...], vt_ref[...]
    lane = _lane_iota(q.shape[0])
    zero = jnp.zeros_like(q)
    ot0 = _softmax_pv_t(jnp.where(lane < C_HEAD, q, zero), k, vt)
    ot1 = _softmax_pv_t(jnp.where(lane >= C_HEAD, q, zero), k, vt)
    o_ref[...] = _pair_rows(ot0, ot1, C_HEAD).astype(BF16)


def _diff_attn_kernel(lam_init, q_ref, k_ref, vt_ref, lp_ref, sub_ref, o_ref):
    q, k, vt = q_ref[...], k_ref[...], vt_ref[...]
    lp = lp_ref[...]
    lam = (jnp.exp(jnp.sum(lp[0:1] * lp[1:2], axis=-1, keepdims=True))
           - jnp.exp(jnp.sum(lp[2:3] * lp[3:4], axis=-1, keepdims=True)) + lam_init)
    lane = _lane_iota(q.shape[0])
    zero = jnp.zeros_like(q)
    outs = []
    for hd in range(2):
        base = hd * 2 * DIFF_HEAD
        m1 = (lane >= base) & (lane < base + DIFF_HEAD)
        m2 = (lane >= base + DIFF_HEAD) & (lane < base + 2 * DIFF_HEAD)
        a1 = _softmax_pv_t(jnp.where(m1, q, zero), k, vt)
        a2 = _softmax_pv_t(jnp.where(m2, q, zero), k, vt)
        outs.append(a1 - lam * a2)
    o = _pair_rows(outs[0], outs[1], 2 * DIFF_HEAD)
    lo = lane < 2 * DIFF_HEAD
    o2 = o * o
    inv = 1.0 / (2 * DIFF_HEAD)
    r0 = lax.rsqrt(jnp.sum(jnp.where(lo, o2, 0.0), axis=-1, keepdims=True) * inv + EPS)
    r1 = lax.rsqrt(jnp.sum(jnp.where(lo, 0.0, o2), axis=-1, keepdims=True) * inv + EPS)
    o = o * jnp.where(lo, r0, r1) * sub_ref[...] * (1.0 - lam_init)
    o_ref[...] = o.astype(BF16)


def _tile(n, want):
    return want if n % want == 0 else n


def _const_spec(shape):
    nd = len(shape)
    return pl.BlockSpec(shape, lambda *_: (0,) * nd, pipeline_mode=pl.Buffered(1))


def _tok_spec(tm, width):
    return pl.BlockSpec((None, tm, width), lambda b, t: (b, t, 0))


def _tok_t_spec(tm, width):
    return pl.BlockSpec((None, width, tm), lambda b, t: (b, 0, t))


def _tab_spec(tm):
    return pl.BlockSpec((tm, LANES), lambda b, t: (t, 0))


def _token_call(body, name, x, consts, tabs, extra_tok, outs, tm):
    bsz, seq, _ = x.shape
    tok_inputs = [x] + list(extra_tok)
    in_specs = ([_tok_spec(tm, a.shape[-1]) for a in tok_inputs]
                + [_const_spec(c.shape) for c in consts]
                + [_tab_spec(tm) for _ in tabs])
    out_shape = [jax.ShapeDtypeStruct((bsz, w, seq) if tr else (bsz, seq, w), d) for w, d, tr in outs]
    out_specs = [_tok_t_spec(tm, w) if tr else _tok_spec(tm, w) for w, d, tr in outs]
    return pl.pallas_call(
        body, name=name,
        grid=(bsz, seq // tm),
        in_specs=in_specs, out_specs=out_specs, out_shape=out_shape,
        compiler_params=pltpu.CompilerParams(
            dimension_semantics=("arbitrary", "arbitrary"), vmem_limit_bytes=VMEM_LIMIT),
    )(*tok_inputs, *consts, *tabs)


def _attn_call(body, name, q, k, vt, extra, q_blk, kv_blk, n_blocks, kv_map, tq):
    bsz, seq, _ = q.shape
    in_specs = [
        pl.BlockSpec((None, tq, q_blk), lambda b, h, t: (b, t, h)),
        pl.BlockSpec((None, seq, kv_blk), lambda b, h, t: (b, 0, kv_map(h))),
        pl.BlockSpec((None, LANES, seq), lambda b, h, t: (b, kv_map(h), 0)),
    ] + [pl.BlockSpec(e.shape, lambda b, h, t: (0, 0)) for e in extra]
    return pl.pallas_call(
        body, name=name,
        grid=(bsz, n_blocks, seq // tq),
        in_specs=in_specs,
        out_specs=pl.BlockSpec((None, tq, LANES), lambda b, h, t: (b, t, h)),
        out_shape=jax.ShapeDtypeStruct((bsz, seq, n_blocks * LANES), BF16),
        compiler_params=pltpu.CompilerParams(
            dimension_semantics=("arbitrary", "arbitrary", "arbitrary"), vmem_limit_bytes=VMEM_LIMIT),
    )(q, k, vt, *extra)


def _rope_lane_tables(pos_a, pos_b):
    half = ROPE_DIM // 2
    inv = ROPE_THETA ** (-jnp.arange(0, ROPE_DIM, 2, dtype=F32) / ROPE_DIM)
    lane = np.arange(LANES)
    sign = jnp.asarray(np.where(lane % ROPE_DIM < half, -1.0, 1.0), F32)
    use_b = jnp.asarray((lane // ROPE_DIM) % 2 == 1)
    freq = inv[lane % half]
    ang_a = pos_a.astype(F32)[:, None] * freq[None, :]
    ang_b = pos_b.astype(F32)[:, None] * freq[None, :]
    cos = jnp.where(use_b[None, :], jnp.cos(ang_b), jnp.cos(ang_a))
    sin = jnp.where(use_b[None, :], jnp.sin(ang_b), jnp.sin(ang_a)) * sign[None, :]
    return cos, sin


def _group_sum_matrix(group):
    lane = np.arange(LANES)
    g = (lane[:, None] // group == lane[None, :] // group).astype(np.float32)
    return jnp.asarray(np.concatenate([g, g], axis=0), BF16)


def _row(v):
    return v.reshape(1, -1).astype(F32)


def kernel(x, ffn1_norm, ffn1_w_gate, ffn1_w_up, ffn1_w_down, mix_norm, ffn2_norm, ffn2_w_gate, ffn2_w_up, ffn2_w_down, ab_w_in, mla_q_lora_norm, mla_w_uq, mla_kv_lora_norm, mla_w_ukv, mla_q_norm, mla_k_norm, diff_q_norm, diff_k_norm, diff_lambda_q1, diff_lambda_k1, diff_lambda_q2, diff_lambda_k2, diff_subln, ab_w_out, c_w_in, c_q_norm, c_k_norm, c_w_out):
    bsz, seq, d = x.shape
    depth = ffn1_norm.shape[0]
    tm = _tile(seq, TOKEN_TILE)
    tq = _tile(seq, QUERY_TILE)

    pos = jnp.arange(seq, dtype=jnp.int32)
    cos_t, sin_t = _rope_lane_tables(pos, pos)
    cos_ax, sin_ax = _rope_lane_tables(pos // GRID_W, pos % GRID_W)
    lane = np.arange(LANES)
    mla_rot = jnp.asarray((lane >= MLA_NOPE) & (lane < MLA_NOPE + MLA_ROPE))
    cos_m = jnp.where(mla_rot[None, :], cos_t, 1.0)
    sin_m = jnp.where(mla_rot[None, :], sin_t, 0.0)
    gs_all, gs64, gs32 = _group_sum_matrix(LANES), _group_sum_matrix(C_HEAD), _group_sum_matrix(DIFF_HEAD)

    for i in range(depth):
        j = i // 2
        ffn1 = (_row(ffn1_norm[i]), ffn1_w_gate[i].astype(BF16), ffn1_w_up[i].astype(BF16),
                ffn1_w_down[i].astype(BF16))
        ffn2 = (_row(ffn2_norm[i]), ffn2_w_gate[i].astype(BF16), ffn2_w_up[i].astype(BF16),
                ffn2_w_down[i].astype(BF16))
        if i % 2 == 0:
            lam_init = 0.8 - 0.6 * math.exp(-0.3 * i)
            w = ab_w_in[j]
            kr_blk = jnp.pad(w[:, 384:416], ((0, 0), (MLA_NOPE, LANES - MLA_NOPE - MLA_ROPE)))
            w_in = jnp.concatenate([w[:, 0:384], kr_blk, w[:, 416:1440]], axis=1).astype(BF16)
            w_vt = w[:, 1440:].T.astype(BF16)
            w_uq = jnp.pad(mla_w_uq[j].reshape(MLA_Q_RANK, MLA_HEADS, MLA_NOPE + MLA_ROPE),
                           ((0, 0), (0, 0), (0, LANES - MLA_NOPE - MLA_ROPE))
                           ).reshape(MLA_Q_RANK, MLA_HEADS * LANES).astype(BF16)
            ukv = mla_w_ukv[j].reshape(MLA_KV_RANK, MLA_HEADS, MLA_NOPE + MLA_V)
            w_uk = jnp.pad(ukv[:, :, :MLA_NOPE], ((0, 0), (0, 0), (0, LANES - MLA_NOPE))
                           ).reshape(MLA_KV_RANK, MLA_HEADS * LANES).astype(BF16)
            w_uvt = ukv[:, :, MLA_NOPE:].reshape(MLA_KV_RANK, MLA_HEADS * MLA_V).T.astype(BF16)
            pad96 = (0, LANES - MLA_NOPE - MLA_ROPE)
            consts = ffn1 + (_row(mix_norm[i]), w_in, w_vt,
                             _row(mla_q_lora_norm[j]), w_uq, _row(mla_kv_lora_norm[j]), w_uk, w_uvt,
                             _row(jnp.pad(mla_q_norm[j], pad96)), _row(jnp.pad(mla_k_norm[j], pad96)),
                             _row(jnp.tile(diff_q_norm[j], LANES // DIFF_HEAD)),
                             _row(jnp.tile(diff_k_norm[j], LANES // DIFF_HEAD)),
                             gs_all, gs32)
            x, q, k, vt, dq, dk, dvt = _token_call(
                _even_in_kernel, f"even_in_{i}", x, consts, (cos_m, sin_m, cos_t, sin_t), (),
                ((d, F32, False), (1024, BF16, False), (1024, BF16, False), (512, BF16, True),
                 (512, BF16, False), (512, BF16, False), (512, BF16, True)), tm)
            o_mla = _attn_call(_mla_attn_kernel, f"mla_attn_{i}", q, k, vt, (),
                               2 * LANES, 2 * LANES, MLA_HEADS // 2, lambda h: h, tq)
            lam_p = jnp.stack([diff_lambda_q1[j], diff_lambda_k1[j],
                               diff_lambda_q2[j], diff_lambda_k2[j]]).astype(F32)
            sub = _row(jnp.tile(diff_subln[j], 2))
            o_diff = _attn_call(functools.partial(_diff_attn_kernel, lam_init), f"diff_attn_{i}",
                                dq, dk, dvt, (lam_p, sub), LANES, LANES, DIFF_HEADS // 2, lambda h: h, tq)
            wo = ab_w_out[j].astype(BF16)
            consts = (wo[:MLA_HEADS * MLA_V], wo[MLA_HEADS * MLA_V:]) + ffn2
            (x,) = _token_call(_even_out_kernel, f"even_out_{i}", x, consts, (), (o_mla, o_diff),
                               ((d, F32, False),), tm)
        else:
            w = c_w_in[j]
            wq = w[:, :C_HEADS * C_HEAD].reshape(d, 2, 2, C_GROUPS, C_HEAD)
            wq = wq.transpose(0, 1, 3, 2, 4).reshape(d, C_HEADS * C_HEAD)
            n_qk = (C_HEADS + C_KV_HEADS) * C_HEAD
            w_in = jnp.concatenate([wq, w[:, C_HEADS * C_HEAD:n_qk]], axis=1).astype(BF16)
            w_vt = w[:, n_qk:].T.astype(BF16)
            consts = ffn1 + (_row(mix_norm[i]), w_in, w_vt,
                             _row(jnp.tile(c_q_norm[j], 2)), _row(jnp.tile(c_k_norm[j], 2)), gs64)
            x, q, k, vt = _token_call(
                _odd_in_kernel, f"odd_in_{i}", x, consts, (cos_ax, sin_ax), (),
                ((d, F32, False), (1024, BF16, False), (256, BF16, False), (256, BF16, True)), tm)
            o = _attn_call(_gqa_attn_kernel, f"gqa_attn_{i}", q, k, vt, (),
                           LANES, LANES, C_HEADS // 2, lambda h: h // C_GROUPS, tq)
            wo = c_w_out[j].reshape(2, 2, C_GROUPS, C_HEAD, d).transpose(0, 2, 1, 3, 4)
            wo = wo.reshape(C_HEADS * C_HEAD, d).astype(BF16)
            consts = (wo,) + ffn2
            (x,) = _token_call(_odd_out_kernel, f"odd_out_{i}", x, consts, (), (o,),
                               ((d, F32, False),), tm)
    return x
```

```python
import functools
import math

import jax
import jax.numpy as jnp
import numpy as np
from jax import lax
from jax.experimental import pallas as pl
from jax.experimental.pallas import tpu as pltpu

F32 = jnp.float32
BF16 = jnp.bfloat16

LANES = 128
EPS = 1e-6
ROPE_THETA = 10000.0
GRID_W = 64
LOG2E = math.log2(math.e)

D_MODEL = 1024
D_FF = 2816
MLA_HEADS, MLA_Q_RANK, MLA_KV_RANK = 8, 256, 128
MLA_NOPE, MLA_ROPE, MLA_V = 64, 32, 64
DIFF_HEADS, DIFF_HEAD = 8, 32
C_HEADS, C_KV_HEADS, C_HEAD = 16, 4, 64
C_GROUPS = C_HEADS // C_KV_HEADS
ROPE_DIM = 32

VMEM_LIMIT = 56 * 1024 * 1024
TOKEN_TILE = 512
TOKEN_SPLIT = 2
QUERY_TILE = 512

NT_DIMS = (((1,), (1,)), ((), ()))
LOGIT_BOUND = 64.0
ONES_ROWS = 16


def _rms(x, g):
    ms = jnp.mean(x * x, axis=-1, keepdims=True)
    return x * lax.rsqrt(ms + EPS) * g


def _bdot(a, b):
    return jnp.dot(a.astype(BF16), b, preferred_element_type=F32)


def _swiglu_half(x, g, wg, wu, wd):
    h = _rms(x, g).astype(BF16)
    gate = jnp.dot(h, wg[...], preferred_element_type=F32)
    up = jnp.dot(h, wu[...], preferred_element_type=F32)
    act = (gate / (1.0 + jnp.exp(-gate))) * up
    return x + 0.5 * jnp.dot(act.astype(BF16), wd[...], preferred_element_type=F32)


def _lane_iota(rows):
    return lax.broadcasted_iota(jnp.int32, (rows, LANES), 1)


def _norm_rope_block(x, gsum, inv_count, gain, cos, sin, post_scale):
    lane = _lane_iota(x.shape[0])
    x2 = x * x
    hi = x2.astype(BF16)
    lo = (x2 - hi.astype(F32)).astype(BF16)
    ss = jnp.dot(jnp.concatenate([hi, lo], axis=1), gsum, preferred_element_type=F32)
    y = x * lax.rsqrt(ss * inv_count + EPS) * gain
    up = pltpu.roll(y, LANES - ROPE_DIM // 2, axis=1)
    dn = pltpu.roll(y, ROPE_DIM // 2, axis=1)
    partner = jnp.where((lane % ROPE_DIM) < ROPE_DIM // 2, up, dn)
    y = y * cos + partner * sin
    if post_scale != 1.0:
        y = y * post_scale
    return y


def _softmax_pv_t(qm, k, vt, bounded):
    st = lax.dot_general(k, qm, NT_DIMS, preferred_element_type=F32)
    if bounded:
        ones = jnp.ones((ONES_ROWS, vt.shape[1]), BF16)
        ot = jnp.dot(jnp.concatenate([vt, ones], axis=0), jnp.exp2(st).astype(BF16),
                     preferred_element_type=F32)
        return ot[:LANES] / ot[LANES:LANES + 1]
    m = jnp.max(st, axis=0, keepdims=True)
    p = jnp.exp2(st - m)
    l = jnp.sum(p, axis=0, keepdims=True)
    ot = jnp.dot(vt, p.astype(BF16), preferred_element_type=F32)
    return ot / l


def _row_groups(n_rows):
    step = n_rows // TOKEN_SPLIT
    return [slice(r * step, (r + 1) * step) for r in range(TOKEN_SPLIT)]


def _even_in_kernel(x_ref, g1_ref, wg_ref, wu_ref, wd_ref, gm_ref, win_ref, wvt_ref,
                    gq_ref, wuq_ref, gkv_ref, wuk_ref, wuvt_ref, qn_ref, kn_ref, dqn_ref, dkn_ref,
                    gs96_ref, gs32_ref, cm_ref, sm_ref, cd_ref, sd_ref,
                    xo_ref, q_ref, k_ref, vt_ref, dq_ref, dk_ref, dvt_ref):
    q_scale = (MLA_NOPE + MLA_ROPE) ** -0.5 * LOG2E
    d_scale = DIFF_HEAD ** -0.5 * LOG2E
    for rows in _row_groups(x_ref.shape[0]):
        x = _swiglu_half(x_ref[rows, :], g1_ref[...], wg_ref, wu_ref, wd_ref)
        xo_ref[rows, :] = x
        h = _rms(x, gm_ref[...]).astype(BF16)
        z = jnp.dot(h, win_ref[...], preferred_element_type=F32)
        dvt_ref[:, rows] = lax.dot_general(wvt_ref[...], h, NT_DIMS,
                                           preferred_element_type=F32).astype(BF16)
        kr = z[:, 384:512]
        cm, sm, cd, sd = cm_ref[rows, :], sm_ref[rows, :], cd_ref[rows, :], sd_ref[rows, :]
        ckv = _rms(z[:, 256:384], gkv_ref[...]).astype(BF16)
        qf = _bdot(_rms(z[:, 0:256], gq_ref[...]), wuq_ref[...])
        kf = jnp.dot(ckv, wuk_ref[...], preferred_element_type=F32)
        vt_ref[:, rows] = lax.dot_general(wuvt_ref[...], ckv, NT_DIMS,
                                          preferred_element_type=F32).astype(BF16)
        for hd in range(MLA_HEADS):
            sl = slice(hd * LANES, (hd + 1) * LANES)
            q_ref[rows, sl] = _norm_rope_block(qf[:, sl], gs96_ref[...], 1.0 / 96.0, qn_ref[...],
                                               cm, sm, q_scale).astype(BF16)
            k_ref[rows, sl] = _norm_rope_block(kf[:, sl] + kr, gs96_ref[...], 1.0 / 96.0, kn_ref[...],
                                               cm, sm, 1.0).astype(BF16)
        for b in range(4):
            sl = slice(b * LANES, (b + 1) * LANES)
            dq_ref[rows, sl] = _norm_rope_block(z[:, 512 + b * LANES:512 + (b + 1) * LANES],
                                                gs32_ref[...], 1.0 / DIFF_HEAD, dqn_ref[...],
                                                cd, sd, d_scale).astype(BF16)
            dk_ref[rows, sl] = _norm_rope_block(z[:, 1024 + b * LANES:1024 + (b + 1) * LANES],
                                                gs32_ref[...], 1.0 / DIFF_HEAD, dkn_ref[...],
                                                cd, sd, 1.0).astype(BF16)


def _odd_in_kernel(x_ref, g1_ref, wg_ref, wu_ref, wd_ref, gm_ref, win_ref, wvt_ref,
                   qn_ref, kn_ref, gs64_ref, ca_ref, sa_ref,
                   xo_ref, q_ref, k_ref, vt_ref):
    q_scale = C_HEAD ** -0.5 * LOG2E
    for rows in _row_groups(x_ref.shape[0]):
        x = _swiglu_half(x_ref[rows, :], g1_ref[...], wg_ref, wu_ref, wd_ref)
        xo_ref[rows, :] = x
        h = _rms(x, gm_ref[...]).astype(BF16)
        z = jnp.dot(h, win_ref[...], preferred_element_type=F32)
        vt_ref[:, rows] = lax.dot_general(wvt_ref[...], h, NT_DIMS,
                                          preferred_element_type=F32).astype(BF16)
        ca, sa = ca_ref[rows, :], sa_ref[rows, :]
        for b in range(8):
            sl = slice(b * LANES, (b + 1) * LANES)
            q_ref[rows, sl] = _norm_rope_block(z[:, sl], gs64_ref[...], 1.0 / C_HEAD, qn_ref[...],
                                               ca, sa, q_scale).astype(BF16)
        for b in range(2):
            sl = slice(b * LANES, (b + 1) * LANES)
            k_ref[rows, sl] = _norm_rope_block(z[:, 1024 + b * LANES:1024 + (b + 1) * LANES],
                                               gs64_ref[...], 1.0 / C_HEAD, kn_ref[...],
                                               ca, sa, 1.0).astype(BF16)


def _even_out_kernel(x_ref, oa_ref, ob_ref, woa_ref, wob_ref, g2_ref, wg_ref, wu_ref, wd_ref, xo_ref):
    for rows in _row_groups(x_ref.shape[0]):
        x = x_ref[rows, :]
        x = x + jnp.dot(oa_ref[rows, :], woa_ref[...], preferred_element_type=F32)
        x = x + jnp.dot(ob_ref[rows, :], wob_ref[...], preferred_element_type=F32)
        xo_ref[rows, :] = _swiglu_half(x, g2_ref[...], wg_ref, wu_ref, wd_ref)


def _odd_out_kernel(x_ref, o_ref, wo_ref, g2_ref, wg_ref, wu_ref, wd_ref, xo_ref):
    for rows in _row_groups(x_ref.shape[0]):
        x = x_ref[rows, :] + jnp.dot(o_ref[rows, :], wo_ref[...], preferred_element_type=F32)
        xo_ref[rows, :] = _swiglu_half(x, g2_ref[...], wg_ref, wu_ref, wd_ref)


def _pair_rows(ot0, ot1, split):
    return jnp.concatenate([ot0[:split], ot1[split:]], axis=0).T


def _mla_attn_kernel(bounded, q_ref, k_ref, vt_ref, o_ref):
    vt = vt_ref[...]
    ot0 = _softmax_pv_t(q_ref[:, 0:LANES], k_ref[:, 0:LANES], vt, bounded)
    ot1 = _softmax_pv_t(q_ref[:, LANES:2 * LANES], k_ref[:, LANES:2 * LANES], vt, bounded)
    o_ref[...] = _pair_rows(ot0, ot1, MLA_V).astype(BF16)


def _gqa_attn_kernel(bounded, q_ref, k_ref, vt_ref, o_ref):
    q, k, vt = q_ref[...], k_ref[...], vt_ref[...]
    lane = _lane_iota(q.shape[0])
    zero = jnp.zeros_like(q)
    ot0 = _softmax_pv_t(jnp.where(lane < C_HEAD, q, zero), k, vt, bounded)
    ot1 = _softmax_pv_t(jnp.where(lane >= C_HEAD, q, zero), k, vt, bounded)
    o_ref[...] = _pair_rows(ot0, ot1, C_HEAD).astype(BF16)


def _diff_attn_kernel(lam_init, bounded, q_ref, k_ref, vt_ref, lp_ref, sub_ref, o_ref):
    q, k, vt = q_ref[...], k_ref[...], vt_ref[...]
    lp = lp_ref[...]
    lam = (jnp.exp(jnp.sum(lp[0:1] * lp[1:2], axis=-1, keepdims=True))
           - jnp.exp(jnp.sum(lp[2:3] * lp[3:4], axis=-1, keepdims=True)) + lam_init)
    lane = _lane_iota(q.shape[0])
    zero = jnp.zeros_like(q)
    outs = []
    for hd in range(2):
        base = hd * 2 * DIFF_HEAD
        m1 = (lane >= base) & (lane < base + DIFF_HEAD)
        m2 = (lane >= base + DIFF_HEAD) & (lane < base + 2 * DIFF_HEAD)
        a1 = _softmax_pv_t(jnp.where(m1, q, zero), k, vt, bounded)
        a2 = _softmax_pv_t(jnp.where(m2, q, zero), k, vt, bounded)
        outs.append(a1 - lam * a2)
    o = _pair_rows(outs[0], outs[1], 2 * DIFF_HEAD)
    lo = lane < 2 * DIFF_HEAD
    o2 = o * o
    inv = 1.0 / (2 * DIFF_HEAD)
    r0 = lax.rsqrt(jnp.sum(jnp.where(lo, o2, 0.0), axis=-1, keepdims=True) * inv + EPS)
    r1 = lax.rsqrt(jnp.sum(jnp.where(lo, 0.0, o2), axis=-1, keepdims=True) * inv + EPS)
    o = o * jnp.where(lo, r0, r1) * sub_ref[...] * (1.0 - lam_init)
    o_ref[...] = o.astype(BF16)


def _tile(n, want):
    return want if n % want == 0 else n


def _const_spec(shape):
    nd = len(shape)
    return pl.BlockSpec(shape, lambda *_: (0,) * nd, pipeline_mode=pl.Buffered(1))


def _tok_spec(tm, width):
    return pl.BlockSpec((None, tm, width), lambda b, t: (b, t, 0))


def _tok_t_spec(tm, width):
    return pl.BlockSpec((None, width, tm), lambda b, t: (b, 0, t))


def _tab_spec(tm):
    return pl.BlockSpec((tm, LANES), lambda b, t: (t, 0))


def _token_call(body, name, x, consts, tabs, extra_tok, outs, tm):
    bsz, seq, _ = x.shape
    tok_inputs = [x] + list(extra_tok)
    in_specs = ([_tok_spec(tm, a.shape[-1]) for a in tok_inputs]
                + [_const_spec(c.shape) for c in consts]
                + [_tab_spec(tm) for _ in tabs])
    out_shape = [jax.ShapeDtypeStruct((bsz, w, seq) if tr else (bsz, seq, w), d) for w, d, tr in outs]
    out_specs = [_tok_t_spec(tm, w) if tr else _tok_spec(tm, w) for w, d, tr in outs]
    return pl.pallas_call(
        body, name=name,
        grid=(bsz, seq // tm),
        in_specs=in_specs, out_specs=out_specs, out_shape=out_shape,
        compiler_params=pltpu.CompilerParams(
            dimension_semantics=("arbitrary", "arbitrary"), vmem_limit_bytes=VMEM_LIMIT),
    )(*tok_inputs, *consts, *tabs)


def _attn_dispatch(body, name, logit_bound, *args):
    q, k, vt, extra, *static = args

    def run(bounded, q, k, vt, extra):
        tag = "bounded" if bounded else "shifted"
        return _attn_call(functools.partial(body, bounded), f"{name}_{tag}", q, k, vt, extra, *static)

    return lax.cond(logit_bound <= LOGIT_BOUND, functools.partial(run, True),
                    functools.partial(run, False), q, k, vt, tuple(extra))


def _attn_call(body, name, q, k, vt, extra, q_blk, kv_blk, n_blocks, kv_map, tq):
    bsz, seq, _ = q.shape
    in_specs = [
        pl.BlockSpec((None, tq, q_blk), lambda b, h, t: (b, t, h)),
        pl.BlockSpec((None, seq, kv_blk), lambda b, h, t: (b, 0, kv_map(h))),
        pl.BlockSpec((None, LANES, seq), lambda b, h, t: (b, kv_map(h), 0)),
    ] + [pl.BlockSpec(e.shape, lambda b, h, t: (0, 0)) for e in extra]
    return pl.pallas_call(
        body, name=name,
        grid=(bsz, n_blocks, seq // tq),
        in_specs=in_specs,
        out_specs=pl.BlockSpec((None, tq, LANES), lambda b, h, t: (b, t, h)),
        out_shape=jax.ShapeDtypeStruct((bsz, seq, n_blocks * LANES), BF16),
        compiler_params=pltpu.CompilerParams(
            dimension_semantics=("arbitrary", "arbitrary", "arbitrary"), vmem_limit_bytes=VMEM_LIMIT),
    )(q, k, vt, *extra)


def _rope_lane_tables(pos_a, pos_b):
    half = ROPE_DIM // 2
    inv = ROPE_THETA ** (-jnp.arange(0, ROPE_DIM, 2, dtype=F32) / ROPE_DIM)
    lane = np.arange(LANES)
    sign = jnp.asarray(np.where(lane % ROPE_DIM < half, -1.0, 1.0), F32)
    use_b = jnp.asarray((lane // ROPE_DIM) % 2 == 1)
    freq = inv[lane % half]
    ang_a = pos_a.astype(F32)[:, None] * freq[None, :]
    ang_b = pos_b.astype(F32)[:, None] * freq[None, :]
    cos = jnp.where(use_b[None, :], jnp.cos(ang_b), jnp.cos(ang_a))
    sin = jnp.where(use_b[None, :], jnp.sin(ang_b), jnp.sin(ang_a)) * sign[None, :]
    return cos, sin


def _group_sum_matrix(group):
    lane = np.arange(LANES)
    g = (lane[:, None] // group == lane[None, :] // group).astype(np.float32)
    return jnp.asarray(np.concatenate([g, g], axis=0), BF16)


def _row(v):
    return v.reshape(1, -1).astype(F32)


def _max_abs(v):
    return jnp.max(jnp.abs(v.astype(F32)))


def kernel(x, ffn1_norm, ffn1_w_gate, ffn1_w_up, ffn1_w_down, mix_norm, ffn2_norm, ffn2_w_gate, ffn2_w_up, ffn2_w_down, ab_w_in, mla_q_lora_norm, mla_w_uq, mla_kv_lora_norm, mla_w_ukv, mla_q_norm, mla_k_norm, diff_q_norm, diff_k_norm, diff_lambda_q1, diff_lambda_k1, diff_lambda_q2, diff_lambda_k2, diff_subln, ab_w_out, c_w_in, c_q_norm, c_k_norm, c_w_out):
    bsz, seq, d = x.shape
    depth = ffn1_norm.shape[0]
    tm = _tile(seq, TOKEN_TILE)
    tq = _tile(seq, QUERY_TILE)

    pos = jnp.arange(seq, dtype=jnp.int32)
    cos_t, sin_t = _rope_lane_tables(pos, pos)
    cos_ax, sin_ax = _rope_lane_tables(pos // GRID_W, pos % GRID_W)
    lane = np.arange(LANES)
    mla_rot = jnp.asarray((lane >= MLA_NOPE) & (lane < MLA_NOPE + MLA_ROPE))
    cos_m = jnp.where(mla_rot[None, :], cos_t, 1.0)
    sin_m = jnp.where(mla_rot[None, :], sin_t, 0.0)
    gs_all, gs64, gs32 = _group_sum_matrix(LANES), _group_sum_matrix(C_HEAD), _group_sum_matrix(DIFF_HEAD)

    for i in range(depth):
        j = i // 2
        ffn1 = (_row(ffn1_norm[i]), ffn1_w_gate[i].astype(BF16), ffn1_w_up[i].astype(BF16),
                ffn1_w_down[i].astype(BF16))
        ffn2 = (_row(ffn2_norm[i]), ffn2_w_gate[i].astype(BF16), ffn2_w_up[i].astype(BF16),
                ffn2_w_down[i].astype(BF16))
        if i % 2 == 0:
            lam_init = 0.8 - 0.6 * math.exp(-0.3 * i)
            w = ab_w_in[j]
            kr_blk = jnp.pad(w[:, 384:416], ((0, 0), (MLA_NOPE, LANES - MLA_NOPE - MLA_ROPE)))
            w_in = jnp.concatenate([w[:, 0:384], kr_blk, w[:, 416:1440]], axis=1).astype(BF16)
            w_vt = w[:, 1440:].T.astype(BF16)
            w_uq = jnp.pad(mla_w_uq[j].reshape(MLA_Q_RANK, MLA_HEADS, MLA_NOPE + MLA_ROPE),
                           ((0, 0), (0, 0), (0, LANES - MLA_NOPE - MLA_ROPE))
                           ).reshape(MLA_Q_RANK, MLA_HEADS * LANES).astype(BF16)
            ukv = mla_w_ukv[j].reshape(MLA_KV_RANK, MLA_HEADS, MLA_NOPE + MLA_V)
            w_uk = jnp.pad(ukv[:, :, :MLA_NOPE], ((0, 0), (0, 0), (0, LANES - MLA_NOPE))
                           ).reshape(MLA_KV_RANK, MLA_HEADS * LANES).astype(BF16)
            w_uvt = ukv[:, :, MLA_NOPE:].reshape(MLA_KV_RANK, MLA_HEADS * MLA_V).T.astype(BF16)
            pad96 = (0, LANES - MLA_NOPE - MLA_ROPE)
            consts = ffn1 + (_row(mix_norm[i]), w_in, w_vt,
                             _row(mla_q_lora_norm[j]), w_uq, _row(mla_kv_lora_norm[j]), w_uk, w_uvt,
                             _row(jnp.pad(mla_q_norm[j], pad96)), _row(jnp.pad(mla_k_norm[j], pad96)),
                             _row(jnp.tile(diff_q_norm[j], LANES // DIFF_HEAD)),
                             _row(jnp.tile(diff_k_norm[j], LANES // DIFF_HEAD)),
                             gs_all, gs32)
            x, q, k, vt, dq, dk, dvt = _token_call(
                _even_in_kernel, f"even_in_{i}", x, consts, (cos_m, sin_m, cos_t, sin_t), (),
                ((d, F32, False), (1024, BF16, False), (1024, BF16, False), (512, BF16, True),
                 (512, BF16, False), (512, BF16, False), (512, BF16, True)), tm)
            mla_dim = MLA_NOPE + MLA_ROPE
            mla_bound = (mla_dim ** 0.5 * LOG2E) * _max_abs(mla_q_norm[j]) * _max_abs(mla_k_norm[j])
            o_mla = _attn_dispatch(_mla_attn_kernel, f"mla_attn_{i}", mla_bound, q, k, vt, (),
                                   2 * LANES, 2 * LANES, MLA_HEADS // 2, lambda h: h, tq)
            lam_p = jnp.stack([diff_lambda_q1[j], diff_lambda_k1[j],
                               diff_lambda_q2[j], diff_lambda_k2[j]]).astype(F32)
            sub = _row(jnp.tile(diff_subln[j], 2))
            diff_bound = (DIFF_HEAD ** 0.5 * LOG2E) * _max_abs(diff_q_norm[j]) * _max_abs(diff_k_norm[j])
            o_diff = _attn_dispatch(functools.partial(_diff_attn_kernel, lam_init), f"diff_attn_{i}",
                                    diff_bound, dq, dk, dvt, (lam_p, sub),
                                    LANES, LANES, DIFF_HEADS // 2, lambda h: h, tq)
            wo = ab_w_out[j].astype(BF16)
            consts = (wo[:MLA_HEADS * MLA_V], wo[MLA_HEADS * MLA_V:]) + ffn2
            (x,) = _token_call(_even_out_kernel, f"even_out_{i}", x, consts, (), (o_mla, o_diff),
                               ((d, F32, False),), tm)
        else:
            w = c_w_in[j]
            wq = w[:, :C_HEADS * C_HEAD].reshape(d, 2, 2, C_GROUPS, C_HEAD)
            wq = wq.transpose(0, 1, 3, 2, 4).reshape(d, C_HEADS * C_HEAD)
            n_qk = (C_HEADS + C_KV_HEADS) * C_HEAD
            w_in = jnp.concatenate([wq, w[:, C_HEADS * C_HEAD:n_qk]], axis=1).astype(BF16)
            w_vt = w[:, n_qk:].T.astype(BF16)
            consts = ffn1 + (_row(mix_norm[i]), w_in, w_vt,
                             _row(jnp.tile(c_q_norm[j], 2)), _row(jnp.tile(c_k_norm[j], 2)), gs64)
            x, q, k, vt = _token_call(
                _odd_in_kernel, f"odd_in_{i}", x, consts, (cos_ax, sin_ax), (),
                ((d, F32, False), (1024, BF16, False), (256, BF16, False), (256, BF16, True)), tm)
            gqa_bound = (C_HEAD ** 0.5 * LOG2E) * _max_abs(c_q_norm[j]) * _max_abs(c_k_norm[j])
            o = _attn_dispatch(_gqa_attn_kernel, f"gqa_attn_{i}", gqa_bound, q, k, vt, (),
                               LANES, LANES, C_HEADS // 2, lambda h: h // C_GROUPS, tq)
            wo = c_w_out[j].reshape(2, 2, C_GROUPS, C_HEAD, d).transpose(0, 2, 1, 3, 4)
            wo = wo.reshape(C_HEADS * C_HEAD, d).astype(BF16)
            consts = (wo,) + ffn2
            (x,) = _token_call(_odd_out_kernel, f"odd_out_{i}", x, consts, (), (o,),
                               ((d, F32, False),), tm)
    return x
```

```python
import functools
import math

import jax
import jax.numpy as jnp
import numpy as np
from jax import lax
from jax.experimental import pallas as pl
from jax.experimental.pallas import tpu as pltpu

F32 = jnp.float32
BF16 = jnp.bfloat16

LANES = 128
EPS = 1e-6
ROPE_THETA = 10000.0
GRID_W = 64
LOG2E = math.log2(math.e)

D_MODEL = 1024
D_FF = 2816
MLA_HEADS, MLA_Q_RANK, MLA_KV_RANK = 8, 256, 128
MLA_NOPE, MLA_ROPE, MLA_V = 64, 32, 64
DIFF_HEADS, DIFF_HEAD = 8, 32
C_HEADS, C_KV_HEADS, C_HEAD = 16, 4, 64
C_GROUPS = C_HEADS // C_KV_HEADS
ROPE_DIM = 32

VMEM_LIMIT = 56 * 1024 * 1024
TOKEN_TILE = 512
TOKEN_SPLIT = 2
QUERY_TILE_BOUNDED = 1024
QUERY_TILE_SHIFTED = 512

NT_DIMS = (((1,), (1,)), ((), ()))
LOGIT_BOUND = 64.0
HEAD_V = 64


def _rms(x, g):
    ms = jnp.mean(x * x, axis=-1, keepdims=True)
    return x * lax.rsqrt(ms + EPS) * g


def _bdot(a, b):
    return jnp.dot(a.astype(BF16), b, preferred_element_type=F32)


def _swiglu_half(x, g, wg, wu, wd):
    h = _rms(x, g).astype(BF16)
    gate = jnp.dot(h, wg[...], preferred_element_type=F32)
    up = jnp.dot(h, wu[...], preferred_element_type=F32)
    act = (gate / (1.0 + jnp.exp(-gate))) * up
    return x + 0.5 * jnp.dot(act.astype(BF16), wd[...], preferred_element_type=F32)


def _lane_iota(rows):
    return lax.broadcasted_iota(jnp.int32, (rows, LANES), 1)


def _norm_rope_block(x, gsum, inv_count, gain, cos, sin, post_scale):
    lane = _lane_iota(x.shape[0])
    x2 = x * x
    hi = x2.astype(BF16)
    lo = (x2 - hi.astype(F32)).astype(BF16)
    ss = jnp.dot(jnp.concatenate([hi, lo], axis=1), gsum, preferred_element_type=F32)
    y = x * lax.rsqrt(ss * inv_count + EPS) * gain
    up = pltpu.roll(y, LANES - ROPE_DIM // 2, axis=1)
    dn = pltpu.roll(y, ROPE_DIM // 2, axis=1)
    partner = jnp.where((lane % ROPE_DIM) < ROPE_DIM // 2, up, dn)
    y = y * cos + partner * sin
    if post_scale != 1.0:
        y = y * post_scale
    return y


def _softmax_pv_t(qm, k, vt, bounded):
    st = lax.dot_general(k, qm, NT_DIMS, preferred_element_type=F32)
    if not bounded:
        st = st - jnp.max(st, axis=0, keepdims=True)
    ot = jnp.dot(vt, jnp.exp2(st).astype(BF16), preferred_element_type=F32)
    return ot[:HEAD_V] / ot[HEAD_V:HEAD_V + 1]


def _store_vt_with_ones(vt_ref, cols, vt):
    ones = jnp.ones((HEAD_V, vt.shape[1]), BF16)
    for hd in range(vt.shape[0] // HEAD_V):
        vt_ref[2 * hd * HEAD_V:(2 * hd + 1) * HEAD_V, cols] = vt[hd * HEAD_V:(hd + 1) * HEAD_V].astype(BF16)
        vt_ref[(2 * hd + 1) * HEAD_V:(2 * hd + 2) * HEAD_V, cols] = ones


def _row_groups(n_rows):
    step = n_rows // TOKEN_SPLIT
    return [slice(r * step, (r + 1) * step) for r in range(TOKEN_SPLIT)]


def _even_in_kernel(x_ref, g1_ref, wg_ref, wu_ref, wd_ref, gm_ref, win_ref, wvt_ref,
                    gq_ref, wuq_ref, gkv_ref, wuk_ref, wuvt_ref, qn_ref, kn_ref, dqn_ref, dkn_ref,
                    gs96_ref, gs32_ref, cm_ref, sm_ref, cd_ref, sd_ref,
                    xo_ref, q_ref, k_ref, vt_ref, dq_ref, dk_ref, dvt_ref):
    q_scale = (MLA_NOPE + MLA_ROPE) ** -0.5 * LOG2E
    d_scale = DIFF_HEAD ** -0.5 * LOG2E
    for rows in _row_groups(x_ref.shape[0]):
        x = _swiglu_half(x_ref[rows, :], g1_ref[...], wg_ref, wu_ref, wd_ref)
        xo_ref[rows, :] = x
        h = _rms(x, gm_ref[...]).astype(BF16)
        z = jnp.dot(h, win_ref[...], preferred_element_type=F32)
        _store_vt_with_ones(dvt_ref, rows, lax.dot_general(wvt_ref[...], h, NT_DIMS,
                                                           preferred_element_type=F32))
        kr = z[:, 384:512]
        cm, sm, cd, sd = cm_ref[rows, :], sm_ref[rows, :], cd_ref[rows, :], sd_ref[rows, :]
        ckv = _rms(z[:, 256:384], gkv_ref[...]).astype(BF16)
        qf = _bdot(_rms(z[:, 0:256], gq_ref[...]), wuq_ref[...])
        kf = jnp.dot(ckv, wuk_ref[...], preferred_element_type=F32)
        _store_vt_with_ones(vt_ref, rows, lax.dot_general(wuvt_ref[...], ckv, NT_DIMS,
                                                          preferred_element_type=F32))
        for hd in range(MLA_HEADS):
            sl = slice(hd * LANES, (hd + 1) * LANES)
            q_ref[rows, sl] = _norm_rope_block(qf[:, sl], gs96_ref[...], 1.0 / (MLA_NOPE + MLA_ROPE), qn_ref[...],
                                               cm, sm, q_scale).astype(BF16)
            k_ref[rows, sl] = _norm_rope_block(kf[:, sl] + kr, gs96_ref[...], 1.0 / (MLA_NOPE + MLA_ROPE), kn_ref[...],
                                               cm, sm, 1.0).astype(BF16)
        for b in range(4):
            sl = slice(b * LANES, (b + 1) * LANES)
            dq_ref[rows, sl] = _norm_rope_block(z[:, 512 + b * LANES:512 + (b + 1) * LANES],
                                                gs32_ref[...], 1.0 / DIFF_HEAD, dqn_ref[...],
                                                cd, sd, d_scale).astype(BF16)
            dk_ref[rows, sl] = _norm_rope_block(z[:, 1024 + b * LANES:1024 + (b + 1) * LANES],
                                                gs32_ref[...], 1.0 / DIFF_HEAD, dkn_ref[...],
                                                cd, sd, 1.0).astype(BF16)


def _odd_in_kernel(x_ref, g1_ref, wg_ref, wu_ref, wd_ref, gm_ref, win_ref, wvt_ref,
                   qn_ref, kn_ref, gs64_ref, ca_ref, sa_ref,
                   xo_ref, q_ref, k_ref, vt_ref):
    q_scale = C_HEAD ** -0.5 * LOG2E
    for rows in _row_groups(x_ref.shape[0]):
        x = _swiglu_half(x_ref[rows, :], g1_ref[...], wg_ref, wu_ref, wd_ref)
        xo_ref[rows, :] = x
        h = _rms(x, gm_ref[...]).astype(BF16)
        z = jnp.dot(h, win_ref[...], preferred_element_type=F32)
        _store_vt_with_ones(vt_ref, rows, lax.dot_general(wvt_ref[...], h, NT_DIMS,
                                                          preferred_element_type=F32))
        ca, sa = ca_ref[rows, :], sa_ref[rows, :]
        for b in range(8):
            sl = slice(b * LANES, (b + 1) * LANES)
            q_ref[rows, sl] = _norm_rope_block(z[:, sl], gs64_ref[...], 1.0 / C_HEAD, qn_ref[...],
                                               ca, sa, q_scale).astype(BF16)
        for b in range(2):
            sl = slice(b * LANES, (b + 1) * LANES)
            k_ref[rows, sl] = _norm_rope_block(z[:, 1024 + b * LANES:1024 + (b + 1) * LANES],
                                               gs64_ref[...], 1.0 / C_HEAD, kn_ref[...],
                                               ca, sa, 1.0).astype(BF16)


def _even_out_kernel(x_ref, oa_ref, ob_ref, woa_ref, wob_ref, g2_ref, wg_ref, wu_ref, wd_ref, xo_ref):
    for rows in _row_groups(x_ref.shape[0]):
        x = x_ref[rows, :]
        x = x + jnp.dot(oa_ref[rows, :], woa_ref[...], preferred_element_type=F32)
        x = x + jnp.dot(ob_ref[rows, :], wob_ref[...], preferred_element_type=F32)
        xo_ref[rows, :] = _swiglu_half(x, g2_ref[...], wg_ref, wu_ref, wd_ref)


def _odd_out_kernel(x_ref, o_ref, wo_ref, g2_ref, wg_ref, wu_ref, wd_ref, xo_ref):
    for rows in _row_groups(x_ref.shape[0]):
        x = x_ref[rows, :] + jnp.dot(o_ref[rows, :], wo_ref[...], preferred_element_type=F32)
        xo_ref[rows, :] = _swiglu_half(x, g2_ref[...], wg_ref, wu_ref, wd_ref)


def _pair_rows(ot0, ot1):
    return jnp.concatenate([ot0, ot1], axis=0).T


def _mla_attn_kernel(bounded, q_ref, k_ref, vt_ref, o_ref):
    ot0 = _softmax_pv_t(q_ref[:, 0:LANES], k_ref[:, 0:LANES], vt_ref[0:LANES, :], bounded)
    ot1 = _softmax_pv_t(q_ref[:, LANES:2 * LANES], k_ref[:, LANES:2 * LANES],
                        vt_ref[LANES:2 * LANES, :], bounded)
    o_ref[...] = _pair_rows(ot0, ot1).astype(BF16)


def _gqa_attn_kernel(bounded, q_ref, k_ref, vt_ref, o_ref):
    q, k = q_ref[...], k_ref[...]
    lane = _lane_iota(q.shape[0])
    zero = jnp.zeros_like(q)
    ot0 = _softmax_pv_t(jnp.where(lane < C_HEAD, q, zero), k, vt_ref[0:LANES, :], bounded)
    ot1 = _softmax_pv_t(jnp.where(lane >= C_HEAD, q, zero), k, vt_ref[LANES:2 * LANES, :], bounded)
    o_ref[...] = _pair_rows(ot0, ot1).astype(BF16)


def _diff_attn_kernel(lam_init, bounded, q_ref, k_ref, vt_ref, lp_ref, sub_ref, o_ref):
    q, k = q_ref[...], k_ref[...]
    lp = lp_ref[...]
    lam = (jnp.exp(jnp.sum(lp[0:1] * lp[1:2], axis=-1, keepdims=True))
           - jnp.exp(jnp.sum(lp[2:3] * lp[3:4], axis=-1, keepdims=True)) + lam_init)
    lane = _lane_iota(q.shape[0])
    zero = jnp.zeros_like(q)
    outs = []
    for hd in range(2):
        base = hd * 2 * DIFF_HEAD
        m1 = (lane >= base) & (lane < base + DIFF_HEAD)
        m2 = (lane >= base + DIFF_HEAD) & (lane < base + 2 * DIFF_HEAD)
        vt = vt_ref[hd * LANES:(hd + 1) * LANES, :]
        a1 = _softmax_pv_t(jnp.where(m1, q, zero), k, vt, bounded)
        a2 = _softmax_pv_t(jnp.where(m2, q, zero), k, vt, bounded)
        outs.append(a1 - lam * a2)
    o = _pair_rows(outs[0], outs[1])
    lo = lane < 2 * DIFF_HEAD
    o2 = o * o
    inv = 1.0 / (2 * DIFF_HEAD)
    r0 = lax.rsqrt(jnp.sum(jnp.where(lo, o2, 0.0), axis=-1, keepdims=True) * inv + EPS)
    r1 = lax.rsqrt(jnp.sum(jnp.where(lo, 0.0, o2), axis=-1, keepdims=True) * inv + EPS)
    o = o * jnp.where(lo, r0, r1) * sub_ref[...] * (1.0 - lam_init)
    o_ref[...] = o.astype(BF16)


def _tile(n, want):
    return want if n % want == 0 else n


def _const_spec(shape):
    nd = len(shape)
    return pl.BlockSpec(shape, lambda *_: (0,) * nd, pipeline_mode=pl.Buffered(1))


def _tok_spec(tm, width):
    return pl.BlockSpec((None, tm, width), lambda b, t: (b, t, 0))


def _tok_t_spec(tm, width):
    return pl.BlockSpec((None, width, tm), lambda b, t: (b, 0, t))


def _tab_spec(tm):
    return pl.BlockSpec((tm, LANES), lambda b, t: (t, 0))


def _token_call(body, name, x, consts, tabs, extra_tok, outs, tm):
    bsz, seq, _ = x.shape
    tok_inputs = [x] + list(extra_tok)
    in_specs = ([_tok_spec(tm, a.shape[-1]) for a in tok_inputs]
                + [_const_spec(c.shape) for c in consts]
                + [_tab_spec(tm) for _ in tabs])
    out_shape = [jax.ShapeDtypeStruct((bsz, w, seq) if tr else (bsz, seq, w), d) for w, d, tr in outs]
    out_specs = [_tok_t_spec(tm, w) if tr else _tok_spec(tm, w) for w, d, tr in outs]
    return pl.pallas_call(
        body, name=name,
        grid=(bsz, seq // tm),
        in_specs=in_specs, out_specs=out_specs, out_shape=out_shape,
        compiler_params=pltpu.CompilerParams(
            dimension_semantics=("arbitrary", "arbitrary"), vmem_limit_bytes=VMEM_LIMIT),
    )(*tok_inputs, *consts, *tabs)


def _attn_dispatch(body, name, logit_bound, *args):
    q, k, vt, extra, *static = args

    def run(bounded, q, k, vt, extra):
        tag = "bounded" if bounded else "shifted"
        tq = _tile(q.shape[1], QUERY_TILE_BOUNDED if bounded else QUERY_TILE_SHIFTED)
        return _attn_call(functools.partial(body, bounded), f"{name}_{tag}", q, k, vt, extra, *static, tq)

    return lax.cond(logit_bound <= LOGIT_BOUND, functools.partial(run, True),
                    functools.partial(run, False), q, k, vt, tuple(extra))


def _attn_call(body, name, q, k, vt, extra, q_blk, kv_blk, n_blocks, kv_map, tq):
    bsz, seq, _ = q.shape
    in_specs = [
        pl.BlockSpec((None, tq, q_blk), lambda b, h, t: (b, t, h)),
        pl.BlockSpec((None, seq, kv_blk), lambda b, h, t: (b, 0, kv_map(h))),
        pl.BlockSpec((None, 2 * LANES, seq), lambda b, h, t: (b, kv_map(h), 0)),
    ] + [pl.BlockSpec(e.shape, lambda b, h, t: (0, 0)) for e in extra]
    return pl.pallas_call(
        body, name=name,
        grid=(bsz, n_blocks, seq // tq),
        in_specs=in_specs,
        out_specs=pl.BlockSpec((None, tq, LANES), lambda b, h, t: (b, t, h)),
        out_shape=jax.ShapeDtypeStruct((bsz, seq, n_blocks * LANES), BF16),
        compiler_params=pltpu.CompilerParams(
            dimension_semantics=("arbitrary", "arbitrary", "arbitrary"), vmem_limit_bytes=VMEM_LIMIT),
    )(q, k, vt, *extra)


def _rope_lane_tables(pos_a, pos_b):
    half = ROPE_DIM // 2
    inv = ROPE_THETA ** (-jnp.arange(0, ROPE_DIM, 2, dtype=F32) / ROPE_DIM)
    lane = np.arange(LANES)
    sign = jnp.asarray(np.where(lane % ROPE_DIM < half, -1.0, 1.0), F32)
    use_b = jnp.asarray((lane // ROPE_DIM) % 2 == 1)
    freq = inv[lane % half]
    ang_a = pos_a.astype(F32)[:, None] * freq[None, :]
    ang_b = pos_b.astype(F32)[:, None] * freq[None, :]
    cos = jnp.where(use_b[None, :], jnp.cos(ang_b), jnp.cos(ang_a))
    sin = jnp.where(use_b[None, :], jnp.sin(ang_b), jnp.sin(ang_a)) * sign[None, :]
    return cos, sin


def _group_sum_matrix(group):
    lane = np.arange(LANES)
    g = (lane[:, None] // group == lane[None, :] // group).astype(np.float32)
    return jnp.asarray(np.concatenate([g, g], axis=0), BF16)


def _row(v):
    return v.reshape(1, -1).astype(F32)


def _max_abs(v):
    return jnp.max(jnp.abs(v.astype(F32)))


def kernel(x, ffn1_norm, ffn1_w_gate, ffn1_w_up, ffn1_w_down, mix_norm, ffn2_norm, ffn2_w_gate, ffn2_w_up, ffn2_w_down, ab_w_in, mla_q_lora_norm, mla_w_uq, mla_kv_lora_norm, mla_w_ukv, mla_q_norm, mla_k_norm, diff_q_norm, diff_k_norm, diff_lambda_q1, diff_lambda_k1, diff_lambda_q2, diff_lambda_k2, diff_subln, ab_w_out, c_w_in, c_q_norm, c_k_norm, c_w_out):
    bsz, seq, d = x.shape
    depth = ffn1_norm.shape[0]
    tm = _tile(seq, TOKEN_TILE)

    pos = jnp.arange(seq, dtype=jnp.int32)
    cos_t, sin_t = _rope_lane_tables(pos, pos)
    cos_ax, sin_ax = _rope_lane_tables(pos // GRID_W, pos % GRID_W)
    lane = np.arange(LANES)
    mla_rot = jnp.asarray((lane >= MLA_NOPE) & (lane < MLA_NOPE + MLA_ROPE))
    cos_m = jnp.where(mla_rot[None, :], cos_t, 1.0)
    sin_m = jnp.where(mla_rot[None, :], sin_t, 0.0)
    gs_all, gs64, gs32 = _group_sum_matrix(LANES), _group_sum_matrix(C_HEAD), _group_sum_matrix(DIFF_HEAD)

    for i in range(depth):
        j = i // 2
        ffn1 = (_row(ffn1_norm[i]), ffn1_w_gate[i].astype(BF16), ffn1_w_up[i].astype(BF16),
                ffn1_w_down[i].astype(BF16))
        ffn2 = (_row(ffn2_norm[i]), ffn2_w_gate[i].astype(BF16), ffn2_w_up[i].astype(BF16),
                ffn2_w_down[i].astype(BF16))
        if i % 2 == 0:
            lam_init = 0.8 - 0.6 * math.exp(-0.3 * i)
            w = ab_w_in[j]
            kr_blk = jnp.pad(w[:, 384:416], ((0, 0), (MLA_NOPE, LANES - MLA_NOPE - MLA_ROPE)))
            w_in = jnp.concatenate([w[:, 0:384], kr_blk, w[:, 416:1440]], axis=1).astype(BF16)
            w_vt = w[:, 1440:].T.astype(BF16)
            w_uq = jnp.pad(mla_w_uq[j].reshape(MLA_Q_RANK, MLA_HEADS, MLA_NOPE + MLA_ROPE),
                           ((0, 0), (0, 0), (0, LANES - MLA_NOPE - MLA_ROPE))
                           ).reshape(MLA_Q_RANK, MLA_HEADS * LANES).astype(BF16)
            ukv = mla_w_ukv[j].reshape(MLA_KV_RANK, MLA_HEADS, MLA_NOPE + MLA_V)
            w_uk = jnp.pad(ukv[:, :, :MLA_NOPE], ((0, 0), (0, 0), (0, LANES - MLA_NOPE))
                           ).reshape(MLA_KV_RANK, MLA_HEADS * LANES).astype(BF16)
            w_uvt = ukv[:, :, MLA_NOPE:].reshape(MLA_KV_RANK, MLA_HEADS * MLA_V).T.astype(BF16)
            pad96 = (0, LANES - MLA_NOPE - MLA_ROPE)
            consts = ffn1 + (_row(mix_norm[i]), w_in, w_vt,
                             _row(mla_q_lora_norm[j]), w_uq, _row(mla_kv_lora_norm[j]), w_uk, w_uvt,
                             _row(jnp.pad(mla_q_norm[j], pad96)), _row(jnp.pad(mla_k_norm[j], pad96)),
                             _row(jnp.tile(diff_q_norm[j], LANES // DIFF_HEAD)),
                             _row(jnp.tile(diff_k_norm[j], LANES // DIFF_HEAD)),
                             gs_all, gs32)
            x, q, k, vt, dq, dk, dvt = _token_call(
                _even_in_kernel, f"even_in_{i}", x, consts, (cos_m, sin_m, cos_t, sin_t), (),
                ((d, F32, False), (1024, BF16, False), (1024, BF16, False),
                 (2 * HEAD_V * MLA_HEADS, BF16, True), (512, BF16, False), (512, BF16, False),
                 (2 * HEAD_V * DIFF_HEADS, BF16, True)), tm)
            mla_dim = MLA_NOPE + MLA_ROPE
            mla_bound = (mla_dim ** 0.5 * LOG2E) * _max_abs(mla_q_norm[j]) * _max_abs(mla_k_norm[j])
            o_mla = _attn_dispatch(_mla_attn_kernel, f"mla_attn_{i}", mla_bound, q, k, vt, (),
                                   2 * LANES, 2 * LANES, MLA_HEADS // 2, lambda h: h)
            lam_p = jnp.stack([diff_lambda_q1[j], diff_lambda_k1[j],
                               diff_lambda_q2[j], diff_lambda_k2[j]]).astype(F32)
            sub = _row(jnp.tile(diff_subln[j], 2))
            diff_bound = (DIFF_HEAD ** 0.5 * LOG2E) * _max_abs(diff_q_norm[j]) * _max_abs(diff_k_norm[j])
            o_diff = _attn_dispatch(functools.partial(_diff_attn_kernel, lam_init), f"diff_attn_{i}",
                                    diff_bound, dq, dk, dvt, (lam_p, sub),
                                    LANES, LANES, DIFF_HEADS // 2, lambda h: h)
            wo = ab_w_out[j].astype(BF16)
            consts = (wo[:MLA_HEADS * MLA_V], wo[MLA_HEADS * MLA_V:]) + ffn2
            (x,) = _token_call(_even_out_kernel, f"even_out_{i}", x, consts, (), (o_mla, o_diff),
                               ((d, F32, False),), tm)
        else:
            w = c_w_in[j]
            wq = w[:, :C_HEADS * C_HEAD].reshape(d, 2, 2, C_GROUPS, C_HEAD)
            wq = wq.transpose(0, 1, 3, 2, 4).reshape(d, C_HEADS * C_HEAD)
            n_qk = (C_HEADS + C_KV_HEADS) * C_HEAD
            w_in = jnp.concatenate([wq, w[:, C_HEADS * C_HEAD:n_qk]], axis=1).astype(BF16)
            w_vt = w[:, n_qk:].T.astype(BF16)
            consts = ffn1 + (_row(mix_norm[i]), w_in, w_vt,
                             _row(jnp.tile(c_q_norm[j], 2)), _row(jnp.tile(c_k_norm[j], 2)), gs64)
            x, q, k, vt = _token_call(
                _odd_in_kernel, f"odd_in_{i}", x, consts, (cos_ax, sin_ax), (),
                ((d, F32, False), (1024, BF16, False), (256, BF16, False),
                 (2 * HEAD_V * C_KV_HEADS, BF16, True)), tm)
            gqa_bound = (C_HEAD ** 0.5 * LOG2E) * _max_abs(c_q_norm[j]) * _max_abs(c_k_norm[j])
            o = _attn_dispatch(_gqa_attn_kernel, f"gqa_attn_{i}", gqa_bound, q, k, vt, (),
                               LANES, LANES, C_HEADS // 2, lambda h: h // C_GROUPS)
            wo = c_w_out[j].reshape(2, 2, C_GROUPS, C_HEAD, d).transpose(0, 2, 1, 3, 4)
            wo = wo.reshape(C_HEADS * C_HEAD, d).astype(BF16)
            consts = (wo,) + ffn2
            (x,) = _token_call(_odd_out_kernel, f"odd_out_{i}", x, consts, (), (o,),
                               ((d, F32, False),), tm)
    return x
```

```python
import functools
import math

import jax
import jax.numpy as jnp
import numpy as np
from jax import lax
from jax.experimental import pallas as pl
from jax.experimental.pallas import tpu as pltpu

F32 = jnp.float32
BF16 = jnp.bfloat16

LANES = 128
EPS = 1e-6
ROPE_THETA = 10000.0
GRID_W = 64
LOG2E = math.log2(math.e)

D_MODEL = 1024
D_FF = 2816
MLA_HEADS, MLA_Q_RANK, MLA_KV_RANK = 8, 256, 128
MLA_NOPE, MLA_ROPE, MLA_V = 64, 32, 64
DIFF_HEADS, DIFF_HEAD = 8, 32
C_HEADS, C_KV_HEADS, C_HEAD = 16, 4, 64
C_GROUPS = C_HEADS // C_KV_HEADS
ROPE_DIM = 32

VMEM_LIMIT = 56 * 1024 * 1024
TOKEN_TILE = 512
TOKEN_SPLIT = 2
QUERY_TILE_BOUNDED = 1024
QUERY_TILE_SHIFTED = 512
OUT_BLOCKS_BOUNDED = 2
OUT_BLOCKS_SHIFTED = 1
NT_DIMS = (((1,), (1,)), ((), ()))
LOGIT_BOUND = 64.0
HEAD_V = 64
ONES_ROWS = 16
VT_ROWS = HEAD_V + ONES_ROWS


def _rms(x, g):
    ms = jnp.mean(x * x, axis=-1, keepdims=True)
    return x * lax.rsqrt(ms + EPS) * g


def _bdot(a, b):
    return jnp.dot(a.astype(BF16), b, preferred_element_type=F32)


def _swiglu_half(x, g, wg, wu, wd):
    h = _rms(x, g).astype(BF16)
    gate = jnp.dot(h, wg[...], preferred_element_type=F32)
    up = jnp.dot(h, wu[...], preferred_element_type=F32)
    act = (gate / (1.0 + jnp.exp(-gate))) * up
    return x + 0.5 * jnp.dot(act.astype(BF16), wd[...], preferred_element_type=F32)


def _lane_iota(rows):
    return lax.broadcasted_iota(jnp.int32, (rows, LANES), 1)


def _norm_rope_block(x, gsum, inv_count, gain, cos, sin, post_scale):
    lane = _lane_iota(x.shape[0])
    x2 = x * x
    hi = x2.astype(BF16)
    lo = (x2 - hi.astype(F32)).astype(BF16)
    ss = jnp.dot(jnp.concatenate([hi, lo], axis=1), gsum, preferred_element_type=F32)
    y = x * lax.rsqrt(ss * inv_count + EPS) * gain
    up = pltpu.roll(y, LANES - ROPE_DIM // 2, axis=1)
    dn = pltpu.roll(y, ROPE_DIM // 2, axis=1)
    partner = jnp.where((lane % ROPE_DIM) < ROPE_DIM // 2, up, dn)
    y = y * cos + partner * sin
    if post_scale != 1.0:
        y = y * post_scale
    return y


def _softmax_pv_t(qm, k, vt, bounded):
    st = lax.dot_general(k, qm, NT_DIMS, preferred_element_type=F32)
    if not bounded:
        st = st - jnp.max(st, axis=0, keepdims=True)
    ot = jnp.dot(vt, jnp.exp2(st).astype(BF16), preferred_element_type=F32)
    return ot[:HEAD_V] / ot[HEAD_V:HEAD_V + 1]


def _store_vt_with_ones(vt_ref, cols, vt):
    ones = jnp.ones((ONES_ROWS, vt.shape[1]), BF16)
    for hd in range(vt.shape[0] // HEAD_V):
        vt_ref[hd * VT_ROWS:hd * VT_ROWS + HEAD_V, cols] = vt[hd * HEAD_V:(hd + 1) * HEAD_V].astype(BF16)
        vt_ref[hd * VT_ROWS + HEAD_V:(hd + 1) * VT_ROWS, cols] = ones


def _row_groups(n_rows):
    step = n_rows // TOKEN_SPLIT
    return [slice(r * step, (r + 1) * step) for r in range(TOKEN_SPLIT)]


def _even_in_kernel(x_ref, g1_ref, wg_ref, wu_ref, wd_ref, gm_ref, win_ref, wvt_ref,
                    gq_ref, wuq_ref, gkv_ref, wuk_ref, wuvt_ref, qn_ref, kn_ref, dqn_ref, dkn_ref,
                    gs96_ref, gs32_ref, cm_ref, sm_ref, cd_ref, sd_ref,
                    xo_ref, q_ref, k_ref, vt_ref, dq_ref, dk_ref, dvt_ref):
    q_scale = (MLA_NOPE + MLA_ROPE) ** -0.5 * LOG2E
    d_scale = DIFF_HEAD ** -0.5 * LOG2E
    for rows in _row_groups(x_ref.shape[0]):
        x = _swiglu_half(x_ref[rows, :], g1_ref[...], wg_ref, wu_ref, wd_ref)
        xo_ref[rows, :] = x
        h = _rms(x, gm_ref[...]).astype(BF16)
        z = jnp.dot(h, win_ref[...], preferred_element_type=F32)
        _store_vt_with_ones(dvt_ref, rows, lax.dot_general(wvt_ref[...], h, NT_DIMS,
                                                           preferred_element_type=F32))
        kr = z[:, 384:512]
        cm, sm, cd, sd = cm_ref[rows, :], sm_ref[rows, :], cd_ref[rows, :], sd_ref[rows, :]
        ckv = _rms(z[:, 256:384], gkv_ref[...]).astype(BF16)
        qf = _bdot(_rms(z[:, 0:256], gq_ref[...]), wuq_ref[...])
        kf = jnp.dot(ckv, wuk_ref[...], preferred_element_type=F32)
        _store_vt_with_ones(vt_ref, rows, lax.dot_general(wuvt_ref[...], ckv, NT_DIMS,
                                                          preferred_element_type=F32))
        for hd in range(MLA_HEADS):
            sl = slice(hd * LANES, (hd + 1) * LANES)
            q_ref[rows, sl] = _norm_rope_block(qf[:, sl], gs96_ref[...], 1.0 / (MLA_NOPE + MLA_ROPE), qn_ref[...],
                                               cm, sm, q_scale).astype(BF16)
            k_ref[rows, sl] = _norm_rope_block(kf[:, sl] + kr, gs96_ref[...], 1.0 / (MLA_NOPE + MLA_ROPE), kn_ref[...],
                                               cm, sm, 1.0).astype(BF16)
        for b in range(4):
            sl = slice(b * LANES, (b + 1) * LANES)
            dq_ref[rows, sl] = _norm_rope_block(z[:, 512 + b * LANES:512 + (b + 1) * LANES],
                                                gs32_ref[...], 1.0 / DIFF_HEAD, dqn_ref[...],
                                                cd, sd, d_scale).astype(BF16)
            dk_ref[rows, sl] = _norm_rope_block(z[:, 1024 + b * LANES:1024 + (b + 1) * LANES],
                                                gs32_ref[...], 1.0 / DIFF_HEAD, dkn_ref[...],
                                                cd, sd, 1.0).astype(BF16)


def _odd_in_kernel(x_ref, g1_ref, wg_ref, wu_ref, wd_ref, gm_ref, win_ref, wvt_ref,
                   qn_ref, kn_ref, gs64_ref, ca_ref, sa_ref,
                   xo_ref, q_ref, k_ref, vt_ref):
    q_scale = C_HEAD ** -0.5 * LOG2E
    for rows in _row_groups(x_ref.shape[0]):
        x = _swiglu_half(x_ref[rows, :], g1_ref[...], wg_ref, wu_ref, wd_ref)
        xo_ref[rows, :] = x
        h = _rms(x, gm_ref[...]).astype(BF16)
        z = jnp.dot(h, win_ref[...], preferred_element_type=F32)
        _store_vt_with_ones(vt_ref, rows, lax.dot_general(wvt_ref[...], h, NT_DIMS,
                                                          preferred_element_type=F32))
        ca, sa = ca_ref[rows, :], sa_ref[rows, :]
        for b in range(8):
            sl = slice(b * LANES, (b + 1) * LANES)
            q_ref[rows, sl] = _norm_rope_block(z[:, sl], gs64_ref[...], 1.0 / C_HEAD, qn_ref[...],
                                               ca, sa, q_scale).astype(BF16)
        for b in range(2):
            sl = slice(b * LANES, (b + 1) * LANES)
            k_ref[rows, sl] = _norm_rope_block(z[:, 1024 + b * LANES:1024 + (b + 1) * LANES],
                                               gs64_ref[...], 1.0 / C_HEAD, kn_ref[...],
                                               ca, sa, 1.0).astype(BF16)


def _even_out_kernel(x_ref, oa_ref, ob_ref, woa_ref, wob_ref, g2_ref, wg_ref, wu_ref, wd_ref, xo_ref):
    for rows in _row_groups(x_ref.shape[0]):
        x = x_ref[rows, :]
        x = x + jnp.dot(oa_ref[rows, :], woa_ref[...], preferred_element_type=F32)
        x = x + jnp.dot(ob_ref[rows, :], wob_ref[...], preferred_element_type=F32)
        xo_ref[rows, :] = _swiglu_half(x, g2_ref[...], wg_ref, wu_ref, wd_ref)


def _odd_out_kernel(x_ref, o_ref, wo_ref, g2_ref, wg_ref, wu_ref, wd_ref, xo_ref):
    for rows in _row_groups(x_ref.shape[0]):
        x = x_ref[rows, :] + jnp.dot(o_ref[rows, :], wo_ref[...], preferred_element_type=F32)
        xo_ref[rows, :] = _swiglu_half(x, g2_ref[...], wg_ref, wu_ref, wd_ref)


def _pair_rows(ot0, ot1):
    return jnp.concatenate([ot0, ot1], axis=0).T


def _out_blocks(o_ref):
    return range(o_ref.shape[1] // LANES)


def _mla_attn_kernel(bounded, q_ref, k_ref, vt_ref, o_ref):
    for s in _out_blocks(o_ref):
        outs = []
        for hd in (2 * s, 2 * s + 1):
            sl = slice(hd * LANES, (hd + 1) * LANES)
            outs.append(_softmax_pv_t(q_ref[:, sl], k_ref[:, sl],
                                      vt_ref[hd * VT_ROWS:(hd + 1) * VT_ROWS, :], bounded))
        o_ref[:, s * LANES:(s + 1) * LANES] = _pair_rows(*outs).astype(BF16)


def _gqa_attn_kernel(bounded, q_ref, k_ref, vt_ref, o_ref):
    k = k_ref[...]
    lane = _lane_iota(q_ref.shape[0])
    for s in _out_blocks(o_ref):
        q = q_ref[:, s * LANES:(s + 1) * LANES]
        zero = jnp.zeros_like(q)
        ot0 = _softmax_pv_t(jnp.where(lane < C_HEAD, q, zero), k, vt_ref[0:VT_ROWS, :], bounded)
        ot1 = _softmax_pv_t(jnp.where(lane >= C_HEAD, q, zero), k, vt_ref[VT_ROWS:2 * VT_ROWS, :], bounded)
        o_ref[:, s * LANES:(s + 1) * LANES] = _pair_rows(ot0, ot1).astype(BF16)


def _diff_attn_kernel(lam_init, bounded, q_ref, k_ref, vt_ref, lp_ref, sub_ref, o_ref):
    lp = lp_ref[...]
    lam = (jnp.exp(jnp.sum(lp[0:1] * lp[1:2], axis=-1, keepdims=True))
           - jnp.exp(jnp.sum(lp[2:3] * lp[3:4], axis=-1, keepdims=True)) + lam_init)
    lane = _lane_iota(q_ref.shape[0])
    lo = lane < 2 * DIFF_HEAD
    inv = 1.0 / (2 * DIFF_HEAD)
    for s in _out_blocks(o_ref):
        sl = slice(s * LANES, (s + 1) * LANES)
        q, k = q_ref[:, sl], k_ref[:, sl]
        zero = jnp.zeros_like(q)
        outs = []
        for hd in range(2):
            base = hd * 2 * DIFF_HEAD
            m1 = (lane >= base) & (lane < base + DIFF_HEAD)
            m2 = (lane >= base + DIFF_HEAD) & (lane < base + 2 * DIFF_HEAD)
            vt = vt_ref[(2 * s + hd) * VT_ROWS:(2 * s + hd + 1) * VT_ROWS, :]
            a1 = _softmax_pv_t(jnp.where(m1, q, zero), k, vt, bounded)
            a2 = _softmax_pv_t(jnp.where(m2, q, zero), k, vt, bounded)
            outs.append(a1 - lam * a2)
        o = _pair_rows(outs[0], outs[1])
        o2 = o * o
        r0 = lax.rsqrt(jnp.sum(jnp.where(lo, o2, 0.0), axis=-1, keepdims=True) * inv + EPS)
        r1 = lax.rsqrt(jnp.sum(jnp.where(lo, 0.0, o2), axis=-1, keepdims=True) * inv + EPS)
        o = o * jnp.where(lo, r0, r1) * sub_ref[...] * (1.0 - lam_init)
        o_ref[:, sl] = o.astype(BF16)


def _tile(n, want):
    return want if n % want == 0 else n


def _const_spec(shape):
    nd = len(shape)
    return pl.BlockSpec(shape, lambda *_: (0,) * nd, pipeline_mode=pl.Buffered(1))


def _tok_spec(tm, width):
    return pl.BlockSpec((None, tm, width), lambda b, t: (b, t, 0))


def _tok_t_spec(tm, width):
    return pl.BlockSpec((None, width, tm), lambda b, t: (b, 0, t))


def _tab_spec(tm):
    return pl.BlockSpec((tm, LANES), lambda b, t: (t, 0))


def _token_call(body, name, x, consts, tabs, extra_tok, outs, tm):
    bsz, seq, _ = x.shape
    tok_inputs = [x] + list(extra_tok)
    in_specs = ([_tok_spec(tm, a.shape[-1]) for a in tok_inputs]
                + [_const_spec(c.shape) for c in consts]
                + [_tab_spec(tm) for _ in tabs])
    out_shape = [jax.ShapeDtypeStruct((bsz, w, seq) if tr else (bsz, seq, w), d) for w, d, tr in outs]
    out_specs = [_tok_t_spec(tm, w) if tr else _tok_spec(tm, w) for w, d, tr in outs]
    return pl.pallas_call(
        body, name=name,
        grid=(bsz, seq // tm),
        in_specs=in_specs, out_specs=out_specs, out_shape=out_shape,
        compiler_params=pltpu.CompilerParams(
            dimension_semantics=("arbitrary", "arbitrary"), vmem_limit_bytes=VMEM_LIMIT),
    )(*tok_inputs, *consts, *tabs)


def _attn_dispatch(body, name, logit_bound, *args):
    q, k, vt, extra, *static = args

    def run(bounded, q, k, vt, extra):
        tag = "bounded" if bounded else "shifted"
        tq = _tile(q.shape[1], QUERY_TILE_BOUNDED if bounded else QUERY_TILE_SHIFTED)
        n_sub = OUT_BLOCKS_BOUNDED if bounded else OUT_BLOCKS_SHIFTED
        return _attn_call(functools.partial(body, bounded), f"{name}_{tag}", q, k, vt, extra,
                          *static, tq, n_sub)

    return lax.cond(logit_bound <= LOGIT_BOUND, functools.partial(run, True),
                    functools.partial(run, False), q, k, vt, tuple(extra))


def _attn_call(body, name, q, k, vt, extra, q_blk, k_blk, n_blocks, kv_share, tq, n_sub):
    bsz, seq, _ = q.shape
    if kv_share == 1:
        k_spec = pl.BlockSpec((None, seq, k_blk * n_sub), lambda b, h, t: (b, 0, h))
        vt_spec = pl.BlockSpec((None, 2 * VT_ROWS * n_sub, seq), lambda b, h, t: (b, h, 0))
    else:
        assert kv_share % n_sub == 0
        k_spec = pl.BlockSpec((None, seq, k_blk), lambda b, h, t: (b, 0, (h * n_sub) // kv_share))
        vt_spec = pl.BlockSpec((None, 2 * VT_ROWS, seq), lambda b, h, t: (b, (h * n_sub) // kv_share, 0))
    in_specs = [pl.BlockSpec((None, tq, q_blk * n_sub), lambda b, h, t: (b, t, h)), k_spec, vt_spec
                ] + [pl.BlockSpec(e.shape, lambda b, h, t: (0, 0)) for e in extra]
    return pl.pallas_call(
        body, name=name,
        grid=(bsz, n_blocks // n_sub, seq // tq),
        in_specs=in_specs,
        out_specs=pl.BlockSpec((None, tq, LANES * n_sub), lambda b, h, t: (b, t, h)),
        out_shape=jax.ShapeDtypeStruct((bsz, seq, n_blocks * LANES), BF16),
        compiler_params=pltpu.CompilerParams(
            dimension_semantics=("arbitrary", "arbitrary", "arbitrary"), vmem_limit_bytes=VMEM_LIMIT),
    )(q, k, vt, *extra)


def _rope_lane_tables(pos_a, pos_b):
    half = ROPE_DIM // 2
    inv = ROPE_THETA ** (-jnp.arange(0, ROPE_DIM, 2, dtype=F32) / ROPE_DIM)
    lane = np.arange(LANES)
    sign = jnp.asarray(np.where(lane % ROPE_DIM < half, -1.0, 1.0), F32)
    use_b = jnp.asarray((lane // ROPE_DIM) % 2 == 1)
    freq = inv[lane % half]
    ang_a = pos_a.astype(F32)[:, None] * freq[None, :]
    ang_b = pos_b.astype(F32)[:, None] * freq[None, :]
    cos = jnp.where(use_b[None, :], jnp.cos(ang_b), jnp.cos(ang_a))
    sin = jnp.where(use_b[None, :], jnp.sin(ang_b), jnp.sin(ang_a)) * sign[None, :]
    return cos, sin


def _group_sum_matrix(group):
    lane = np.arange(LANES)
    g = (lane[:, None] // group == lane[None, :] // group).astype(np.float32)
    return jnp.asarray(np.concatenate([g, g], axis=0), BF16)


def _row(v):
    return v.reshape(1, -1).astype(F32)


def _max_abs(v):
    return jnp.max(jnp.abs(v.astype(F32)))


def kernel(x, ffn1_norm, ffn1_w_gate, ffn1_w_up, ffn1_w_down, mix_norm, ffn2_norm, ffn2_w_gate, ffn2_w_up, ffn2_w_down, ab_w_in, mla_q_lora_norm, mla_w_uq, mla_kv_lora_norm, mla_w_ukv, mla_q_norm, mla_k_norm, diff_q_norm, diff_k_norm, diff_lambda_q1, diff_lambda_k1, diff_lambda_q2, diff_lambda_k2, diff_subln, ab_w_out, c_w_in, c_q_norm, c_k_norm, c_w_out):
    bsz, seq, d = x.shape
    depth = ffn1_norm.shape[0]
    tm = _tile(seq, TOKEN_TILE)

    pos = jnp.arange(seq, dtype=jnp.int32)
    cos_t, sin_t = _rope_lane_tables(pos, pos)
    cos_ax, sin_ax = _rope_lane_tables(pos // GRID_W, pos % GRID_W)
    lane = np.arange(LANES)
    mla_rot = jnp.asarray((lane >= MLA_NOPE) & (lane < MLA_NOPE + MLA_ROPE))
    cos_m = jnp.where(mla_rot[None, :], cos_t, 1.0)
    sin_m = jnp.where(mla_rot[None, :], sin_t, 0.0)
    gs_all, gs64, gs32 = _group_sum_matrix(LANES), _group_sum_matrix(C_HEAD), _group_sum_matrix(DIFF_HEAD)

    for i in range(depth):
        j = i // 2
        ffn1 = (_row(ffn1_norm[i]), ffn1_w_gate[i].astype(BF16), ffn1_w_up[i].astype(BF16),
                ffn1_w_down[i].astype(BF16))
        ffn2 = (_row(ffn2_norm[i]), ffn2_w_gate[i].astype(BF16), ffn2_w_up[i].astype(BF16),
                ffn2_w_down[i].astype(BF16))
        if i % 2 == 0:
            lam_init = 0.8 - 0.6 * math.exp(-0.3 * i)
            w = ab_w_in[j]
            kr_blk = jnp.pad(w[:, 384:416], ((0, 0), (MLA_NOPE, LANES - MLA_NOPE - MLA_ROPE)))
            w_in = jnp.concatenate([w[:, 0:384], kr_blk, w[:, 416:1440]], axis=1).astype(BF16)
            w_vt = w[:, 1440:].T.astype(BF16)
            w_uq = jnp.pad(mla_w_uq[j].reshape(MLA_Q_RANK, MLA_HEADS, MLA_NOPE + MLA_ROPE),
                           ((0, 0), (0, 0), (0, LANES - MLA_NOPE - MLA_ROPE))
                           ).reshape(MLA_Q_RANK, MLA_HEADS * LANES).astype(BF16)
            ukv = mla_w_ukv[j].reshape(MLA_KV_RANK, MLA_HEADS, MLA_NOPE + MLA_V)
            w_uk = jnp.pad(ukv[:, :, :MLA_NOPE], ((0, 0), (0, 0), (0, LANES - MLA_NOPE))
                           ).reshape(MLA_KV_RANK, MLA_HEADS * LANES).astype(BF16)
            w_uvt = ukv[:, :, MLA_NOPE:].reshape(MLA_KV_RANK, MLA_HEADS * MLA_V).T.astype(BF16)
            pad96 = (0, LANES - MLA_NOPE - MLA_ROPE)
            consts = ffn1 + (_row(mix_norm[i]), w_in, w_vt,
                             _row(mla_q_lora_norm[j]), w_uq, _row(mla_kv_lora_norm[j]), w_uk, w_uvt,
                             _row(jnp.pad(mla_q_norm[j], pad96)), _row(jnp.pad(mla_k_norm[j], pad96)),
                             _row(jnp.tile(diff_q_norm[j], LANES // DIFF_HEAD)),
                             _row(jnp.tile(diff_k_norm[j], LANES // DIFF_HEAD)),
                             gs_all, gs32)
            x, q, k, vt, dq, dk, dvt = _token_call(
                _even_in_kernel, f"even_in_{i}", x, consts, (cos_m, sin_m, cos_t, sin_t), (),
                ((d, F32, False), (1024, BF16, False), (1024, BF16, False),
                 (VT_ROWS * MLA_HEADS, BF16, True), (512, BF16, False), (512, BF16, False),
                 (VT_ROWS * DIFF_HEADS, BF16, True)), tm)
            mla_dim = MLA_NOPE + MLA_ROPE
            mla_bound = (mla_dim ** 0.5 * LOG2E) * _max_abs(mla_q_norm[j]) * _max_abs(mla_k_norm[j])
            o_mla = _attn_dispatch(_mla_attn_kernel, f"mla_attn_{i}", mla_bound, q, k, vt, (),
                                   2 * LANES, 2 * LANES, MLA_HEADS // 2, 1)
            lam_p = jnp.stack([diff_lambda_q1[j], diff_lambda_k1[j],
                               diff_lambda_q2[j], diff_lambda_k2[j]]).astype(F32)
            sub = _row(jnp.tile(diff_subln[j], 2))
            diff_bound = (DIFF_HEAD ** 0.5 * LOG2E) * _max_abs(diff_q_norm[j]) * _max_abs(diff_k_norm[j])
            o_diff = _attn_dispatch(functools.partial(_diff_attn_kernel, lam_init), f"diff_attn_{i}",
                                    diff_bound, dq, dk, dvt, (lam_p, sub),
                                    LANES, LANES, DIFF_HEADS // 2, 1)
            wo = ab_w_out[j].astype(BF16)
            consts = (wo[:MLA_HEADS * MLA_V], wo[MLA_HEADS * MLA_V:]) + ffn2
            (x,) = _token_call(_even_out_kernel, f"even_out_{i}", x, consts, (), (o_mla, o_diff),
                               ((d, F32, False),), tm)
        else:
            w = c_w_in[j]
            wq = w[:, :C_HEADS * C_HEAD].reshape(d, 2, 2, C_GROUPS, C_HEAD)
            wq = wq.transpose(0, 1, 3, 2, 4).reshape(d, C_HEADS * C_HEAD)
            n_qk = (C_HEADS + C_KV_HEADS) * C_HEAD
            w_in = jnp.concatenate([wq, w[:, C_HEADS * C_HEAD:n_qk]], axis=1).astype(BF16)
            w_vt = w[:, n_qk:].T.astype(BF16)
            consts = ffn1 + (_row(mix_norm[i]), w_in, w_vt,
                             _row(jnp.tile(c_q_norm[j], 2)), _row(jnp.tile(c_k_norm[j], 2)), gs64)
            x, q, k, vt = _token_call(
                _odd_in_kernel, f"odd_in_{i}", x, consts, (cos_ax, sin_ax), (),
                ((d, F32, False), (1024, BF16, False), (256, BF16, False),
                 (VT_ROWS * C_KV_HEADS, BF16, True)), tm)
            gqa_bound = (C_HEAD ** 0.5 * LOG2E) * _max_abs(c_q_norm[j]) * _max_abs(c_k_norm[j])
            o = _attn_dispatch(_gqa_attn_kernel, f"gqa_attn_{i}", gqa_bound, q, k, vt, (),
                               LANES, LANES, C_HEADS // 2, C_GROUPS)
            wo = c_w_out[j].reshape(2, 2, C_GROUPS, C_HEAD, d).transpose(0, 2, 1, 3, 4)
            wo = wo.reshape(C_HEADS * C_HEAD, d).astype(BF16)
            consts = (wo,) + ffn2
            (x,) = _token_call(_odd_out_kernel, f"odd_out_{i}", x, consts, (), (o,),
                               ((d, F32, False),), tm)
    return x
```

```python
import functools
import math

import jax
import jax.numpy as jnp
import numpy as np
from jax import lax
from jax.experimental import pallas as pl
from jax.experimental.pallas import tpu as pltpu

F32 = jnp.float32
BF16 = jnp.bfloat16

LANES = 128
EPS = 1e-6
ROPE_THETA = 10000.0
GRID_W = 64
LOG2E = math.log2(math.e)

D_MODEL = 1024
D_FF = 2816
MLA_HEADS, MLA_Q_RANK, MLA_KV_RANK = 8, 256, 128
MLA_NOPE, MLA_ROPE, MLA_V = 64, 32, 64
DIFF_HEADS, DIFF_HEAD = 8, 32
C_HEADS, C_KV_HEADS, C_HEAD = 16, 4, 64
C_GROUPS = C_HEADS // C_KV_HEADS
ROPE_DIM = 32

VMEM_LIMIT = 56 * 1024 * 1024
TOKEN_TILE = 512
TOKEN_SPLIT = 2
QUERY_TILE_BOUNDED = 1024
QUERY_TILE_SHIFTED = 512
OUT_BLOCKS_BOUNDED = 2
OUT_BLOCKS_SHIFTED = 1
NT_DIMS = (((1,), (1,)), ((), ()))
LOGIT_BOUND = 64.0
HEAD_V = 64
ONES_ROWS = 64
VT_ROWS = HEAD_V + ONES_ROWS


def _rms(x, g):
    ms = jnp.mean(x * x, axis=-1, keepdims=True)
    return x * lax.rsqrt(ms + EPS) * g


def _bdot(a, b):
    return jnp.dot(a.astype(BF16), b, preferred_element_type=F32)


def _swiglu_half(x, g, wg, wu, wd):
    h = _rms(x, g).astype(BF16)
    gate = jnp.dot(h, wg[...], preferred_element_type=F32)
    up = jnp.dot(h, wu[...], preferred_element_type=F32)
    act = (gate / (1.0 + jnp.exp(-gate))) * up
    return x + 0.5 * jnp.dot(act.astype(BF16), wd[...], preferred_element_type=F32)


def _lane_iota(rows):
    return lax.broadcasted_iota(jnp.int32, (rows, LANES), 1)


def _norm_rope_block(x, gsum, inv_count, gain, cos, sin, post_scale):
    lane = _lane_iota(x.shape[0])
    x2 = x * x
    hi = x2.astype(BF16)
    lo = (x2 - hi.astype(F32)).astype(BF16)
    ss = jnp.dot(jnp.concatenate([hi, lo], axis=1), gsum, preferred_element_type=F32)
    y = x * lax.rsqrt(ss * inv_count + EPS) * gain
    up = pltpu.roll(y, LANES - ROPE_DIM // 2, axis=1)
    dn = pltpu.roll(y, ROPE_DIM // 2, axis=1)
    partner = jnp.where((lane % ROPE_DIM) < ROPE_DIM // 2, up, dn)
    y = y * cos + partner * sin
    if post_scale != 1.0:
        y = y * post_scale
    return y


def _softmax_pv_t(qm, k, vt, bounded):
    st = lax.dot_general(k, qm, NT_DIMS, preferred_element_type=F32)
    if not bounded:
        st = st - jnp.max(st, axis=0, keepdims=True)
    ot = jnp.dot(vt, jnp.exp2(st).astype(BF16), preferred_element_type=F32)
    return ot[:HEAD_V] / ot[HEAD_V:HEAD_V + 1]


def _store_vt_with_ones(vt_ref, cols, vt):
    ones = jnp.ones((ONES_ROWS, vt.shape[1]), BF16)
    for hd in range(vt.shape[0] // HEAD_V):
        vt_ref[hd * VT_ROWS:hd * VT_ROWS + HEAD_V, cols] = vt[hd * HEAD_V:(hd + 1) * HEAD_V].astype(BF16)
        vt_ref[hd * VT_ROWS + HEAD_V:(hd + 1) * VT_ROWS, cols] = ones


def _row_groups(n_rows):
    step = n_rows // TOKEN_SPLIT
    return [slice(r * step, (r + 1) * step) for r in range(TOKEN_SPLIT)]


def _even_in_kernel(x_ref, g1_ref, wg_ref, wu_ref, wd_ref, gm_ref, win_ref, wvt_ref,
                    gq_ref, wuq_ref, gkv_ref, wuk_ref, wuvt_ref, qn_ref, kn_ref, dqn_ref, dkn_ref,
                    gs96_ref, gs32_ref, cm_ref, sm_ref, cd_ref, sd_ref,
                    xo_ref, q_ref, k_ref, vt_ref, dq_ref, dk_ref, dvt_ref):
    q_scale = (MLA_NOPE + MLA_ROPE) ** -0.5 * LOG2E
    d_scale = DIFF_HEAD ** -0.5 * LOG2E
    for rows in _row_groups(x_ref.shape[0]):
        x = _swiglu_half(x_ref[rows, :], g1_ref[...], wg_ref, wu_ref, wd_ref)
        xo_ref[rows, :] = x
        h = _rms(x, gm_ref[...]).astype(BF16)
        z = jnp.dot(h, win_ref[...], preferred_element_type=F32)
        _store_vt_with_ones(dvt_ref, rows, lax.dot_general(wvt_ref[...], h, NT_DIMS,
                                                           preferred_element_type=F32))
        kr = z[:, 384:512]
        cm, sm, cd, sd = cm_ref[rows, :], sm_ref[rows, :], cd_ref[rows, :], sd_ref[rows, :]
        ckv = _rms(z[:, 256:384], gkv_ref[...]).astype(BF16)
        qf = _bdot(_rms(z[:, 0:256], gq_ref[...]), wuq_ref[...])
        kf = jnp.dot(ckv, wuk_ref[...], preferred_element_type=F32)
        _store_vt_with_ones(vt_ref, rows, lax.dot_general(wuvt_ref[...], ckv, NT_DIMS,
                                                          preferred_element_type=F32))
        for hd in range(MLA_HEADS):
            sl = slice(hd * LANES, (hd + 1) * LANES)
            q_ref[rows, sl] = _norm_rope_block(qf[:, sl], gs96_ref[...], 1.0 / (MLA_NOPE + MLA_ROPE), qn_ref[...],
                                               cm, sm, q_scale).astype(BF16)
            k_ref[rows, sl] = _norm_rope_block(kf[:, sl] + kr, gs96_ref[...], 1.0 / (MLA_NOPE + MLA_ROPE), kn_ref[...],
                                               cm, sm, 1.0).astype(BF16)
        for b in range(4):
            sl = slice(b * LANES, (b + 1) * LANES)
            dq_ref[rows, sl] = _norm_rope_block(z[:, 512 + b * LANES:512 + (b + 1) * LANES],
                                                gs32_ref[...], 1.0 / DIFF_HEAD, dqn_ref[...],
                                                cd, sd, d_scale).astype(BF16)
            dk_ref[rows, sl] = _norm_rope_block(z[:, 1024 + b * LANES:1024 + (b + 1) * LANES],
                                                gs32_ref[...], 1.0 / DIFF_HEAD, dkn_ref[...],
                                                cd, sd, 1.0).astype(BF16)


def _odd_in_kernel(x_ref, g1_ref, wg_ref, wu_ref, wd_ref, gm_ref, win_ref, wvt_ref,
                   qn_ref, kn_ref, gs64_ref, ca_ref, sa_ref,
                   xo_ref, q_ref, k_ref, vt_ref):
    q_scale = C_HEAD ** -0.5 * LOG2E
    for rows in _row_groups(x_ref.shape[0]):
        x = _swiglu_half(x_ref[rows, :], g1_ref[...], wg_ref, wu_ref, wd_ref)
        xo_ref[rows, :] = x
        h = _rms(x, gm_ref[...]).astype(BF16)
        z = jnp.dot(h, win_ref[...], preferred_element_type=F32)
        _store_vt_with_ones(vt_ref, rows, lax.dot_general(wvt_ref[...], h, NT_DIMS,
                                                          preferred_element_type=F32))
        ca, sa = ca_ref[rows, :], sa_ref[rows, :]
        for b in range(8):
            sl = slice(b * LANES, (b + 1) * LANES)
            q_ref[rows, sl] = _norm_rope_block(z[:, sl], gs64_ref[...], 1.0 / C_HEAD, qn_ref[...],
                                               ca, sa, q_scale).astype(BF16)
        for b in range(2):
            sl = slice(b * LANES, (b + 1) * LANES)
            k_ref[rows, sl] = _norm_rope_block(z[:, 1024 + b * LANES:1024 + (b + 1) * LANES],
                                               gs64_ref[...], 1.0 / C_HEAD, kn_ref[...],
                                               ca, sa, 1.0).astype(BF16)


def _even_out_kernel(x_ref, oa_ref, ob_ref, woa_ref, wob_ref, g2_ref, wg_ref, wu_ref, wd_ref, xo_ref):
    for rows in _row_groups(x_ref.shape[0]):
        x = x_ref[rows, :]
        x = x + jnp.dot(oa_ref[rows, :], woa_ref[...], preferred_element_type=F32)
        x = x + jnp.dot(ob_ref[rows, :], wob_ref[...], preferred_element_type=F32)
        xo_ref[rows, :] = _swiglu_half(x, g2_ref[...], wg_ref, wu_ref, wd_ref)


def _odd_out_kernel(x_ref, o_ref, wo_ref, g2_ref, wg_ref, wu_ref, wd_ref, xo_ref):
    for rows in _row_groups(x_ref.shape[0]):
        x = x_ref[rows, :] + jnp.dot(o_ref[rows, :], wo_ref[...], preferred_element_type=F32)
        xo_ref[rows, :] = _swiglu_half(x, g2_ref[...], wg_ref, wu_ref, wd_ref)


def _pair_rows(ot0, ot1):
    return jnp.concatenate([ot0, ot1], axis=0).T


def _out_blocks(o_ref):
    return range(o_ref.shape[1] // LANES)


def _mla_attn_kernel(bounded, q_ref, k_ref, vt_ref, o_ref):
    for s in _out_blocks(o_ref):
        outs = []
        for hd in (2 * s, 2 * s + 1):
            sl = slice(hd * LANES, (hd + 1) * LANES)
            outs.append(_softmax_pv_t(q_ref[:, sl], k_ref[:, sl],
                                      vt_ref[hd * VT_ROWS:(hd + 1) * VT_ROWS, :], bounded))
        o_ref[:, s * LANES:(s + 1) * LANES] = _pair_rows(*outs).astype(BF16)


def _gqa_attn_kernel(bounded, q_ref, k_ref, vt_ref, o_ref):
    k = k_ref[...]
    lane = _lane_iota(q_ref.shape[0])
    for s in _out_blocks(o_ref):
        q = q_ref[:, s * LANES:(s + 1) * LANES]
        zero = jnp.zeros_like(q)
        ot0 = _softmax_pv_t(jnp.where(lane < C_HEAD, q, zero), k, vt_ref[0:VT_ROWS, :], bounded)
        ot1 = _softmax_pv_t(jnp.where(lane >= C_HEAD, q, zero), k, vt_ref[VT_ROWS:2 * VT_ROWS, :], bounded)
        o_ref[:, s * LANES:(s + 1) * LANES] = _pair_rows(ot0, ot1).astype(BF16)


def _diff_attn_kernel(lam_init, bounded, q_ref, k_ref, vt_ref, lp_ref, sub_ref, o_ref):
    lp = lp_ref[...]
    lam = (jnp.exp(jnp.sum(lp[0:1] * lp[1:2], axis=-1, keepdims=True))
           - jnp.exp(jnp.sum(lp[2:3] * lp[3:4], axis=-1, keepdims=True)) + lam_init)
    lane = _lane_iota(q_ref.shape[0])
    lo = lane < 2 * DIFF_HEAD
    inv = 1.0 / (2 * DIFF_HEAD)
    for s in _out_blocks(o_ref):
        sl = slice(s * LANES, (s + 1) * LANES)
        q, k = q_ref[:, sl], k_ref[:, sl]
        zero = jnp.zeros_like(q)
        outs = []
        for hd in range(2):
            base = hd * 2 * DIFF_HEAD
            m1 = (lane >= base) & (lane < base + DIFF_HEAD)
            m2 = (lane >= base + DIFF_HEAD) & (lane < base + 2 * DIFF_HEAD)
            vt = vt_ref[(2 * s + hd) * VT_ROWS:(2 * s + hd + 1) * VT_ROWS, :]
            a1 = _softmax_pv_t(jnp.where(m1, q, zero), k, vt, bounded)
            a2 = _softmax_pv_t(jnp.where(m2, q, zero), k, vt, bounded)
            outs.append(a1 - lam * a2)
        o = _pair_rows(outs[0], outs[1])
        o2 = o * o
        r0 = lax.rsqrt(jnp.sum(jnp.where(lo, o2, 0.0), axis=-1, keepdims=True) * inv + EPS)
        r1 = lax.rsqrt(jnp.sum(jnp.where(lo, 0.0, o2), axis=-1, keepdims=True) * inv + EPS)
        o = o * jnp.where(lo, r0, r1) * sub_ref[...] * (1.0 - lam_init)
        o_ref[:, sl] = o.astype(BF16)


def _tile(n, want):
    return want if n % want == 0 else n


def _const_spec(shape):
    nd = len(shape)
    return pl.BlockSpec(shape, lambda *_: (0,) * nd, pipeline_mode=pl.Buffered(1))


def _tok_spec(tm, width):
    return pl.BlockSpec((None, tm, width), lambda b, t: (b, t, 0))


def _tok_t_spec(tm, width):
    return pl.BlockSpec((None, width, tm), lambda b, t: (b, 0, t))


def _tab_spec(tm):
    return pl.BlockSpec((tm, LANES), lambda b, t: (t, 0))


def _token_call(body, name, x, consts, tabs, extra_tok, outs, tm):
    bsz, seq, _ = x.shape
    tok_inputs = [x] + list(extra_tok)
    in_specs = ([_tok_spec(tm, a.shape[-1]) for a in tok_inputs]
                + [_const_spec(c.shape) for c in consts]
                + [_tab_spec(tm) for _ in tabs])
    out_shape = [jax.ShapeDtypeStruct((bsz, w, seq) if tr else (bsz, seq, w), d) for w, d, tr in outs]
    out_specs = [_tok_t_spec(tm, w) if tr else _tok_spec(tm, w) for w, d, tr in outs]
    return pl.pallas_call(
        body, name=name,
        grid=(bsz, seq // tm),
        in_specs=in_specs, out_specs=out_specs, out_shape=out_shape,
        compiler_params=pltpu.CompilerParams(
            dimension_semantics=("arbitrary", "arbitrary"), vmem_limit_bytes=VMEM_LIMIT),
    )(*tok_inputs, *consts, *tabs)


def _attn_dispatch(body, name, logit_bound, *args):
    q, k, vt, extra, *static = args

    def run(bounded, q, k, vt, extra):
        tag = "bounded" if bounded else "shifted"
        tq = _tile(q.shape[1], QUERY_TILE_BOUNDED if bounded else QUERY_TILE_SHIFTED)
        n_sub = OUT_BLOCKS_BOUNDED if bounded else OUT_BLOCKS_SHIFTED
        return _attn_call(functools.partial(body, bounded), f"{name}_{tag}", q, k, vt, extra,
                          *static, tq, n_sub)

    return lax.cond(logit_bound <= LOGIT_BOUND, functools.partial(run, True),
                    functools.partial(run, False), q, k, vt, tuple(extra))


def _attn_call(body, name, q, k, vt, extra, q_blk, k_blk, n_blocks, kv_share, tq, n_sub):
    bsz, seq, _ = q.shape
    if kv_share == 1:
        k_spec = pl.BlockSpec((None, seq, k_blk * n_sub), lambda b, h, t: (b, 0, h))
        vt_spec = pl.BlockSpec((None, 2 * VT_ROWS * n_sub, seq), lambda b, h, t: (b, h, 0))
    else:
        assert kv_share % n_sub == 0
        k_spec = pl.BlockSpec((None, seq, k_blk), lambda b, h, t: (b, 0, (h * n_sub) // kv_share))
        vt_spec = pl.BlockSpec((None, 2 * VT_ROWS, seq), lambda b, h, t: (b, (h * n_sub) // kv_share, 0))
    in_specs = [pl.BlockSpec((None, tq, q_blk * n_sub), lambda b, h, t: (b, t, h)), k_spec, vt_spec
                ] + [pl.BlockSpec(e.shape, lambda b, h, t: (0, 0)) for e in extra]
    return pl.pallas_call(
        body, name=name,
        grid=(bsz, n_blocks // n_sub, seq // tq),
        in_specs=in_specs,
        out_specs=pl.BlockSpec((None, tq, LANES * n_sub), lambda b, h, t: (b, t, h)),
        out_shape=jax.ShapeDtypeStruct((bsz, seq, n_blocks * LANES), BF16),
        compiler_params=pltpu.CompilerParams(
            dimension_semantics=("arbitrary", "arbitrary", "arbitrary"), vmem_limit_bytes=VMEM_LIMIT),
    )(q, k, vt, *extra)


def _rope_lane_tables(pos_a, pos_b):
    half = ROPE_DIM // 2
    inv = ROPE_THETA ** (-jnp.arange(0, ROPE_DIM, 2, dtype=F32) / ROPE_DIM)
    lane = np.arange(LANES)
    sign = jnp.asarray(np.where(lane % ROPE_DIM < half, -1.0, 1.0), F32)
    use_b = jnp.asarray((lane // ROPE_DIM) % 2 == 1)
    freq = inv[lane % half]
    ang_a = pos_a.astype(F32)[:, None] * freq[None, :]
    ang_b = pos_b.astype(F32)[:, None] * freq[None, :]
    cos = jnp.where(use_b[None, :], jnp.cos(ang_b), jnp.cos(ang_a))
    sin = jnp.where(use_b[None, :], jnp.sin(ang_b), jnp.sin(ang_a)) * sign[None, :]
    return cos, sin


def _group_sum_matrix(group):
    lane = np.arange(LANES)
    g = (lane[:, None] // group == lane[None, :] // group).astype(np.float32)
    return jnp.asarray(np.concatenate([g, g], axis=0), BF16)


def _row(v):
    return v.reshape(1, -1).astype(F32)


def _max_abs(v):
    return jnp.max(jnp.abs(v.astype(F32)))


def kernel(x, ffn1_norm, ffn1_w_gate, ffn1_w_up, ffn1_w_down, mix_norm, ffn2_norm, ffn2_w_gate, ffn2_w_up, ffn2_w_down, ab_w_in, mla_q_lora_norm, mla_w_uq, mla_kv_lora_norm, mla_w_ukv, mla_q_norm, mla_k_norm, diff_q_norm, diff_k_norm, diff_lambda_q1, diff_lambda_k1, diff_lambda_q2, diff_lambda_k2, diff_subln, ab_w_out, c_w_in, c_q_norm, c_k_norm, c_w_out):
    bsz, seq, d = x.shape
    depth = ffn1_norm.shape[0]
    tm = _tile(seq, TOKEN_TILE)

    pos = jnp.arange(seq, dtype=jnp.int32)
    cos_t, sin_t = _rope_lane_tables(pos, pos)
    cos_ax, sin_ax = _rope_lane_tables(pos // GRID_W, pos % GRID_W)
    lane = np.arange(LANES)
    mla_rot = jnp.asarray((lane >= MLA_NOPE) & (lane < MLA_NOPE + MLA_ROPE))
    cos_m = jnp.where(mla_rot[None, :], cos_t, 1.0)
    sin_m = jnp.where(mla_rot[None, :], sin_t, 0.0)
    gs_all, gs64, gs32 = _group_sum_matrix(LANES), _group_sum_matrix(C_HEAD), _group_sum_matrix(DIFF_HEAD)

    for i in range(depth):
        j = i // 2
        ffn1 = (_row(ffn1_norm[i]), ffn1_w_gate[i].astype(BF16), ffn1_w_up[i].astype(BF16),
                ffn1_w_down[i].astype(BF16))
        ffn2 = (_row(ffn2_norm[i]), ffn2_w_gate[i].astype(BF16), ffn2_w_up[i].astype(BF16),
                ffn2_w_down[i].astype(BF16))
        if i % 2 == 0:
            lam_init = 0.8 - 0.6 * math.exp(-0.3 * i)
            w = ab_w_in[j]
            kr_blk = jnp.pad(w[:, 384:416], ((0, 0), (MLA_NOPE, LANES - MLA_NOPE - MLA_ROPE)))
            w_in = jnp.concatenate([w[:, 0:384], kr_blk, w[:, 416:1440]], axis=1).astype(BF16)
            w_vt = w[:, 1440:].T.astype(BF16)
            w_uq = jnp.pad(mla_w_uq[j].reshape(MLA_Q_RANK, MLA_HEADS, MLA_NOPE + MLA_ROPE),
                           ((0, 0), (0, 0), (0, LANES - MLA_NOPE - MLA_ROPE))
                           ).reshape(MLA_Q_RANK, MLA_HEADS * LANES).astype(BF16)
            ukv = mla_w_ukv[j].reshape(MLA_KV_RANK, MLA_HEADS, MLA_NOPE + MLA_V)
            w_uk = jnp.pad(ukv[:, :, :MLA_NOPE], ((0, 0), (0, 0), (0, LANES - MLA_NOPE))
                           ).reshape(MLA_KV_RANK, MLA_HEADS * LANES).astype(BF16)
            w_uvt = ukv[:, :, MLA_NOPE:].reshape(MLA_KV_RANK, MLA_HEADS * MLA_V).T.astype(BF16)
            pad96 = (0, LANES - MLA_NOPE - MLA_ROPE)
            consts = ffn1 + (_row(mix_norm[i]), w_in, w_vt,
                             _row(mla_q_lora_norm[j]), w_uq, _row(mla_kv_lora_norm[j]), w_uk, w_uvt,
                             _row(jnp.pad(mla_q_norm[j], pad96)), _row(jnp.pad(mla_k_norm[j], pad96)),
                             _row(jnp.tile(diff_q_norm[j], LANES // DIFF_HEAD)),
                             _row(jnp.tile(diff_k_norm[j], LANES // DIFF_HEAD)),
                             gs_all, gs32)
            x, q, k, vt, dq, dk, dvt = _token_call(
                _even_in_kernel, f"even_in_{i}", x, consts, (cos_m, sin_m, cos_t, sin_t), (),
                ((d, F32, False), (1024, BF16, False), (1024, BF16, False),
                 (VT_ROWS * MLA_HEADS, BF16, True), (512, BF16, False), (512, BF16, False),
                 (VT_ROWS * DIFF_HEADS, BF16, True)), tm)
            mla_dim = MLA_NOPE + MLA_ROPE
            mla_bound = (mla_dim ** 0.5 * LOG2E) * _max_abs(mla_q_norm[j]) * _max_abs(mla_k_norm[j])
            o_mla = _attn_dispatch(_mla_attn_kernel, f"mla_attn_{i}", mla_bound, q, k, vt, (),
                                   2 * LANES, 2 * LANES, MLA_HEADS // 2, 1)
            lam_p = jnp.stack([diff_lambda_q1[j], diff_lambda_k1[j],
                               diff_lambda_q2[j], diff_lambda_k2[j]]).astype(F32)
            sub = _row(jnp.tile(diff_subln[j], 2))
            diff_bound = (DIFF_HEAD ** 0.5 * LOG2E) * _max_abs(diff_q_norm[j]) * _max_abs(diff_k_norm[j])
            o_diff = _attn_dispatch(functools.partial(_diff_attn_kernel, lam_init), f"diff_attn_{i}",
                                    diff_bound, dq, dk, dvt, (lam_p, sub),
                                    LANES, LANES, DIFF_HEADS // 2, 1)
            wo = ab_w_out[j].astype(BF16)
            consts = (wo[:MLA_HEADS * MLA_V], wo[MLA_HEADS * MLA_V:]) + ffn2
            (x,) = _token_call(_even_out_kernel, f"even_out_{i}", x, consts, (), (o_mla, o_diff),
                               ((d, F32, False),), tm)
        else:
            w = c_w_in[j]
            wq = w[:, :C_HEADS * C_HEAD].reshape(d, 2, 2, C_GROUPS, C_HEAD)
            wq = wq.transpose(0, 1, 3, 2, 4).reshape(d, C_HEADS * C_HEAD)
            n_qk = (C_HEADS + C_KV_HEADS) * C_HEAD
            w_in = jnp.concatenate([wq, w[:, C_HEADS * C_HEAD:n_qk]], axis=1).astype(BF16)
            w_vt = w[:, n_qk:].T.astype(BF16)
            consts = ffn1 + (_row(mix_norm[i]), w_in, w_vt,
                             _row(jnp.tile(c_q_norm[j], 2)), _row(jnp.tile(c_k_norm[j], 2)), gs64)
            x, q, k, vt = _token_call(
                _odd_in_kernel, f"odd_in_{i}", x, consts, (cos_ax, sin_ax), (),
                ((d, F32, False), (1024, BF16, False), (256, BF16, False),
                 (VT_ROWS * C_KV_HEADS, BF16, True)), tm)
            gqa_bound = (C_HEAD ** 0.5 * LOG2E) * _max_abs(c_q_norm[j]) * _max_abs(c_k_norm[j])
            o = _attn_dispatch(_gqa_attn_kernel, f"gqa_attn_{i}", gqa_bound, q, k, vt, (),
                               LANES, LANES, C_HEADS // 2, C_GROUPS)
            wo = c_w_out[j].reshape(2, 2, C_GROUPS, C_HEAD, d).transpose(0, 2, 1, 3, 4)
            wo = wo.reshape(C_HEADS * C_HEAD, d).astype(BF16)
            consts = (wo,) + ffn2
            (x,) = _token_call(_odd_out_kernel, f"odd_out_{i}", x, consts, (), (o,),
                               ((d, F32, False),), tm)
    return x
```

```python
import functools
import math

import jax
import jax.numpy as jnp
import numpy as np
from jax import lax
from jax.experimental import pallas as pl
from jax.experimental.pallas import tpu as pltpu

F32 = jnp.float32
BF16 = jnp.bfloat16

LANES = 128
EPS = 1e-6
ROPE_THETA = 10000.0
GRID_W = 64
LOG2E = math.log2(math.e)

D_MODEL = 1024
D_FF = 2816
MLA_HEADS, MLA_Q_RANK, MLA_KV_RANK = 8, 256, 128
MLA_NOPE, MLA_ROPE, MLA_V = 64, 32, 64
DIFF_HEADS, DIFF_HEAD = 8, 32
C_HEADS, C_KV_HEADS, C_HEAD = 16, 4, 64
C_GROUPS = C_HEADS // C_KV_HEADS
ROPE_DIM = 32

VMEM_LIMIT = 56 * 1024 * 1024
TOKEN_TILE = 512
TOKEN_SPLIT_IN = 2
TOKEN_SPLIT_OUT = 1
QUERY_TILE_BOUNDED = 1024
QUERY_TILE_SHIFTED = 512
OUT_BLOCKS_BOUNDED = 2
OUT_BLOCKS_SHIFTED = 1
NT_DIMS = (((1,), (1,)), ((), ()))
LOGIT_BOUND = 64.0
HEAD_V = 64
ONES_ROWS = 64
VT_ROWS = HEAD_V + ONES_ROWS


def _rms(x, g):
    ms = jnp.mean(x * x, axis=-1, keepdims=True)
    return x * lax.rsqrt(ms + EPS) * g


def _bdot(a, b):
    return jnp.dot(a.astype(BF16), b, preferred_element_type=F32)


def _swiglu_half(x, g, wg, wu, wd):
    h = _rms(x, g).astype(BF16)
    gate = jnp.dot(h, wg[...], preferred_element_type=F32)
    up = jnp.dot(h, wu[...], preferred_element_type=F32)
    act = (gate / (1.0 + jnp.exp(-gate))) * up
    return x + 0.5 * jnp.dot(act.astype(BF16), wd[...], preferred_element_type=F32)


def _lane_iota(rows):
    return lax.broadcasted_iota(jnp.int32, (rows, LANES), 1)


def _norm_rope_block(x, gsum, inv_count, gain, cos, sin, post_scale):
    lane = _lane_iota(x.shape[0])
    x2 = x * x
    hi = x2.astype(BF16)
    lo = (x2 - hi.astype(F32)).astype(BF16)
    ss = jnp.dot(jnp.concatenate([hi, lo], axis=1), gsum, preferred_element_type=F32)
    y = x * lax.rsqrt(ss * inv_count + EPS) * gain
    up = pltpu.roll(y, LANES - ROPE_DIM // 2, axis=1)
    dn = pltpu.roll(y, ROPE_DIM // 2, axis=1)
    partner = jnp.where((lane % ROPE_DIM) < ROPE_DIM // 2, up, dn)
    y = y * cos + partner * sin
    if post_scale != 1.0:
        y = y * post_scale
    return y


def _softmax_pv_t(qm, k, vt, bounded):
    st = lax.dot_general(k, qm, NT_DIMS, preferred_element_type=F32)
    if not bounded:
        st = st - jnp.max(st, axis=0, keepdims=True)
    ot = jnp.dot(vt, jnp.exp2(st).astype(BF16), preferred_element_type=F32)
    return ot[:HEAD_V] / ot[HEAD_V:HEAD_V + 1]


def _store_vt_with_ones(vt_ref, cols, vt):
    ones = jnp.ones((ONES_ROWS, vt.shape[1]), BF16)
    for hd in range(vt.shape[0] // HEAD_V):
        vt_ref[hd * VT_ROWS:hd * VT_ROWS + HEAD_V, cols] = vt[hd * HEAD_V:(hd + 1) * HEAD_V].astype(BF16)
        vt_ref[hd * VT_ROWS + HEAD_V:(hd + 1) * VT_ROWS, cols] = ones


def _row_groups(n_rows, split):
    step = n_rows // split
    return [slice(r * step, (r + 1) * step) for r in range(split)]


def _even_in_kernel(x_ref, g1_ref, wg_ref, wu_ref, wd_ref, gm_ref, win_ref, wvt_ref,
                    gq_ref, wuq_ref, gkv_ref, wuk_ref, wuvt_ref, qn_ref, kn_ref, dqn_ref, dkn_ref,
                    gs96_ref, gs32_ref, cm_ref, sm_ref, cd_ref, sd_ref,
                    xo_ref, q_ref, k_ref, vt_ref, dq_ref, dk_ref, dvt_ref):
    q_scale = (MLA_NOPE + MLA_ROPE) ** -0.5 * LOG2E
    d_scale = DIFF_HEAD ** -0.5 * LOG2E
    for rows in _row_groups(x_ref.shape[0], TOKEN_SPLIT_IN):
        x = _swiglu_half(x_ref[rows, :], g1_ref[...], wg_ref, wu_ref, wd_ref)
        xo_ref[rows, :] = x
        h = _rms(x, gm_ref[...]).astype(BF16)
        z = jnp.dot(h, win_ref[...], preferred_element_type=F32)
        _store_vt_with_ones(dvt_ref, rows, lax.dot_general(wvt_ref[...], h, NT_DIMS,
                                                           preferred_element_type=F32))
        kr = z[:, 384:512]
        cm, sm, cd, sd = cm_ref[rows, :], sm_ref[rows, :], cd_ref[rows, :], sd_ref[rows, :]
        ckv = _rms(z[:, 256:384], gkv_ref[...]).astype(BF16)
        qf = _bdot(_rms(z[:, 0:256], gq_ref[...]), wuq_ref[...])
        kf = jnp.dot(ckv, wuk_ref[...], preferred_element_type=F32)
        _store_vt_with_ones(vt_ref, rows, lax.dot_general(wuvt_ref[...], ckv, NT_DIMS,
                                                          preferred_element_type=F32))
        for hd in range(MLA_HEADS):
            sl = slice(hd * LANES, (hd + 1) * LANES)
            q_ref[rows, sl] = _norm_rope_block(qf[:, sl], gs96_ref[...], 1.0 / (MLA_NOPE + MLA_ROPE), qn_ref[...],
                                               cm, sm, q_scale).astype(BF16)
            k_ref[rows, sl] = _norm_rope_block(kf[:, sl] + kr, gs96_ref[...], 1.0 / (MLA_NOPE + MLA_ROPE), kn_ref[...],
                                               cm, sm, 1.0).astype(BF16)
        for b in range(4):
            sl = slice(b * LANES, (b + 1) * LANES)
            dq_ref[rows, sl] = _norm_rope_block(z[:, 512 + b * LANES:512 + (b + 1) * LANES],
                                                gs32_ref[...], 1.0 / DIFF_HEAD, dqn_ref[...],
                                                cd, sd, d_scale).astype(BF16)
            dk_ref[rows, sl] = _norm_rope_block(z[:, 1024 + b * LANES:1024 + (b + 1) * LANES],
                                                gs32_ref[...], 1.0 / DIFF_HEAD, dkn_ref[...],
                                                cd, sd, 1.0).astype(BF16)


def _odd_in_kernel(x_ref, g1_ref, wg_ref, wu_ref, wd_ref, gm_ref, win_ref, wvt_ref,
                   qn_ref, kn_ref, gs64_ref, ca_ref, sa_ref,
                   xo_ref, q_ref, k_ref, vt_ref):
    q_scale = C_HEAD ** -0.5 * LOG2E
    for rows in _row_groups(x_ref.shape[0], TOKEN_SPLIT_IN):
        x = _swiglu_half(x_ref[rows, :], g1_ref[...], wg_ref, wu_ref, wd_ref)
        xo_ref[rows, :] = x
        h = _rms(x, gm_ref[...]).astype(BF16)
        z = jnp.dot(h, win_ref[...], preferred_element_type=F32)
        _store_vt_with_ones(vt_ref, rows, lax.dot_general(wvt_ref[...], h, NT_DIMS,
                                                          preferred_element_type=F32))
        ca, sa = ca_ref[rows, :], sa_ref[rows, :]
        for b in range(8):
            sl = slice(b * LANES, (b + 1) * LANES)
            q_ref[rows, sl] = _norm_rope_block(z[:, sl], gs64_ref[...], 1.0 / C_HEAD, qn_ref[...],
                                               ca, sa, q_scale).astype(BF16)
        for b in range(2):
            sl = slice(b * LANES, (b + 1) * LANES)
            k_ref[rows, sl] = _norm_rope_block(z[:, 1024 + b * LANES:1024 + (b + 1) * LANES],
                                               gs64_ref[...], 1.0 / C_HEAD, kn_ref[...],
                                               ca, sa, 1.0).astype(BF16)


def _even_out_kernel(x_ref, oa_ref, ob_ref, woa_ref, wob_ref, g2_ref, wg_ref, wu_ref, wd_ref, xo_ref):
    for rows in _row_groups(x_ref.shape[0], TOKEN_SPLIT_OUT):
        x = x_ref[rows, :]
        x = x + jnp.dot(oa_ref[rows, :], woa_ref[...], preferred_element_type=F32)
        x = x + jnp.dot(ob_ref[rows, :], wob_ref[...], preferred_element_type=F32)
        xo_ref[rows, :] = _swiglu_half(x, g2_ref[...], wg_ref, wu_ref, wd_ref)


def _odd_out_kernel(x_ref, o_ref, wo_ref, g2_ref, wg_ref, wu_ref, wd_ref, xo_ref):
    for rows in _row_groups(x_ref.shape[0], TOKEN_SPLIT_OUT):
        x = x_ref[rows, :] + jnp.dot(o_ref[rows, :], wo_ref[...], preferred_element_type=F32)
        xo_ref[rows, :] = _swiglu_half(x, g2_ref[...], wg_ref, wu_ref, wd_ref)


def _pair_rows(ot0, ot1):
    return jnp.concatenate([ot0, ot1], axis=0).T


def _out_blocks(o_ref):
    return range(o_ref.shape[1] // LANES)


def _mla_attn_kernel(bounded, q_ref, k_ref, vt_ref, o_ref):
    for s in _out_blocks(o_ref):
        outs = []
        for hd in (2 * s, 2 * s + 1):
            sl = slice(hd * LANES, (hd + 1) * LANES)
            outs.append(_softmax_pv_t(q_ref[:, sl], k_ref[:, sl],
                                      vt_ref[hd * VT_ROWS:(hd + 1) * VT_ROWS, :], bounded))
        o_ref[:, s * LANES:(s + 1) * LANES] = _pair_rows(*outs).astype(BF16)


def _gqa_attn_kernel(bounded, q_ref, k_ref, vt_ref, o_ref):
    k = k_ref[...]
    lane = _lane_iota(q_ref.shape[0])
    for s in _out_blocks(o_ref):
        q = q_ref[:, s * LANES:(s + 1) * LANES]
        zero = jnp.zeros_like(q)
        ot0 = _softmax_pv_t(jnp.where(lane < C_HEAD, q, zero), k, vt_ref[0:VT_ROWS, :], bounded)
        ot1 = _softmax_pv_t(jnp.where(lane >= C_HEAD, q, zero), k, vt_ref[VT_ROWS:2 * VT_ROWS, :], bounded)
        o_ref[:, s * LANES:(s + 1) * LANES] = _pair_rows(ot0, ot1).astype(BF16)


def _diff_attn_kernel(lam_init, bounded, q_ref, k_ref, vt_ref, lp_ref, sub_ref, o_ref):
    lp = lp_ref[...]
    lam = (jnp.exp(jnp.sum(lp[0:1] * lp[1:2], axis=-1, keepdims=True))
           - jnp.exp(jnp.sum(lp[2:3] * lp[3:4], axis=-1, keepdims=True)) + lam_init)
    lane = _lane_iota(q_ref.shape[0])
    lo = lane < 2 * DIFF_HEAD
    inv = 1.0 / (2 * DIFF_HEAD)
    for s in _out_blocks(o_ref):
        sl = slice(s * LANES, (s + 1) * LANES)
        q, k = q_ref[:, sl], k_ref[:, sl]
        zero = jnp.zeros_like(q)
        outs = []
        for hd in range(2):
            base = hd * 2 * DIFF_HEAD
            m1 = (lane >= base) & (lane < base + DIFF_HEAD)
            m2 = (lane >= base + DIFF_HEAD) & (lane < base + 2 * DIFF_HEAD)
            vt = vt_ref[(2 * s + hd) * VT_ROWS:(2 * s + hd + 1) * VT_ROWS, :]
            a1 = _softmax_pv_t(jnp.where(m1, q, zero), k, vt, bounded)
            a2 = _softmax_pv_t(jnp.where(m2, q, zero), k, vt, bounded)
            outs.append(a1 - lam * a2)
        o = _pair_rows(outs[0], outs[1])
        o2 = o * o
        r0 = lax.rsqrt(jnp.sum(jnp.where(lo, o2, 0.0), axis=-1, keepdims=True) * inv + EPS)
        r1 = lax.rsqrt(jnp.sum(jnp.where(lo, 0.0, o2), axis=-1, keepdims=True) * inv + EPS)
        o = o * jnp.where(lo, r0, r1) * sub_ref[...] * (1.0 - lam_init)
        o_ref[:, sl] = o.astype(BF16)


def _tile(n, want):
    return want if n % want == 0 else n


def _const_spec(shape):
    nd = len(shape)
    return pl.BlockSpec(shape, lambda *_: (0,) * nd, pipeline_mode=pl.Buffered(1))


def _tok_spec(tm, width):
    return pl.BlockSpec((None, tm, width), lambda b, t: (b, t, 0))


def _tok_t_spec(tm, width):
    return pl.BlockSpec((None, width, tm), lambda b, t: (b, 0, t))


def _tab_spec(tm):
    return pl.BlockSpec((tm, LANES), lambda b, t: (t, 0))


def _token_call(body, name, x, consts, tabs, extra_tok, outs, tm):
    bsz, seq, _ = x.shape
    tok_inputs = [x] + list(extra_tok)
    in_specs = ([_tok_spec(tm, a.shape[-1]) for a in tok_inputs]
                + [_const_spec(c.shape) for c in consts]
                + [_tab_spec(tm) for _ in tabs])
    out_shape = [jax.ShapeDtypeStruct((bsz, w, seq) if tr else (bsz, seq, w), d) for w, d, tr in outs]
    out_specs = [_tok_t_spec(tm, w) if tr else _tok_spec(tm, w) for w, d, tr in outs]
    return pl.pallas_call(
        body, name=name,
        grid=(bsz, seq // tm),
        in_specs=in_specs, out_specs=out_specs, out_shape=out_shape,
        compiler_params=pltpu.CompilerParams(
            dimension_semantics=("arbitrary", "arbitrary"), vmem_limit_bytes=VMEM_LIMIT),
    )(*tok_inputs, *consts, *tabs)


def _attn_dispatch(body, name, logit_bound, *args):
    q, k, vt, extra, *static = args

    def run(bounded, q, k, vt, extra):
        tag = "bounded" if bounded else "shifted"
        tq = _tile(q.shape[1], QUERY_TILE_BOUNDED if bounded else QUERY_TILE_SHIFTED)
        n_sub = OUT_BLOCKS_BOUNDED if bounded else OUT_BLOCKS_SHIFTED
        return _attn_call(functools.partial(body, bounded), f"{name}_{tag}", q, k, vt, extra,
                          *static, tq, n_sub)

    return lax.cond(logit_bound <= LOGIT_BOUND, functools.partial(run, True),
                    functools.partial(run, False), q, k, vt, tuple(extra))


def _attn_call(body, name, q, k, vt, extra, q_blk, k_blk, n_blocks, kv_share, tq, n_sub):
    bsz, seq, _ = q.shape
    if kv_share == 1:
        k_spec = pl.BlockSpec((None, seq, k_blk * n_sub), lambda b, h, t: (b, 0, h))
        vt_spec = pl.BlockSpec((None, 2 * VT_ROWS * n_sub, seq), lambda b, h, t: (b, h, 0))
    else:
        assert kv_share % n_sub == 0
        k_spec = pl.BlockSpec((None, seq, k_blk), lambda b, h, t: (b, 0, (h * n_sub) // kv_share))
        vt_spec = pl.BlockSpec((None, 2 * VT_ROWS, seq), lambda b, h, t: (b, (h * n_sub) // kv_share, 0))
    in_specs = [pl.BlockSpec((None, tq, q_blk * n_sub), lambda b, h, t: (b, t, h)), k_spec, vt_spec
                ] + [pl.BlockSpec(e.shape, lambda b, h, t: (0, 0)) for e in extra]
    return pl.pallas_call(
        body, name=name,
        grid=(bsz, n_blocks // n_sub, seq // tq),
        in_specs=in_specs,
        out_specs=pl.BlockSpec((None, tq, LANES * n_sub), lambda b, h, t: (b, t, h)),
        out_shape=jax.ShapeDtypeStruct((bsz, seq, n_blocks * LANES), BF16),
        compiler_params=pltpu.CompilerParams(
            dimension_semantics=("arbitrary", "arbitrary", "arbitrary"), vmem_limit_bytes=VMEM_LIMIT),
    )(q, k, vt, *extra)


def _rope_lane_tables(pos_a, pos_b):
    half = ROPE_DIM // 2
    inv = ROPE_THETA ** (-jnp.arange(0, ROPE_DIM, 2, dtype=F32) / ROPE_DIM)
    lane = np.arange(LANES)
    sign = jnp.asarray(np.where(lane % ROPE_DIM < half, -1.0, 1.0), F32)
    use_b = jnp.asarray((lane // ROPE_DIM) % 2 == 1)
    freq = inv[lane % half]
    ang_a = pos_a.astype(F32)[:, None] * freq[None, :]
    ang_b = pos_b.astype(F32)[:, None] * freq[None, :]
    cos = jnp.where(use_b[None, :], jnp.cos(ang_b), jnp.cos(ang_a))
    sin = jnp.where(use_b[None, :], jnp.sin(ang_b), jnp.sin(ang_a)) * sign[None, :]
    return cos, sin


def _group_sum_matrix(group):
    lane = np.arange(LANES)
    g = (lane[:, None] // group == lane[None, :] // group).astype(np.float32)
    return jnp.asarray(np.concatenate([g, g], axis=0), BF16)


def _row(v):
    return v.reshape(1, -1).astype(F32)


def _max_abs(v):
    return jnp.max(jnp.abs(v.astype(F32)))


def kernel(x, ffn1_norm, ffn1_w_gate, ffn1_w_up, ffn1_w_down, mix_norm, ffn2_norm, ffn2_w_gate, ffn2_w_up, ffn2_w_down, ab_w_in, mla_q_lora_norm, mla_w_uq, mla_kv_lora_norm, mla_w_ukv, mla_q_norm, mla_k_norm, diff_q_norm, diff_k_norm, diff_lambda_q1, diff_lambda_k1, diff_lambda_q2, diff_lambda_k2, diff_subln, ab_w_out, c_w_in, c_q_norm, c_k_norm, c_w_out):
    bsz, seq, d = x.shape
    depth = ffn1_norm.shape[0]
    tm = _tile(seq, TOKEN_TILE)

    pos = jnp.arange(seq, dtype=jnp.int32)
    cos_t, sin_t = _rope_lane_tables(pos, pos)
    cos_ax, sin_ax = _rope_lane_tables(pos // GRID_W, pos % GRID_W)
    lane = np.arange(LANES)
    mla_rot = jnp.asarray((lane >= MLA_NOPE) & (lane < MLA_NOPE + MLA_ROPE))
    cos_m = jnp.where(mla_rot[None, :], cos_t, 1.0)
    sin_m = jnp.where(mla_rot[None, :], sin_t, 0.0)
    gs_all, gs64, gs32 = _group_sum_matrix(LANES), _group_sum_matrix(C_HEAD), _group_sum_matrix(DIFF_HEAD)

    for i in range(depth):
        j = i // 2
        ffn1 = (_row(ffn1_norm[i]), ffn1_w_gate[i].astype(BF16), ffn1_w_up[i].astype(BF16),
                ffn1_w_down[i].astype(BF16))
        ffn2 = (_row(ffn2_norm[i]), ffn2_w_gate[i].astype(BF16), ffn2_w_up[i].astype(BF16),
                ffn2_w_down[i].astype(BF16))
        if i % 2 == 0:
            lam_init = 0.8 - 0.6 * math.exp(-0.3 * i)
            w = ab_w_in[j]
            kr_blk = jnp.pad(w[:, 384:416], ((0, 0), (MLA_NOPE, LANES - MLA_NOPE - MLA_ROPE)))
            w_in = jnp.concatenate([w[:, 0:384], kr_blk, w[:, 416:1440]], axis=1).astype(BF16)
            w_vt = w[:, 1440:].T.astype(BF16)
            w_uq = jnp.pad(mla_w_uq[j].reshape(MLA_Q_RANK, MLA_HEADS, MLA_NOPE + MLA_ROPE),
                           ((0, 0), (0, 0), (0, LANES - MLA_NOPE - MLA_ROPE))
                           ).reshape(MLA_Q_RANK, MLA_HEADS * LANES).astype(BF16)
            ukv = mla_w_ukv[j].reshape(MLA_KV_RANK, MLA_HEADS, MLA_NOPE + MLA_V)
            w_uk = jnp.pad(ukv[:, :, :MLA_NOPE], ((0, 0), (0, 0), (0, LANES - MLA_NOPE))
                           ).reshape(MLA_KV_RANK, MLA_HEADS * LANES).astype(BF16)
            w_uvt = ukv[:, :, MLA_NOPE:].reshape(MLA_KV_RANK, MLA_HEADS * MLA_V).T.astype(BF16)
            pad96 = (0, LANES - MLA_NOPE - MLA_ROPE)
            consts = ffn1 + (_row(mix_norm[i]), w_in, w_vt,
                             _row(mla_q_lora_norm[j]), w_uq, _row(mla_kv_lora_norm[j]), w_uk, w_uvt,
                             _row(jnp.pad(mla_q_norm[j], pad96)), _row(jnp.pad(mla_k_norm[j], pad96)),
                             _row(jnp.tile(diff_q_norm[j], LANES // DIFF_HEAD)),
                             _row(jnp.tile(diff_k_norm[j], LANES // DIFF_HEAD)),
                             gs_all, gs32)
            x, q, k, vt, dq, dk, dvt = _token_call(
                _even_in_kernel, f"even_in_{i}", x, consts, (cos_m, sin_m, cos_t, sin_t), (),
                ((d, F32, False), (1024, BF16, False), (1024, BF16, False),
                 (VT_ROWS * MLA_HEADS, BF16, True), (512, BF16, False), (512, BF16, False),
                 (VT_ROWS * DIFF_HEADS, BF16, True)), tm)
            mla_dim = MLA_NOPE + MLA_ROPE
            mla_bound = (mla_dim ** 0.5 * LOG2E) * _max_abs(mla_q_norm[j]) * _max_abs(mla_k_norm[j])
            o_mla = _attn_dispatch(_mla_attn_kernel, f"mla_attn_{i}", mla_bound, q, k, vt, (),
                                   2 * LANES, 2 * LANES, MLA_HEADS // 2, 1)
            lam_p = jnp.stack([diff_lambda_q1[j], diff_lambda_k1[j],
                               diff_lambda_q2[j], diff_lambda_k2[j]]).astype(F32)
            sub = _row(jnp.tile(diff_subln[j], 2))
            diff_bound = (DIFF_HEAD ** 0.5 * LOG2E) * _max_abs(diff_q_norm[j]) * _max_abs(diff_k_norm[j])
            o_diff = _attn_dispatch(functools.partial(_diff_attn_kernel, lam_init), f"diff_attn_{i}",
                                    diff_bound, dq, dk, dvt, (lam_p, sub),
                                    LANES, LANES, DIFF_HEADS // 2, 1)
            wo = ab_w_out[j].astype(BF16)
            consts = (wo[:MLA_HEADS * MLA_V], wo[MLA_HEADS * MLA_V:]) + ffn2
            (x,) = _token_call(_even_out_kernel, f"even_out_{i}", x, consts, (), (o_mla, o_diff),
                               ((d, F32, False),), tm)
        else:
            w = c_w_in[j]
            wq = w[:, :C_HEADS * C_HEAD].reshape(d, 2, 2, C_GROUPS, C_HEAD)
            wq = wq.transpose(0, 1, 3, 2, 4).reshape(d, C_HEADS * C_HEAD)
            n_qk = (C_HEADS + C_KV_HEADS) * C_HEAD
            w_in = jnp.concatenate([wq, w[:, C_HEADS * C_HEAD:n_qk]], axis=1).astype(BF16)
            w_vt = w[:, n_qk:].T.astype(BF16)
            consts = ffn1 + (_row(mix_norm[i]), w_in, w_vt,
                             _row(jnp.tile(c_q_norm[j], 2)), _row(jnp.tile(c_k_norm[j], 2)), gs64)
            x, q, k, vt = _token_call(
                _odd_in_kernel, f"odd_in_{i}", x, consts, (cos_ax, sin_ax), (),
                ((d, F32, False), (1024, BF16, False), (256, BF16, False),
                 (VT_ROWS * C_KV_HEADS, BF16, True)), tm)
            gqa_bound = (C_HEAD ** 0.5 * LOG2E) * _max_abs(c_q_norm[j]) * _max_abs(c_k_norm[j])
            o = _attn_dispatch(_gqa_attn_kernel, f"gqa_attn_{i}", gqa_bound, q, k, vt, (),
                               LANES, LANES, C_HEADS // 2, C_GROUPS)
            wo = c_w_out[j].reshape(2, 2, C_GROUPS, C_HEAD, d).transpose(0, 2, 1, 3, 4)
            wo = wo.reshape(C_HEADS * C_HEAD, d).astype(BF16)
            consts = (wo,) + ffn2
            (x,) = _token_call(_odd_out_kernel, f"odd_out_{i}", x, consts, (), (o,),
                               ((d, F32, False),), tm)
    return x
```

```python
import functools
import math

import jax
import jax.numpy as jnp
import numpy as np
from jax import lax
from jax.experimental import pallas as pl
from jax.experimental.pallas import tpu as pltpu

F32 = jnp.float32
BF16 = jnp.bfloat16

LANES = 128
EPS = 1e-6
ROPE_THETA = 10000.0
GRID_W = 64
LOG2E = math.log2(math.e)

MLA_HEADS, MLA_Q_RANK, MLA_KV_RANK = 8, 256, 128
MLA_NOPE, MLA_ROPE, MLA_V = 64, 32, 64
MLA_DIM = MLA_NOPE + MLA_ROPE
DIFF_HEADS, DIFF_HEAD = 8, 32
DIFF_W = DIFF_HEADS * 2 * DIFF_HEAD
C_HEADS, C_KV_HEADS, C_HEAD = 16, 4, 64
C_GROUPS = C_HEADS // C_KV_HEADS
C_Q_W, C_KV_W = C_HEADS * C_HEAD, C_KV_HEADS * C_HEAD
ROPE_DIM = 32

Z_CKV = MLA_Q_RANK
Z_KR = Z_CKV + MLA_KV_RANK
Z_DQ = Z_KR + LANES
Z_DK = Z_DQ + DIFF_W

VMEM_LIMIT = 56 * 1024 * 1024
TOKEN_TILE = 512
TOKEN_SPLIT_IN = 2
TOKEN_SPLIT_OUT = 1
QUERY_TILE_BOUNDED = 1024
QUERY_TILE_SHIFTED = 512
OUT_BLOCKS_BOUNDED = 2
OUT_BLOCKS_SHIFTED = 1
NT_DIMS = (((1,), (1,)), ((), ()))
LOGIT_BOUND = 64.0
HEAD_V = 64
ONES_ROWS = 64
VT_ROWS = HEAD_V + ONES_ROWS


def _rms(x, g):
    ms = jnp.mean(x * x, axis=-1, keepdims=True)
    return x * lax.rsqrt(ms + EPS) * g


def _bdot(a, b):
    return jnp.dot(a.astype(BF16), b, preferred_element_type=F32)


def _swiglu_half(x, g, wg, wu, wd):
    h = _rms(x, g).astype(BF16)
    gate = jnp.dot(h, wg[...], preferred_element_type=F32)
    up = jnp.dot(h, wu[...], preferred_element_type=F32)
    act = (gate / (1.0 + jnp.exp(-gate))) * up
    return x + 0.5 * jnp.dot(act.astype(BF16), wd[...], preferred_element_type=F32)


def _lane_iota(rows):
    return lax.broadcasted_iota(jnp.int32, (rows, LANES), 1)


def _norm_rope_block(x, gsum, inv_count, gain, cos, sin, post_scale):
    lane = _lane_iota(x.shape[0])
    x2 = x * x
    hi = x2.astype(BF16)
    lo = (x2 - hi.astype(F32)).astype(BF16)
    ss = jnp.dot(jnp.concatenate([hi, lo], axis=1), gsum, preferred_element_type=F32)
    y = x * lax.rsqrt(ss * inv_count + EPS) * gain
    up = pltpu.roll(y, LANES - ROPE_DIM // 2, axis=1)
    dn = pltpu.roll(y, ROPE_DIM // 2, axis=1)
    partner = jnp.where((lane % ROPE_DIM) < ROPE_DIM // 2, up, dn)
    y = y * cos + partner * sin
    if post_scale != 1.0:
        y = y * post_scale
    return y


def _softmax_pv_t(qm, k, vt, bounded):
    st = lax.dot_general(k, qm, NT_DIMS, preferred_element_type=F32)
    if not bounded:
        st = st - jnp.max(st, axis=0, keepdims=True)
    ot = jnp.dot(vt, jnp.exp2(st).astype(BF16), preferred_element_type=F32)
    return ot[:HEAD_V] / ot[HEAD_V:HEAD_V + 1]


def _store_vt_with_ones(vt_ref, cols, vt):
    ones = jnp.ones((ONES_ROWS, vt.shape[1]), BF16)
    for hd in range(vt.shape[0] // HEAD_V):
        vt_ref[hd * VT_ROWS:hd * VT_ROWS + HEAD_V, cols] = vt[hd * HEAD_V:(hd + 1) * HEAD_V].astype(BF16)
        vt_ref[hd * VT_ROWS + HEAD_V:(hd + 1) * VT_ROWS, cols] = ones


def _row_groups(n_rows, split):
    step = n_rows // split
    return [slice(r * step, (r + 1) * step) for r in range(split)]


def _even_in_kernel(x_ref, g1_ref, wg_ref, wu_ref, wd_ref, gm_ref, win_ref, wvt_ref,
                    gq_ref, wuq_ref, gkv_ref, wuk_ref, wuvt_ref, qn_ref, kn_ref, dqn_ref, dkn_ref,
                    gs96_ref, gs32_ref, cm_ref, sm_ref, cd_ref, sd_ref,
                    xo_ref, q_ref, k_ref, vt_ref, dq_ref, dk_ref, dvt_ref):
    q_scale = MLA_DIM ** -0.5 * LOG2E
    d_scale = DIFF_HEAD ** -0.5 * LOG2E
    xo_ref[...] = _swiglu_half(x_ref[...], g1_ref[...], wg_ref, wu_ref, wd_ref)
    for rows in _row_groups(x_ref.shape[0], TOKEN_SPLIT_IN):
        x = xo_ref[rows, :]
        h = _rms(x, gm_ref[...]).astype(BF16)
        z = jnp.dot(h, win_ref[...], preferred_element_type=F32)
        _store_vt_with_ones(dvt_ref, rows, lax.dot_general(wvt_ref[...], h, NT_DIMS,
                                                           preferred_element_type=F32))
        kr = z[:, Z_KR:Z_DQ]
        cm, sm, cd, sd = cm_ref[rows, :], sm_ref[rows, :], cd_ref[rows, :], sd_ref[rows, :]
        ckv = _rms(z[:, Z_CKV:Z_KR], gkv_ref[...]).astype(BF16)
        qf = _bdot(_rms(z[:, 0:Z_CKV], gq_ref[...]), wuq_ref[...])
        kf = jnp.dot(ckv, wuk_ref[...], preferred_element_type=F32)
        _store_vt_with_ones(vt_ref, rows, lax.dot_general(wuvt_ref[...], ckv, NT_DIMS,
                                                          preferred_element_type=F32))
        for hd in range(MLA_HEADS):
            sl = slice(hd * LANES, (hd + 1) * LANES)
            q_ref[rows, sl] = _norm_rope_block(qf[:, sl], gs96_ref[...], 1.0 / MLA_DIM, qn_ref[...],
                                               cm, sm, q_scale).astype(BF16)
            k_ref[rows, sl] = _norm_rope_block(kf[:, sl] + kr, gs96_ref[...], 1.0 / MLA_DIM, kn_ref[...],
                                               cm, sm, 1.0).astype(BF16)
        for b in range(DIFF_W // LANES):
            sl = slice(b * LANES, (b + 1) * LANES)
            dq_ref[rows, sl] = _norm_rope_block(z[:, Z_DQ + b * LANES:Z_DQ + (b + 1) * LANES],
                                                gs32_ref[...], 1.0 / DIFF_HEAD, dqn_ref[...],
                                                cd, sd, d_scale).astype(BF16)
            dk_ref[rows, sl] = _norm_rope_block(z[:, Z_DK + b * LANES:Z_DK + (b + 1) * LANES],
                                                gs32_ref[...], 1.0 / DIFF_HEAD, dkn_ref[...],
                                                cd, sd, 1.0).astype(BF16)


def _odd_in_kernel(x_ref, g1_ref, wg_ref, wu_ref, wd_ref, gm_ref, win_ref, wvt_ref,
                   qn_ref, kn_ref, gs64_ref, ca_ref, sa_ref,
                   xo_ref, q_ref, k_ref, vt_ref):
    q_scale = C_HEAD ** -0.5 * LOG2E
    xo_ref[...] = _swiglu_half(x_ref[...], g1_ref[...], wg_ref, wu_ref, wd_ref)
    for rows in _row_groups(x_ref.shape[0], TOKEN_SPLIT_IN):
        x = xo_ref[rows, :]
        h = _rms(x, gm_ref[...]).astype(BF16)
        z = jnp.dot(h, win_ref[...], preferred_element_type=F32)
        _store_vt_with_ones(vt_ref, rows, lax.dot_general(wvt_ref[...], h, NT_DIMS,
                                                          preferred_element_type=F32))
        ca, sa = ca_ref[rows, :], sa_ref[rows, :]
        for b in range(C_Q_W // LANES):
            sl = slice(b * LANES, (b + 1) * LANES)
            q_ref[rows, sl] = _norm_rope_block(z[:, sl], gs64_ref[...], 1.0 / C_HEAD, qn_ref[...],
                                               ca, sa, q_scale).astype(BF16)
        for b in range(C_KV_W // LANES):
            sl = slice(b * LANES, (b + 1) * LANES)
            k_ref[rows, sl] = _norm_rope_block(z[:, C_Q_W + b * LANES:C_Q_W + (b + 1) * LANES],
                                               gs64_ref[...], 1.0 / C_HEAD, kn_ref[...],
                                               ca, sa, 1.0).astype(BF16)


def _even_out_kernel(x_ref, oa_ref, ob_ref, woa_ref, wob_ref, g2_ref, wg_ref, wu_ref, wd_ref, xo_ref):
    for rows in _row_groups(x_ref.shape[0], TOKEN_SPLIT_OUT):
        x = x_ref[rows, :]
        x = x + jnp.dot(oa_ref[rows, :], woa_ref[...], preferred_element_type=F32)
        x = x + jnp.dot(ob_ref[rows, :], wob_ref[...], preferred_element_type=F32)
        xo_ref[rows, :] = _swiglu_half(x, g2_ref[...], wg_ref, wu_ref, wd_ref)


def _odd_out_kernel(x_ref, o_ref, wo_ref, g2_ref, wg_ref, wu_ref, wd_ref, xo_ref):
    for rows in _row_groups(x_ref.shape[0], TOKEN_SPLIT_OUT):
        x = x_ref[rows, :] + jnp.dot(o_ref[rows, :], wo_ref[...], preferred_element_type=F32)
        xo_ref[rows, :] = _swiglu_half(x, g2_ref[...], wg_ref, wu_ref, wd_ref)


def _pair_rows(ot0, ot1):
    return jnp.concatenate([ot0, ot1], axis=0).T


def _out_blocks(o_ref):
    return range(o_ref.shape[1] // LANES)


def _mla_attn_kernel(bounded, q_ref, k_ref, vt_ref, o_ref):
    for s in _out_blocks(o_ref):
        outs = []
        for hd in (2 * s, 2 * s + 1):
            sl = slice(hd * LANES, (hd + 1) * LANES)
            outs.append(_softmax_pv_t(q_ref[:, sl], k_ref[:, sl],
                                      vt_ref[hd * VT_ROWS:(hd + 1) * VT_ROWS, :], bounded))
        o_ref[:, s * LANES:(s + 1) * LANES] = _pair_rows(*outs).astype(BF16)


def _gqa_attn_kernel(bounded, q_ref, k_ref, vt_ref, o_ref):
    k = k_ref[...]
    lane = _lane_iota(q_ref.shape[0])
    for s in _out_blocks(o_ref):
        q = q_ref[:, s * LANES:(s + 1) * LANES]
        zero = jnp.zeros_like(q)
        ot0 = _softmax_pv_t(jnp.where(lane < C_HEAD, q, zero), k, vt_ref[0:VT_ROWS, :], bounded)
        ot1 = _softmax_pv_t(jnp.where(lane >= C_HEAD, q, zero), k, vt_ref[VT_ROWS:2 * VT_ROWS, :], bounded)
        o_ref[:, s * LANES:(s + 1) * LANES] = _pair_rows(ot0, ot1).astype(BF16)


def _diff_attn_kernel(lam_init, bounded, q_ref, k_ref, vt_ref, lp_ref, sub_ref, o_ref):
    lp = lp_ref[...]
    lam = (jnp.exp(jnp.sum(lp[0:1] * lp[1:2], axis=-1, keepdims=True))
           - jnp.exp(jnp.sum(lp[2:3] * lp[3:4], axis=-1, keepdims=True)) + lam_init)
    lane = _lane_iota(q_ref.shape[0])
    lo = lane < 2 * DIFF_HEAD
    inv = 1.0 / (2 * DIFF_HEAD)
    for s in _out_blocks(o_ref):
        sl = slice(s * LANES, (s + 1) * LANES)
        q, k = q_ref[:, sl], k_ref[:, sl]
        zero = jnp.zeros_like(q)
        outs = []
        for hd in range(2):
            base = hd * 2 * DIFF_HEAD
            m1 = (lane >= base) & (lane < base + DIFF_HEAD)
            m2 = (lane >= base + DIFF_HEAD) & (lane < base + 2 * DIFF_HEAD)
            vt = vt_ref[(2 * s + hd) * VT_ROWS:(2 * s + hd + 1) * VT_ROWS, :]
            a1 = _softmax_pv_t(jnp.where(m1, q, zero), k, vt, bounded)
            a2 = _softmax_pv_t(jnp.where(m2, q, zero), k, vt, bounded)
            outs.append(a1 - lam * a2)
        o = _pair_rows(outs[0], outs[1])
        o2 = o * o
        r0 = lax.rsqrt(jnp.sum(jnp.where(lo, o2, 0.0), axis=-1, keepdims=True) * inv + EPS)
        r1 = lax.rsqrt(jnp.sum(jnp.where(lo, 0.0, o2), axis=-1, keepdims=True) * inv + EPS)
        o = o * jnp.where(lo, r0, r1) * sub_ref[...] * (1.0 - lam_init)
        o_ref[:, sl] = o.astype(BF16)


def _tile(n, want):
    return want if n % want == 0 else n


def _const_spec(shape):
    nd = len(shape)
    return pl.BlockSpec(shape, lambda *_: (0,) * nd, pipeline_mode=pl.Buffered(1))


def _tok_spec(tm, width):
    return pl.BlockSpec((None, tm, width), lambda b, t: (b, t, 0))


def _tok_t_spec(tm, width):
    return pl.BlockSpec((None, width, tm), lambda b, t: (b, 0, t))


def _tab_spec(tm):
    return pl.BlockSpec((tm, LANES), lambda b, t: (t, 0))


def _token_call(body, name, x, consts, tabs, extra_tok, outs, tm):
    bsz, seq, _ = x.shape
    tok_inputs = [x] + list(extra_tok)
    in_specs = ([_tok_spec(tm, a.shape[-1]) for a in tok_inputs]
                + [_const_spec(c.shape) for c in consts]
                + [_tab_spec(tm) for _ in tabs])
    out_shape = [jax.ShapeDtypeStruct((bsz, w, seq) if tr else (bsz, seq, w), d) for w, d, tr in outs]
    out_specs = [_tok_t_spec(tm, w) if tr else _tok_spec(tm, w) for w, d, tr in outs]
    return pl.pallas_call(
        body, name=name,
        grid=(bsz, seq // tm),
        in_specs=in_specs, out_specs=out_specs, out_shape=out_shape,
        compiler_params=pltpu.CompilerParams(
            dimension_semantics=("arbitrary", "arbitrary"), vmem_limit_bytes=VMEM_LIMIT),
    )(*tok_inputs, *consts, *tabs)


def _attn_dispatch(body, name, logit_bound, *args):
    q, k, vt, extra, *static = args

    def run(bounded, q, k, vt, extra):
        tag = "bounded" if bounded else "shifted"
        tq = _tile(q.shape[1], QUERY_TILE_BOUNDED if bounded else QUERY_TILE_SHIFTED)
        n_sub = OUT_BLOCKS_BOUNDED if bounded else OUT_BLOCKS_SHIFTED
        return _attn_call(functools.partial(body, bounded), f"{name}_{tag}", q, k, vt, extra,
                          *static, tq, n_sub)

    return lax.cond(logit_bound <= LOGIT_BOUND, functools.partial(run, True),
                    functools.partial(run, False), q, k, vt, tuple(extra))


def _attn_call(body, name, q, k, vt, extra, q_blk, k_blk, n_blocks, kv_share, tq, n_sub):
    bsz, seq, _ = q.shape
    if kv_share == 1:
        k_spec = pl.BlockSpec((None, seq, k_blk * n_sub), lambda b, h, t: (b, 0, h))
        vt_spec = pl.BlockSpec((None, 2 * VT_ROWS * n_sub, seq), lambda b, h, t: (b, h, 0))
    else:
        assert kv_share % n_sub == 0
        k_spec = pl.BlockSpec((None, seq, k_blk), lambda b, h, t: (b, 0, (h * n_sub) // kv_share))
        vt_spec = pl.BlockSpec((None, 2 * VT_ROWS, seq), lambda b, h, t: (b, (h * n_sub) // kv_share, 0))
    in_specs = [pl.BlockSpec((None, tq, q_blk * n_sub), lambda b, h, t: (b, t, h)), k_spec, vt_spec
                ] + [pl.BlockSpec(e.shape, lambda b, h, t: (0, 0)) for e in extra]
    return pl.pallas_call(
        body, name=name,
        grid=(bsz, n_blocks // n_sub, seq // tq),
        in_specs=in_specs,
        out_specs=pl.BlockSpec((None, tq, LANES * n_sub), lambda b, h, t: (b, t, h)),
        out_shape=jax.ShapeDtypeStruct((bsz, seq, n_blocks * LANES), BF16),
        compiler_params=pltpu.CompilerParams(
            dimension_semantics=("arbitrary", "arbitrary", "arbitrary"), vmem_limit_bytes=VMEM_LIMIT),
    )(q, k, vt, *extra)


def _rope_lane_tables(pos_a, pos_b):
    half = ROPE_DIM // 2
    inv = ROPE_THETA ** (-jnp.arange(0, ROPE_DIM, 2, dtype=F32) / ROPE_DIM)
    lane = np.arange(LANES)
    sign = jnp.asarray(np.where(lane % ROPE_DIM < half, -1.0, 1.0), F32)
    use_b = jnp.asarray((lane // ROPE_DIM) % 2 == 1)
    freq = inv[lane % half]
    ang_a = pos_a.astype(F32)[:, None] * freq[None, :]
    ang_b = pos_b.astype(F32)[:, None] * freq[None, :]
    cos = jnp.where(use_b[None, :], jnp.cos(ang_b), jnp.cos(ang_a))
    sin = jnp.where(use_b[None, :], jnp.sin(ang_b), jnp.sin(ang_a)) * sign[None, :]
    return cos, sin


def _group_sum_matrix(group):
    lane = np.arange(LANES)
    g = (lane[:, None] // group == lane[None, :] // group).astype(np.float32)
    return jnp.asarray(np.concatenate([g, g], axis=0), BF16)


def _row(v):
    return v.reshape(1, -1).astype(F32)


def _max_abs(v):
    return jnp.max(jnp.abs(v.astype(F32)))


def kernel(x, ffn1_norm, ffn1_w_gate, ffn1_w_up, ffn1_w_down, mix_norm, ffn2_norm, ffn2_w_gate, ffn2_w_up, ffn2_w_down, ab_w_in, mla_q_lora_norm, mla_w_uq, mla_kv_lora_norm, mla_w_ukv, mla_q_norm, mla_k_norm, diff_q_norm, diff_k_norm, diff_lambda_q1, diff_lambda_k1, diff_lambda_q2, diff_lambda_k2, diff_subln, ab_w_out, c_w_in, c_q_norm, c_k_norm, c_w_out):
    bsz, seq, d = x.shape
    depth = ffn1_norm.shape[0]
    tm = _tile(seq, TOKEN_TILE)

    pos = jnp.arange(seq, dtype=jnp.int32)
    cos_t, sin_t = _rope_lane_tables(pos, pos)
    cos_ax, sin_ax = _rope_lane_tables(pos // GRID_W, pos % GRID_W)
    lane = np.arange(LANES)
    mla_rot = jnp.asarray((lane >= MLA_NOPE) & (lane < MLA_NOPE + MLA_ROPE))
    cos_m = jnp.where(mla_rot[None, :], cos_t, 1.0)
    sin_m = jnp.where(mla_rot[None, :], sin_t, 0.0)
    gs_all, gs64, gs32 = _group_sum_matrix(LANES), _group_sum_matrix(C_HEAD), _group_sum_matrix(DIFF_HEAD)

    for i in range(depth):
        j = i // 2
        ffn1 = (_row(ffn1_norm[i]), ffn1_w_gate[i].astype(BF16), ffn1_w_up[i].astype(BF16),
                ffn1_w_down[i].astype(BF16))
        ffn2 = (_row(ffn2_norm[i]), ffn2_w_gate[i].astype(BF16), ffn2_w_up[i].astype(BF16),
                ffn2_w_down[i].astype(BF16))
        if i % 2 == 0:
            lam_init = 0.8 - 0.6 * math.exp(-0.3 * i)
            w = ab_w_in[j].astype(BF16)
            o_kr = MLA_Q_RANK + MLA_KV_RANK
            o_dq = o_kr + MLA_ROPE
            o_dv = o_dq + 2 * DIFF_W
            kr_blk = jnp.pad(w[:, o_kr:o_dq], ((0, 0), (MLA_NOPE, LANES - MLA_DIM)))
            w_in = jnp.concatenate([w[:, 0:o_kr], kr_blk, w[:, o_dq:o_dv]], axis=1)
            w_vt = w[:, o_dv:].T
            w_uq = jnp.pad(mla_w_uq[j].reshape(MLA_Q_RANK, MLA_HEADS, MLA_DIM),
                           ((0, 0), (0, 0), (0, LANES - MLA_DIM))
                           ).reshape(MLA_Q_RANK, MLA_HEADS * LANES).astype(BF16)
            ukv = mla_w_ukv[j].reshape(MLA_KV_RANK, MLA_HEADS, MLA_NOPE + MLA_V)
            w_uk = jnp.pad(ukv[:, :, :MLA_NOPE], ((0, 0), (0, 0), (0, LANES - MLA_NOPE))
                           ).reshape(MLA_KV_RANK, MLA_HEADS * LANES).astype(BF16)
            w_uvt = ukv[:, :, MLA_NOPE:].reshape(MLA_KV_RANK, MLA_HEADS * MLA_V).T.astype(BF16)
            pad96 = (0, LANES - MLA_DIM)
            consts = ffn1 + (_row(mix_norm[i]), w_in, w_vt,
                             _row(mla_q_lora_norm[j]), w_uq, _row(mla_kv_lora_norm[j]), w_uk, w_uvt,
                             _row(jnp.pad(mla_q_norm[j], pad96)), _row(jnp.pad(mla_k_norm[j], pad96)),
                             _row(jnp.tile(diff_q_norm[j], LANES // DIFF_HEAD)),
                             _row(jnp.tile(diff_k_norm[j], LANES // DIFF_HEAD)),
                             gs_all, gs32)
            x, q, k, vt, dq, dk, dvt = _token_call(
                _even_in_kernel, f"even_in_{i}", x, consts, (cos_m, sin_m, cos_t, sin_t), (),
                ((d, F32, False), (MLA_HEADS * LANES, BF16, False), (MLA_HEADS * LANES, BF16, False),
                 (VT_ROWS * MLA_HEADS, BF16, True), (DIFF_W, BF16, False), (DIFF_W, BF16, False),
                 (VT_ROWS * DIFF_HEADS, BF16, True)), tm)
            mla_bound = (MLA_DIM ** 0.5 * LOG2E) * _max_abs(mla_q_norm[j]) * _max_abs(mla_k_norm[j])
            o_mla = _attn_dispatch(_mla_attn_kernel, f"mla_attn_{i}", mla_bound, q, k, vt, (),
                                   2 * LANES, 2 * LANES, MLA_HEADS // 2, 1)
            lam_p = jnp.stack([diff_lambda_q1[j], diff_lambda_k1[j],
                               diff_lambda_q2[j], diff_lambda_k2[j]]).astype(F32)
            sub = _row(jnp.tile(diff_subln[j], 2))
            diff_bound = (DIFF_HEAD ** 0.5 * LOG2E) * _max_abs(diff_q_norm[j]) * _max_abs(diff_k_norm[j])
            o_diff = _attn_dispatch(functools.partial(_diff_attn_kernel, lam_init), f"diff_attn_{i}",
                                    diff_bound, dq, dk, dvt, (lam_p, sub),
                                    LANES, LANES, DIFF_HEADS // 2, 1)
            wo = ab_w_out[j].astype(BF16)
            consts = (wo[:MLA_HEADS * MLA_V], wo[MLA_HEADS * MLA_V:]) + ffn2
            (x,) = _token_call(_even_out_kernel, f"even_out_{i}", x, consts, (), (o_mla, o_diff),
                               ((d, F32, False),), tm)
        else:
            w = c_w_in[j].astype(BF16)
            wq = w[:, :C_Q_W].reshape(d, 2, 2, C_GROUPS, C_HEAD)
            wq = wq.transpose(0, 1, 3, 2, 4).reshape(d, C_Q_W)
            w_in = jnp.concatenate([wq, w[:, C_Q_W:C_Q_W + C_KV_W]], axis=1)
            w_vt = w[:, C_Q_W + C_KV_W:].T
            consts = ffn1 + (_row(mix_norm[i]), w_in, w_vt,
                             _row(jnp.tile(c_q_norm[j], 2)), _row(jnp.tile(c_k_norm[j], 2)), gs64)
            x, q, k, vt = _token_call(
                _odd_in_kernel, f"odd_in_{i}", x, consts, (cos_ax, sin_ax), (),
                ((d, F32, False), (C_Q_W, BF16, False), (C_KV_W, BF16, False),
                 (VT_ROWS * C_KV_HEADS, BF16, True)), tm)
            gqa_bound = (C_HEAD ** 0.5 * LOG2E) * _max_abs(c_q_norm[j]) * _max_abs(c_k_norm[j])
            o = _attn_dispatch(_gqa_attn_kernel, f"gqa_attn_{i}", gqa_bound, q, k, vt, (),
                               LANES, LANES, C_HEADS // 2, C_GROUPS)
            wo = c_w_out[j].astype(BF16).reshape(2, 2, C_GROUPS, C_HEAD, d).transpose(0, 2, 1, 3, 4)
            wo = wo.reshape(C_Q_W, d)
            consts = (wo,) + ffn2
            (x,) = _token_call(_odd_out_kernel, f"odd_out_{i}", x, consts, (), (o,),
                               ((d, F32, False),), tm)
    return x
```

```python
import functools
import math

import jax
import jax.numpy as jnp
import numpy as np
from jax import lax
from jax.experimental import pallas as pl
from jax.experimental.pallas import tpu as pltpu

F32 = jnp.float32
BF16 = jnp.bfloat16

LANES = 128
EPS = 1e-6
ROPE_THETA = 10000.0
GRID_W = 64
LOG2E = math.log2(math.e)

MLA_HEADS, MLA_Q_RANK, MLA_KV_RANK = 8, 256, 128
MLA_NOPE, MLA_ROPE, MLA_V = 64, 32, 64
MLA_DIM = MLA_NOPE + MLA_ROPE
DIFF_HEADS, DIFF_HEAD = 8, 32
DIFF_W = DIFF_HEADS * 2 * DIFF_HEAD
C_HEADS, C_KV_HEADS, C_HEAD = 16, 4, 64
C_GROUPS = C_HEADS // C_KV_HEADS
C_Q_W, C_KV_W = C_HEADS * C_HEAD, C_KV_HEADS * C_HEAD
ROPE_DIM = 32

Z_CKV = MLA_Q_RANK
Z_KR = Z_CKV + MLA_KV_RANK
Z_DQ = Z_KR + LANES
Z_DK = Z_DQ + DIFF_W

VMEM_LIMIT = 56 * 1024 * 1024
TOKEN_TILE = 512
TOKEN_SPLIT_IN = 2
TOKEN_SPLIT_OUT = 1
QUERY_TILE_BOUNDED = 1024
QUERY_TILE_SHIFTED = 512
OUT_BLOCKS_BOUNDED = 2
OUT_BLOCKS_SHIFTED = 1
NT_DIMS = (((1,), (1,)), ((), ()))
LOGIT_BOUND = 0.0
HEAD_V = 64
ONES_ROWS = 64
VT_ROWS = HEAD_V + ONES_ROWS


def _rms(x, g):
    ms = jnp.mean(x * x, axis=-1, keepdims=True)
    return x * lax.rsqrt(ms + EPS) * g


def _bdot(a, b):
    return jnp.dot(a.astype(BF16), b, preferred_element_type=F32)


def _swiglu_half(x, g, wg, wu, wd):
    h = _rms(x, g).astype(BF16)
    gate = jnp.dot(h, wg[...], preferred_element_type=F32)
    up = jnp.dot(h, wu[...], preferred_element_type=F32)
    act = (gate / (1.0 + jnp.exp(-gate))) * up
    return x + 0.5 * jnp.dot(act.astype(BF16), wd[...], preferred_element_type=F32)


def _lane_iota(rows):
    return lax.broadcasted_iota(jnp.int32, (rows, LANES), 1)


def _norm_rope_block(x, gsum, inv_count, gain, cos, sin, post_scale):
    lane = _lane_iota(x.shape[0])
    x2 = x * x
    hi = x2.astype(BF16)
    lo = (x2 - hi.astype(F32)).astype(BF16)
    ss = jnp.dot(jnp.concatenate([hi, lo], axis=1), gsum, preferred_element_type=F32)
    y = x * lax.rsqrt(ss * inv_count + EPS) * gain
    up = pltpu.roll(y, LANES - ROPE_DIM // 2, axis=1)
    dn = pltpu.roll(y, ROPE_DIM // 2, axis=1)
    partner = jnp.where((lane % ROPE_DIM) < ROPE_DIM // 2, up, dn)
    y = y * cos + partner * sin
    if post_scale != 1.0:
        y = y * post_scale
    return y


def _softmax_pv_t(qm, k, vt, bounded):
    st = lax.dot_general(k, qm, NT_DIMS, preferred_element_type=F32)
    if not bounded:
        st = st - jnp.max(st, axis=0, keepdims=True)
    ot = jnp.dot(vt, jnp.exp2(st).astype(BF16), preferred_element_type=F32)
    return ot[:HEAD_V] / ot[HEAD_V:HEAD_V + 1]


def _store_vt_with_ones(vt_ref, cols, vt):
    ones = jnp.ones((ONES_ROWS, vt.shape[1]), BF16)
    for hd in range(vt.shape[0] // HEAD_V):
        vt_ref[hd * VT_ROWS:hd * VT_ROWS + HEAD_V, cols] = vt[hd * HEAD_V:(hd + 1) * HEAD_V].astype(BF16)
        vt_ref[hd * VT_ROWS + HEAD_V:(hd + 1) * VT_ROWS, cols] = ones


def _row_groups(n_rows, split):
    step = n_rows // split
    return [slice(r * step, (r + 1) * step) for r in range(split)]


def _even_in_kernel(x_ref, g1_ref, wg_ref, wu_ref, wd_ref, gm_ref, win_ref, wvt_ref,
                    gq_ref, wuq_ref, gkv_ref, wuk_ref, wuvt_ref, qn_ref, kn_ref, dqn_ref, dkn_ref,
                    gs96_ref, gs32_ref, cm_ref, sm_ref, cd_ref, sd_ref,
                    xo_ref, q_ref, k_ref, vt_ref, dq_ref, dk_ref, dvt_ref):
    q_scale = MLA_DIM ** -0.5 * LOG2E
    d_scale = DIFF_HEAD ** -0.5 * LOG2E
    xo_ref[...] = _swiglu_half(x_ref[...], g1_ref[...], wg_ref, wu_ref, wd_ref)
    for rows in _row_groups(x_ref.shape[0], TOKEN_SPLIT_IN):
        x = xo_ref[rows, :]
        h = _rms(x, gm_ref[...]).astype(BF16)
        z = jnp.dot(h, win_ref[...], preferred_element_type=F32)
        _store_vt_with_ones(dvt_ref, rows, lax.dot_general(wvt_ref[...], h, NT_DIMS,
                                                           preferred_element_type=F32))
        kr = z[:, Z_KR:Z_DQ]
        cm, sm, cd, sd = cm_ref[rows, :], sm_ref[rows, :], cd_ref[rows, :], sd_ref[rows, :]
        ckv = _rms(z[:, Z_CKV:Z_KR], gkv_ref[...]).astype(BF16)
        qf = _bdot(_rms(z[:, 0:Z_CKV], gq_ref[...]), wuq_ref[...])
        kf = jnp.dot(ckv, wuk_ref[...], preferred_element_type=F32)
        _store_vt_with_ones(vt_ref, rows, lax.dot_general(wuvt_ref[...], ckv, NT_DIMS,
                                                          preferred_element_type=F32))
        for hd in range(MLA_HEADS):
            sl = slice(hd * LANES, (hd + 1) * LANES)
            q_ref[rows, sl] = _norm_rope_block(qf[:, sl], gs96_ref[...], 1.0 / MLA_DIM, qn_ref[...],
                                               cm, sm, q_scale).astype(BF16)
            k_ref[rows, sl] = _norm_rope_block(kf[:, sl] + kr, gs96_ref[...], 1.0 / MLA_DIM, kn_ref[...],
                                               cm, sm, 1.0).astype(BF16)
        for b in range(DIFF_W // LANES):
            sl = slice(b * LANES, (b + 1) * LANES)
            dq_ref[rows, sl] = _norm_rope_block(z[:, Z_DQ + b * LANES:Z_DQ + (b + 1) * LANES],
                                                gs32_ref[...], 1.0 / DIFF_HEAD, dqn_ref[...],
                                                cd, sd, d_scale).astype(BF16)
            dk_ref[rows, sl] = _norm_rope_block(z[:, Z_DK + b * LANES:Z_DK + (b + 1) * LANES],
                                                gs32_ref[...], 1.0 / DIFF_HEAD, dkn_ref[...],
                                                cd, sd, 1.0).astype(BF16)


def _odd_in_kernel(x_ref, g1_ref, wg_ref, wu_ref, wd_ref, gm_ref, win_ref, wvt_ref,
                   qn_ref, kn_ref, gs64_ref, ca_ref, sa_ref,
                   xo_ref, q_ref, k_ref, vt_ref):
    q_scale = C_HEAD ** -0.5 * LOG2E
    xo_ref[...] = _swiglu_half(x_ref[...], g1_ref[...], wg_ref, wu_ref, wd_ref)
    for rows in _row_groups(x_ref.shape[0], TOKEN_SPLIT_IN):
        x = xo_ref[rows, :]
        h = _rms(x, gm_ref[...]).astype(BF16)
        z = jnp.dot(h, win_ref[...], preferred_element_type=F32)
        _store_vt_with_ones(vt_ref, rows, lax.dot_general(wvt_ref[...], h, NT_DIMS,
                                                          preferred_element_type=F32))
        ca, sa = ca_ref[rows, :], sa_ref[rows, :]
        for b in range(C_Q_W // LANES):
            sl = slice(b * LANES, (b + 1) * LANES)
            q_ref[rows, sl] = _norm_rope_block(z[:, sl], gs64_ref[...], 1.0 / C_HEAD, qn_ref[...],
                                               ca, sa, q_scale).astype(BF16)
        for b in range(C_KV_W // LANES):
            sl = slice(b * LANES, (b + 1) * LANES)
            k_ref[rows, sl] = _norm_rope_block(z[:, C_Q_W + b * LANES:C_Q_W + (b + 1) * LANES],
                                               gs64_ref[...], 1.0 / C_HEAD, kn_ref[...],
                                               ca, sa, 1.0).astype(BF16)


def _even_out_kernel(x_ref, oa_ref, ob_ref, woa_ref, wob_ref, g2_ref, wg_ref, wu_ref, wd_ref, xo_ref):
    for rows in _row_groups(x_ref.shape[0], TOKEN_SPLIT_OUT):
        x = x_ref[rows, :]
        x = x + jnp.dot(oa_ref[rows, :], woa_ref[...], preferred_element_type=F32)
        x = x + jnp.dot(ob_ref[rows, :], wob_ref[...], preferred_element_type=F32)
        xo_ref[rows, :] = _swiglu_half(x, g2_ref[...], wg_ref, wu_ref, wd_ref)


def _odd_out_kernel(x_ref, o_ref, wo_ref, g2_ref, wg_ref, wu_ref, wd_ref, xo_ref):
    for rows in _row_groups(x_ref.shape[0], TOKEN_SPLIT_OUT):
        x = x_ref[rows, :] + jnp.dot(o_ref[rows, :], wo_ref[...], preferred_element_type=F32)
        xo_ref[rows, :] = _swiglu_half(x, g2_ref[...], wg_ref, wu_ref, wd_ref)


def _pair_rows(ot0, ot1):
    return jnp.concatenate([ot0, ot1], axis=0).T


def _out_blocks(o_ref):
    return range(o_ref.shape[1] // LANES)


def _mla_attn_kernel(bounded, q_ref, k_ref, vt_ref, o_ref):
    for s in _out_blocks(o_ref):
        outs = []
        for hd in (2 * s, 2 * s + 1):
            sl = slice(hd * LANES, (hd + 1) * LANES)
            outs.append(_softmax_pv_t(q_ref[:, sl], k_ref[:, sl],
                                      vt_ref[hd * VT_ROWS:(hd + 1) * VT_ROWS, :], bounded))
        o_ref[:, s * LANES:(s + 1) * LANES] = _pair_rows(*outs).astype(BF16)


def _gqa_attn_kernel(bounded, q_ref, k_ref, vt_ref, o_ref):
    k = k_ref[...]
    lane = _lane_iota(q_ref.shape[0])
    for s in _out_blocks(o_ref):
        q = q_ref[:, s * LANES:(s + 1) * LANES]
        zero = jnp.zeros_like(q)
        ot0 = _softmax_pv_t(jnp.where(lane < C_HEAD, q, zero), k, vt_ref[0:VT_ROWS, :], bounded)
        ot1 = _softmax_pv_t(jnp.where(lane >= C_HEAD, q, zero), k, vt_ref[VT_ROWS:2 * VT_ROWS, :], bounded)
        o_ref[:, s * LANES:(s + 1) * LANES] = _pair_rows(ot0, ot1).astype(BF16)


def _diff_attn_kernel(lam_init, bounded, q_ref, k_ref, vt_ref, lp_ref, sub_ref, o_ref):
    lp = lp_ref[...]
    lam = (jnp.exp(jnp.sum(lp[0:1] * lp[1:2], axis=-1, keepdims=True))
           - jnp.exp(jnp.sum(lp[2:3] * lp[3:4], axis=-1, keepdims=True)) + lam_init)
    lane = _lane_iota(q_ref.shape[0])
    lo = lane < 2 * DIFF_HEAD
    inv = 1.0 / (2 * DIFF_HEAD)
    for s in _out_blocks(o_ref):
        sl = slice(s * LANES, (s + 1) * LANES)
        q, k = q_ref[:, sl], k_ref[:, sl]
        zero = jnp.zeros_like(q)
        outs = []
        for hd in range(2):
            base = hd * 2 * DIFF_HEAD
            m1 = (lane >= base) & (lane < base + DIFF_HEAD)
            m2 = (lane >= base + DIFF_HEAD) & (lane < base + 2 * DIFF_HEAD)
            vt = vt_ref[(2 * s + hd) * VT_ROWS:(2 * s + hd + 1) * VT_ROWS, :]
            a1 = _softmax_pv_t(jnp.where(m1, q, zero), k, vt, bounded)
            a2 = _softmax_pv_t(jnp.where(m2, q, zero), k, vt, bounded)
            outs.append(a1 - lam * a2)
        o = _pair_rows(outs[0], outs[1])
        o2 = o * o
        r0 = lax.rsqrt(jnp.sum(jnp.where(lo, o2, 0.0), axis=-1, keepdims=True) * inv + EPS)
        r1 = lax.rsqrt(jnp.sum(jnp.where(lo, 0.0, o2), axis=-1, keepdims=True) * inv + EPS)
        o = o * jnp.where(lo, r0, r1) * sub_ref[...] * (1.0 - lam_init)
        o_ref[:, sl] = o.astype(BF16)


def _tile(n, want):
    return want if n % want == 0 else n


def _const_spec(shape):
    nd = len(shape)
    return pl.BlockSpec(shape, lambda *_: (0,) * nd, pipeline_mode=pl.Buffered(1))


def _tok_spec(tm, width):
    return pl.BlockSpec((None, tm, width), lambda b, t: (b, t, 0))


def _tok_t_spec(tm, width):
    return pl.BlockSpec((None, width, tm), lambda b, t: (b, 0, t))


def _tab_spec(tm):
    return pl.BlockSpec((tm, LANES), lambda b, t: (t, 0))


def _token_call(body, name, x, consts, tabs, extra_tok, outs, tm):
    bsz, seq, _ = x.shape
    tok_inputs = [x] + list(extra_tok)
    in_specs = ([_tok_spec(tm, a.shape[-1]) for a in tok_inputs]
                + [_const_spec(c.shape) for c in consts]
                + [_tab_spec(tm) for _ in tabs])
    out_shape = [jax.ShapeDtypeStruct((bsz, w, seq) if tr else (bsz, seq, w), d) for w, d, tr in outs]
    out_specs = [_tok_t_spec(tm, w) if tr else _tok_spec(tm, w) for w, d, tr in outs]
    return pl.pallas_call(
        body, name=name,
        grid=(bsz, seq // tm),
        in_specs=in_specs, out_specs=out_specs, out_shape=out_shape,
        compiler_params=pltpu.CompilerParams(
            dimension_semantics=("arbitrary", "arbitrary"), vmem_limit_bytes=VMEM_LIMIT),
    )(*tok_inputs, *consts, *tabs)


def _attn_dispatch(body, name, logit_bound, *args):
    q, k, vt, extra, *static = args

    def run(bounded, q, k, vt, extra):
        tag = "bounded" if bounded else "shifted"
        tq = _tile(q.shape[1], QUERY_TILE_BOUNDED if bounded else QUERY_TILE_SHIFTED)
        n_sub = OUT_BLOCKS_BOUNDED if bounded else OUT_BLOCKS_SHIFTED
        return _attn_call(functools.partial(body, bounded), f"{name}_{tag}", q, k, vt, extra,
                          *static, tq, n_sub)

    return lax.cond(logit_bound <= LOGIT_BOUND, functools.partial(run, True),
                    functools.partial(run, False), q, k, vt, tuple(extra))


def _attn_call(body, name, q, k, vt, extra, q_blk, k_blk, n_blocks, kv_share, tq, n_sub):
    bsz, seq, _ = q.shape
    if kv_share == 1:
        k_spec = pl.BlockSpec((None, seq, k_blk * n_sub), lambda b, h, t: (b, 0, h))
        vt_spec = pl.BlockSpec((None, 2 * VT_ROWS * n_sub, seq), lambda b, h, t: (b, h, 0))
    else:
        assert kv_share % n_sub == 0
        k_spec = pl.BlockSpec((None, seq, k_blk), lambda b, h, t: (b, 0, (h * n_sub) // kv_share))
        vt_spec = pl.BlockSpec((None, 2 * VT_ROWS, seq), lambda b, h, t: (b, (h * n_sub) // kv_share, 0))
    in_specs = [pl.BlockSpec((None, tq, q_blk * n_sub), lambda b, h, t: (b, t, h)), k_spec, vt_spec
                ] + [pl.BlockSpec(e.shape, lambda b, h, t: (0, 0)) for e in extra]
    return pl.pallas_call(
        body, name=name,
        grid=(bsz, n_blocks // n_sub, seq // tq),
        in_specs=in_specs,
        out_specs=pl.BlockSpec((None, tq, LANES * n_sub), lambda b, h, t: (b, t, h)),
        out_shape=jax.ShapeDtypeStruct((bsz, seq, n_blocks * LANES), BF16),
        compiler_params=pltpu.CompilerParams(
            dimension_semantics=("arbitrary", "arbitrary", "arbitrary"), vmem_limit_bytes=VMEM_LIMIT),
    )(q, k, vt, *extra)


def _rope_lane_tables(pos_a, pos_b):
    half = ROPE_DIM // 2
    inv = ROPE_THETA ** (-jnp.arange(0, ROPE_DIM, 2, dtype=F32) / ROPE_DIM)
    lane = np.arange(LANES)
    sign = jnp.asarray(np.where(lane % ROPE_DIM < half, -1.0, 1.0), F32)
    use_b = jnp.asarray((lane // ROPE_DIM) % 2 == 1)
    freq = inv[lane % half]
    ang_a = pos_a.astype(F32)[:, None] * freq[None, :]
    ang_b = pos_b.astype(F32)[:, None] * freq[None, :]
    cos = jnp.where(use_b[None, :], jnp.cos(ang_b), jnp.cos(ang_a))
    sin = jnp.where(use_b[None, :], jnp.sin(ang_b), jnp.sin(ang_a)) * sign[None, :]
    return cos, sin


def _group_sum_matrix(group):
    lane = np.arange(LANES)
    g = (lane[:, None] // group == lane[None, :] // group).astype(np.float32)
    return jnp.asarray(np.concatenate([g, g], axis=0), BF16)


def _row(v):
    return v.reshape(1, -1).astype(F32)


def _max_abs(v):
    return jnp.max(jnp.abs(v.astype(F32)))


def kernel(x, ffn1_norm, ffn1_w_gate, ffn1_w_up, ffn1_w_down, mix_norm, ffn2_norm, ffn2_w_gate, ffn2_w_up, ffn2_w_down, ab_w_in, mla_q_lora_norm, mla_w_uq, mla_kv_lora_norm, mla_w_ukv, mla_q_norm, mla_k_norm, diff_q_norm, diff_k_norm, diff_lambda_q1, diff_lambda_k1, diff_lambda_q2, diff_lambda_k2, diff_subln, ab_w_out, c_w_in, c_q_norm, c_k_norm, c_w_out):
    bsz, seq, d = x.shape
    depth = ffn1_norm.shape[0]
    tm = _tile(seq, TOKEN_TILE)

    pos = jnp.arange(seq, dtype=jnp.int32)
    cos_t, sin_t = _rope_lane_tables(pos, pos)
    cos_ax, sin_ax = _rope_lane_tables(pos // GRID_W, pos % GRID_W)
    lane = np.arange(LANES)
    mla_rot = jnp.asarray((lane >= MLA_NOPE) & (lane < MLA_NOPE + MLA_ROPE))
    cos_m = jnp.where(mla_rot[None, :], cos_t, 1.0)
    sin_m = jnp.where(mla_rot[None, :], sin_t, 0.0)
    gs_all, gs64, gs32 = _group_sum_matrix(LANES), _group_sum_matrix(C_HEAD), _group_sum_matrix(DIFF_HEAD)

    for i in range(depth):
        j = i // 2
        ffn1 = (_row(ffn1_norm[i]), ffn1_w_gate[i].astype(BF16), ffn1_w_up[i].astype(BF16),
                ffn1_w_down[i].astype(BF16))
        ffn2 = (_row(ffn2_norm[i]), ffn2_w_gate[i].astype(BF16), ffn2_w_up[i].astype(BF16),
                ffn2_w_down[i].astype(BF16))
        if i % 2 == 0:
            lam_init = 0.8 - 0.6 * math.exp(-0.3 * i)
            w = ab_w_in[j].astype(BF16)
            o_kr = MLA_Q_RANK + MLA_KV_RANK
            o_dq = o_kr + MLA_ROPE
            o_dv = o_dq + 2 * DIFF_W
            kr_blk = jnp.pad(w[:, o_kr:o_dq], ((0, 0), (MLA_NOPE, LANES - MLA_DIM)))
            w_in = jnp.concatenate([w[:, 0:o_kr], kr_blk, w[:, o_dq:o_dv]], axis=1)
            w_vt = w[:, o_dv:].T
            w_uq = jnp.pad(mla_w_uq[j].reshape(MLA_Q_RANK, MLA_HEADS, MLA_DIM),
                           ((0, 0), (0, 0), (0, LANES - MLA_DIM))
                           ).reshape(MLA_Q_RANK, MLA_HEADS * LANES).astype(BF16)
            ukv = mla_w_ukv[j].reshape(MLA_KV_RANK, MLA_HEADS, MLA_NOPE + MLA_V)
            w_uk = jnp.pad(ukv[:, :, :MLA_NOPE], ((0, 0), (0, 0), (0, LANES - MLA_NOPE))
                           ).reshape(MLA_KV_RANK, MLA_HEADS * LANES).astype(BF16)
            w_uvt = ukv[:, :, MLA_NOPE:].reshape(MLA_KV_RANK, MLA_HEADS * MLA_V).T.astype(BF16)
            pad96 = (0, LANES - MLA_DIM)
            consts = ffn1 + (_row(mix_norm[i]), w_in, w_vt,
                             _row(mla_q_lora_norm[j]), w_uq, _row(mla_kv_lora_norm[j]), w_uk, w_uvt,
                             _row(jnp.pad(mla_q_norm[j], pad96)), _row(jnp.pad(mla_k_norm[j], pad96)),
                             _row(jnp.tile(diff_q_norm[j], LANES // DIFF_HEAD)),
                             _row(jnp.tile(diff_k_norm[j], LANES // DIFF_HEAD)),
                             gs_all, gs32)
            x, q, k, vt, dq, dk, dvt = _token_call(
                _even_in_kernel, f"even_in_{i}", x, consts, (cos_m, sin_m, cos_t, sin_t), (),
                ((d, F32, False), (MLA_HEADS * LANES, BF16, False), (MLA_HEADS * LANES, BF16, False),
                 (VT_ROWS * MLA_HEADS, BF16, True), (DIFF_W, BF16, False), (DIFF_W, BF16, False),
                 (VT_ROWS * DIFF_HEADS, BF16, True)), tm)
            mla_bound = (MLA_DIM ** 0.5 * LOG2E) * _max_abs(mla_q_norm[j]) * _max_abs(mla_k_norm[j])
            o_mla = _attn_dispatch(_mla_attn_kernel, f"mla_attn_{i}", mla_bound, q, k, vt, (),
                                   2 * LANES, 2 * LANES, MLA_HEADS // 2, 1)
            lam_p = jnp.stack([diff_lambda_q1[j], diff_lambda_k1[j],
                               diff_lambda_q2[j], diff_lambda_k2[j]]).astype(F32)
            sub = _row(jnp.tile(diff_subln[j], 2))
            diff_bound = (DIFF_HEAD ** 0.5 * LOG2E) * _max_abs(diff_q_norm[j]) * _max_abs(diff_k_norm[j])
            o_diff = _attn_dispatch(functools.partial(_diff_attn_kernel, lam_init), f"diff_attn_{i}",
                                    diff_bound, dq, dk, dvt, (lam_p, sub),
                                    LANES, LANES, DIFF_HEADS // 2, 1)
            wo = ab_w_out[j].astype(BF16)
            consts = (wo[:MLA_HEADS * MLA_V], wo[MLA_HEADS * MLA_V:]) + ffn2
            (x,) = _token_call(_even_out_kernel, f"even_out_{i}", x, consts, (), (o_mla, o_diff),
                               ((d, F32, False),), tm)
        else:
            w = c_w_in[j].astype(BF16)
            wq = w[:, :C_Q_W].reshape(d, 2, 2, C_GROUPS, C_HEAD)
            wq = wq.transpose(0, 1, 3, 2, 4).reshape(d, C_Q_W)
            w_in = jnp.concatenate([wq, w[:, C_Q_W:C_Q_W + C_KV_W]], axis=1)
            w_vt = w[:, C_Q_W + C_KV_W:].T
            consts = ffn1 + (_row(mix_norm[i]), w_in, w_vt,
                             _row(jnp.tile(c_q_norm[j], 2)), _row(jnp.tile(c_k_norm[j], 2)), gs64)
            x, q, k, vt = _token_call(
                _odd_in_kernel, f"odd_in_{i}", x, consts, (cos_ax, sin_ax), (),
                ((d, F32, False), (C_Q_W, BF16, False), (C_KV_W, BF16, False),
                 (VT_ROWS * C_KV_HEADS, BF16, True)), tm)
            gqa_bound = (C_HEAD ** 0.5 * LOG2E) * _max_abs(c_q_norm[j]) * _max_abs(c_k_norm[j])
            o = _attn_dispatch(_gqa_attn_kernel, f"gqa_attn_{i}", gqa_bound, q, k, vt, (),
                               LANES, LANES, C_HEADS // 2, C_GROUPS)
            wo = c_w_out[j].astype(BF16).reshape(2, 2, C_GROUPS, C_HEAD, d).transpose(0, 2, 1, 3, 4)
            wo = wo.reshape(C_Q_W, d)
            consts = (wo,) + ffn2
            (x,) = _token_call(_odd_out_kernel, f"odd_out_{i}", x, consts, (), (o,),
                               ((d, F32, False),), tm)
    return x
```

```python
import functools
import math

import jax
import jax.numpy as jnp
import numpy as np
from jax import lax
from jax.experimental import pallas as pl
from jax.experimental.pallas import tpu as pltpu

F32 = jnp.float32
BF16 = jnp.bfloat16
FP8 = jnp.float8_e4m3fn

LANES = 128
EPS = 1e-6
ROPE_THETA = 10000.0
GRID_W = 64
LOG2E = math.log2(math.e)

MLA_HEADS, MLA_Q_RANK, MLA_KV_RANK = 8, 256, 128
MLA_NOPE, MLA_ROPE, MLA_V = 64, 32, 64
MLA_DIM = MLA_NOPE + MLA_ROPE
DIFF_HEADS, DIFF_HEAD = 8, 32
DIFF_W = DIFF_HEADS * 2 * DIFF_HEAD
C_HEADS, C_KV_HEADS, C_HEAD = 16, 4, 64
C_GROUPS = C_HEADS // C_KV_HEADS
C_Q_W, C_KV_W = C_HEADS * C_HEAD, C_KV_HEADS * C_HEAD
ROPE_DIM = 32

Z_CKV = MLA_Q_RANK
Z_KR = Z_CKV + MLA_KV_RANK
Z_DQ = Z_KR + LANES
Z_DK = Z_DQ + DIFF_W

VMEM_LIMIT = 56 * 1024 * 1024
TOKEN_TILE = 512
TOKEN_SPLIT_IN = 2
TOKEN_SPLIT_OUT = 1
QUERY_TILE_BOUNDED = 1024
QUERY_TILE_SHIFTED = 512
OUT_BLOCKS_BOUNDED = 2
OUT_BLOCKS_SHIFTED = 1
NT_DIMS = (((1,), (1,)), ((), ()))
LOGIT_BOUND = 64.0
HEAD_V = 64
ONES_ROWS = 64
VT_ROWS = HEAD_V + ONES_ROWS


def _rms(x, g):
    ms = jnp.mean(x * x, axis=-1, keepdims=True)
    return x * lax.rsqrt(ms + EPS) * g


def _bdot(a, b):
    return jnp.dot(a.astype(BF16), b, preferred_element_type=F32)


def _swiglu_half(x, g, wg, wu, wd):
    h = _rms(x, g).astype(BF16)
    gate = jnp.dot(h, wg[...], preferred_element_type=F32)
    up = jnp.dot(h, wu[...], preferred_element_type=F32)
    act = (gate / (1.0 + jnp.exp(-gate))) * up
    return x + 0.5 * jnp.dot(act.astype(BF16), wd[...], preferred_element_type=F32)


def _lane_iota(rows):
    return lax.broadcasted_iota(jnp.int32, (rows, LANES), 1)


def _norm_rope_block(x, gsum, inv_count, gain, cos, sin, post_scale):
    lane = _lane_iota(x.shape[0])
    x2 = x * x
    hi = x2.astype(BF16)
    lo = (x2 - hi.astype(F32)).astype(BF16)
    ss = jnp.dot(jnp.concatenate([hi, lo], axis=1), gsum, preferred_element_type=F32)
    y = x * lax.rsqrt(ss * inv_count + EPS) * gain
    up = pltpu.roll(y, LANES - ROPE_DIM // 2, axis=1)
    dn = pltpu.roll(y, ROPE_DIM // 2, axis=1)
    partner = jnp.where((lane % ROPE_DIM) < ROPE_DIM // 2, up, dn)
    y = y * cos + partner * sin
    if post_scale != 1.0:
        y = y * post_scale
    return y


def _softmax_pv_t(qm, k, vt, bounded):
    st = lax.dot_general(k, qm, NT_DIMS, preferred_element_type=F32)
    if not bounded:
        st = st - jnp.max(st, axis=0, keepdims=True)
    ot = jnp.dot(vt, jnp.exp2(st).astype(BF16), preferred_element_type=F32)
    return ot[:HEAD_V] / ot[HEAD_V:HEAD_V + 1]


def _store_vt_with_ones(vt_ref, cols, vt):
    ones = jnp.ones((ONES_ROWS, vt.shape[1]), BF16)
    for hd in range(vt.shape[0] // HEAD_V):
        vt_ref[hd * VT_ROWS:hd * VT_ROWS + HEAD_V, cols] = vt[hd * HEAD_V:(hd + 1) * HEAD_V].astype(BF16)
        vt_ref[hd * VT_ROWS + HEAD_V:(hd + 1) * VT_ROWS, cols] = ones


def _store_fp8_pairs(ref, rows, head0, y, is_query):
    lane = _lane_iota(y.shape[0])
    first = lane < C_HEAD
    hi = y.astype(FP8).astype(F32)
    lo = (y - hi).astype(FP8).astype(F32)
    hi_sw, lo_sw = pltpu.roll(hi, C_HEAD, axis=1), pltpu.roll(lo, C_HEAD, axis=1)
    zero = jnp.zeros_like(hi)
    for e, (h_own, h_oth, l_own, l_oth) in enumerate(((hi, hi_sw, lo, lo_sw), (hi_sw, hi, lo_sw, lo))):
        if is_query:
            a, b = jnp.where(first, h_own, l_oth), jnp.where(first, h_own, zero)
        else:
            a, b = jnp.where(first, h_own, h_oth), jnp.where(first, l_own, zero)
        base = (head0 + e) * 2 * LANES
        ref[rows, base:base + LANES] = a.astype(FP8)
        ref[rows, base + LANES:base + 2 * LANES] = b.astype(FP8)


def _row_groups(n_rows, split):
    step = n_rows // split
    return [slice(r * step, (r + 1) * step) for r in range(split)]


def _even_in_kernel(x_ref, g1_ref, wg_ref, wu_ref, wd_ref, gm_ref, win_ref, wvt_ref,
                    gq_ref, wuq_ref, gkv_ref, wuk_ref, wuvt_ref, qn_ref, kn_ref, dqn_ref, dkn_ref,
                    gs96_ref, gs32_ref, cm_ref, sm_ref, cd_ref, sd_ref,
                    xo_ref, q_ref, k_ref, vt_ref, dq_ref, dk_ref, dvt_ref):
    q_scale = MLA_DIM ** -0.5 * LOG2E
    d_scale = DIFF_HEAD ** -0.5 * LOG2E
    xo_ref[...] = _swiglu_half(x_ref[...], g1_ref[...], wg_ref, wu_ref, wd_ref)
    for rows in _row_groups(x_ref.shape[0], TOKEN_SPLIT_IN):
        x = xo_ref[rows, :]
        h = _rms(x, gm_ref[...]).astype(BF16)
        z = jnp.dot(h, win_ref[...], preferred_element_type=F32)
        _store_vt_with_ones(dvt_ref, rows, lax.dot_general(wvt_ref[...], h, NT_DIMS,
                                                           preferred_element_type=F32))
        kr = z[:, Z_KR:Z_DQ]
        cm, sm, cd, sd = cm_ref[rows, :], sm_ref[rows, :], cd_ref[rows, :], sd_ref[rows, :]
        ckv = _rms(z[:, Z_CKV:Z_KR], gkv_ref[...]).astype(BF16)
        qf = _bdot(_rms(z[:, 0:Z_CKV], gq_ref[...]), wuq_ref[...])
        kf = jnp.dot(ckv, wuk_ref[...], preferred_element_type=F32)
        _store_vt_with_ones(vt_ref, rows, lax.dot_general(wuvt_ref[...], ckv, NT_DIMS,
                                                          preferred_element_type=F32))
        for hd in range(MLA_HEADS):
            sl = slice(hd * LANES, (hd + 1) * LANES)
            q_ref[rows, sl] = _norm_rope_block(qf[:, sl], gs96_ref[...], 1.0 / MLA_DIM, qn_ref[...],
                                               cm, sm, q_scale).astype(BF16)
            k_ref[rows, sl] = _norm_rope_block(kf[:, sl] + kr, gs96_ref[...], 1.0 / MLA_DIM, kn_ref[...],
                                               cm, sm, 1.0).astype(BF16)
        for b in range(DIFF_W // LANES):
            sl = slice(b * LANES, (b + 1) * LANES)
            dq_ref[rows, sl] = _norm_rope_block(z[:, Z_DQ + b * LANES:Z_DQ + (b + 1) * LANES],
                                                gs32_ref[...], 1.0 / DIFF_HEAD, dqn_ref[...],
                                                cd, sd, d_scale).astype(BF16)
            dk_ref[rows, sl] = _norm_rope_block(z[:, Z_DK + b * LANES:Z_DK + (b + 1) * LANES],
                                                gs32_ref[...], 1.0 / DIFF_HEAD, dkn_ref[...],
                                                cd, sd, 1.0).astype(BF16)


def _odd_in_kernel(x_ref, g1_ref, wg_ref, wu_ref, wd_ref, gm_ref, win_ref, wvt_ref,
                   qn_ref, kn_ref, gs64_ref, ca_ref, sa_ref,
                   xo_ref, q_ref, k_ref, vt_ref):
    q_scale = C_HEAD ** -0.5 * LOG2E
    xo_ref[...] = _swiglu_half(x_ref[...], g1_ref[...], wg_ref, wu_ref, wd_ref)
    for rows in _row_groups(x_ref.shape[0], TOKEN_SPLIT_IN):
        x = xo_ref[rows, :]
        h = _rms(x, gm_ref[...]).astype(BF16)
        z = jnp.dot(h, win_ref[...], preferred_element_type=F32)
        _store_vt_with_ones(vt_ref, rows, lax.dot_general(wvt_ref[...], h, NT_DIMS,
                                                          preferred_element_type=F32))
        ca, sa = ca_ref[rows, :], sa_ref[rows, :]
        for b in range(C_Q_W // LANES):
            y = _norm_rope_block(z[:, b * LANES:(b + 1) * LANES], gs64_ref[...], 1.0 / C_HEAD,
                                 qn_ref[...], ca, sa, q_scale)
            _store_fp8_pairs(q_ref, rows, 2 * b, y, is_query=True)
        for b in range(C_KV_W // LANES):
            y = _norm_rope_block(z[:, C_Q_W + b * LANES:C_Q_W + (b + 1) * LANES], gs64_ref[...],
                                 1.0 / C_HEAD, kn_ref[...], ca, sa, 1.0)
            _store_fp8_pairs(k_ref, rows, 2 * b, y, is_query=False)


def _even_out_kernel(x_ref, oa_ref, ob_ref, woa_ref, wob_ref, g2_ref, wg_ref, wu_ref, wd_ref, xo_ref):
    for rows in _row_groups(x_ref.shape[0], TOKEN_SPLIT_OUT):
        x = x_ref[rows, :]
        x = x + jnp.dot(oa_ref[rows, :], woa_ref[...], preferred_element_type=F32)
        x = x + jnp.dot(ob_ref[rows, :], wob_ref[...], preferred_element_type=F32)
        xo_ref[rows, :] = _swiglu_half(x, g2_ref[...], wg_ref, wu_ref, wd_ref)


def _odd_out_kernel(x_ref, o_ref, wo_ref, g2_ref, wg_ref, wu_ref, wd_ref, xo_ref):
    for rows in _row_groups(x_ref.shape[0], TOKEN_SPLIT_OUT):
        x = x_ref[rows, :] + jnp.dot(o_ref[rows, :], wo_ref[...], preferred_element_type=F32)
        xo_ref[rows, :] = _swiglu_half(x, g2_ref[...], wg_ref, wu_ref, wd_ref)


def _pair_rows(ot0, ot1):
    return jnp.concatenate([ot0, ot1], axis=0).T


def _out_blocks(o_ref):
    return range(o_ref.shape[1] // LANES)


def _mla_attn_kernel(bounded, q_ref, k_ref, vt_ref, o_ref):
    for s in _out_blocks(o_ref):
        outs = []
        for hd in (2 * s, 2 * s + 1):
            sl = slice(hd * LANES, (hd + 1) * LANES)
            outs.append(_softmax_pv_t(q_ref[:, sl], k_ref[:, sl],
                                      vt_ref[hd * VT_ROWS:(hd + 1) * VT_ROWS, :], bounded))
        o_ref[:, s * LANES:(s + 1) * LANES] = _pair_rows(*outs).astype(BF16)


def _gqa_attn_kernel(bounded, q_ref, k_ref, vt_ref, o_ref):
    w = 2 * LANES
    for s in _out_blocks(o_ref):
        outs = []
        for e in range(2):
            hd = 2 * s + e
            outs.append(_softmax_pv_t(q_ref[:, hd * w:(hd + 1) * w], k_ref[:, e * w:(e + 1) * w],
                                      vt_ref[e * VT_ROWS:(e + 1) * VT_ROWS, :], bounded))
        o_ref[:, s * LANES:(s + 1) * LANES] = _pair_rows(*outs).astype(BF16)


MXU_TILE = 256
PIPE_LAG = 2
S_ADDR = (0, 64, 128)
O_ADDR = (192, 224)
PV_ROWS = HEAD_V + 16


def _gqa_attn_mxu_kernel(q_ref, k_ref, vt_ref, o_ref):
    tq, seq = q_ref.shape[0], k_ref.shape[0]
    n_q, n_k = tq // MXU_TILE, seq // MXU_TILE
    units = {0: [], 1: []}
    for s in _out_blocks(o_ref):
        for e in range(2):
            for n in range(n_q):
                for j in range(n_k):
                    units[n % 2].append((s, e, n, j))
    for mxu in (0, 1):
        for a in S_ADDR:
            pltpu.matmul_pop(a, (MXU_TILE, MXU_TILE), F32, mxu)
        for a in O_ADDR:
            pltpu.matmul_pop(a, (PV_ROWS, MXU_TILE), F32, mxu)
    wq = {}
    for s in _out_blocks(o_ref):
        for e in range(2):
            for n in range(n_q):
                q = q_ref[n * MXU_TILE:(n + 1) * MXU_TILE,
                          (2 * s + e) * MXU_TILE:(2 * s + e + 1) * MXU_TILE]
                wq[(s, e, n)] = q.astype(F32).T.astype(FP8)
    results = {}
    n_units = len(units[0])
    o_lag = PIPE_LAG + 2
    for i in range(n_units + o_lag):
        for mxu in (0, 1):
            if i < n_units:
                s, e, n, j = units[mxu][i]
                pltpu.matmul_push_rhs(wq[(s, e, n)], staging_register=0, mxu_index=mxu)
                pltpu.matmul_acc_lhs(S_ADDR[i % 3],
                                     k_ref[j * MXU_TILE:(j + 1) * MXU_TILE, e * MXU_TILE:(e + 1) * MXU_TILE],
                                     mxu, load_staged_rhs=0)
            if PIPE_LAG <= i < n_units + PIPE_LAG:
                s, e, n, j = units[mxu][i - PIPE_LAG]
                st = pltpu.matmul_pop(S_ADDR[(i - PIPE_LAG) % 3], (MXU_TILE, MXU_TILE), F32, mxu)
                pltpu.matmul_push_rhs(jnp.exp2(st).astype(BF16), staging_register=1, mxu_index=mxu)
                o_addr = O_ADDR[((i - PIPE_LAG) // n_k) % 2]
                v_rows = vt_ref[e * VT_ROWS:e * VT_ROWS + PV_ROWS, j * MXU_TILE:(j + 1) * MXU_TILE]
                pltpu.matmul_acc_lhs(o_addr, v_rows, mxu, load_staged_rhs=1)
            if i >= o_lag:
                s, e, n, j = units[mxu][i - o_lag]
                if j == n_k - 1:
                    ot = pltpu.matmul_pop(O_ADDR[((i - o_lag) // n_k) % 2], (PV_ROWS, MXU_TILE), F32, mxu)
                    results[(s, e, n)] = ot[:HEAD_V] / ot[HEAD_V:HEAD_V + 1]
    for s in _out_blocks(o_ref):
        rows = [jnp.concatenate([results[(s, e, n)] for n in range(n_q)], axis=1) for e in range(2)]
        o_ref[:, s * LANES:(s + 1) * LANES] = _pair_rows(*rows).astype(BF16)


def _diff_attn_kernel(lam_init, bounded, q_ref, k_ref, vt_ref, lp_ref, sub_ref, o_ref):
    lp = lp_ref[...]
    lam = (jnp.exp(jnp.sum(lp[0:1] * lp[1:2], axis=-1, keepdims=True))
           - jnp.exp(jnp.sum(lp[2:3] * lp[3:4], axis=-1, keepdims=True)) + lam_init)
    lane = _lane_iota(q_ref.shape[0])
    lo = lane < 2 * DIFF_HEAD
    inv = 1.0 / (2 * DIFF_HEAD)
    for s in _out_blocks(o_ref):
        sl = slice(s * LANES, (s + 1) * LANES)
        q, k = q_ref[:, sl], k_ref[:, sl]
        zero = jnp.zeros_like(q)
        outs = []
        for hd in range(2):
            base = hd * 2 * DIFF_HEAD
            m1 = (lane >= base) & (lane < base + DIFF_HEAD)
            m2 = (lane >= base + DIFF_HEAD) & (lane < base + 2 * DIFF_HEAD)
            vt = vt_ref[(2 * s + hd) * VT_ROWS:(2 * s + hd + 1) * VT_ROWS, :]
            a1 = _softmax_pv_t(jnp.where(m1, q, zero), k, vt, bounded)
            a2 = _softmax_pv_t(jnp.where(m2, q, zero), k, vt, bounded)
            outs.append(a1 - lam * a2)
        o = _pair_rows(outs[0], outs[1])
        o2 = o * o
        r0 = lax.rsqrt(jnp.sum(jnp.where(lo, o2, 0.0), axis=-1, keepdims=True) * inv + EPS)
        r1 = lax.rsqrt(jnp.sum(jnp.where(lo, 0.0, o2), axis=-1, keepdims=True) * inv + EPS)
        o = o * jnp.where(lo, r0, r1) * sub_ref[...] * (1.0 - lam_init)
        o_ref[:, sl] = o.astype(BF16)


def _tile(n, want):
    return want if n % want == 0 else n


def _const_spec(shape):
    nd = len(shape)
    return pl.BlockSpec(shape, lambda *_: (0,) * nd, pipeline_mode=pl.Buffered(1))


def _tok_spec(tm, width):
    return pl.BlockSpec((None, tm, width), lambda b, t: (b, t, 0))


def _tok_t_spec(tm, width):
    return pl.BlockSpec((None, width, tm), lambda b, t: (b, 0, t))


def _tab_spec(tm):
    return pl.BlockSpec((tm, LANES), lambda b, t: (t, 0))


def _token_call(body, name, x, consts, tabs, extra_tok, outs, tm):
    bsz, seq, _ = x.shape
    tok_inputs = [x] + list(extra_tok)
    in_specs = ([_tok_spec(tm, a.shape[-1]) for a in tok_inputs]
                + [_const_spec(c.shape) for c in consts]
                + [_tab_spec(tm) for _ in tabs])
    out_shape = [jax.ShapeDtypeStruct((bsz, w, seq) if tr else (bsz, seq, w), d) for w, d, tr in outs]
    out_specs = [_tok_t_spec(tm, w) if tr else _tok_spec(tm, w) for w, d, tr in outs]
    return pl.pallas_call(
        body, name=name,
        grid=(bsz, seq // tm),
        in_specs=in_specs, out_specs=out_specs, out_shape=out_shape,
        compiler_params=pltpu.CompilerParams(
            dimension_semantics=("arbitrary", "arbitrary"), vmem_limit_bytes=VMEM_LIMIT),
    )(*tok_inputs, *consts, *tabs)


def _attn_dispatch(body, name, logit_bound, *args):
    q, k, vt, extra, *static, mxu_body = args

    def run(bounded, q, k, vt, extra):
        tag = "bounded" if bounded else "shifted"
        tq = _tile(q.shape[1], QUERY_TILE_BOUNDED if bounded else QUERY_TILE_SHIFTED)
        n_sub = OUT_BLOCKS_BOUNDED if bounded else OUT_BLOCKS_SHIFTED
        if bounded and mxu_body is not None and tq % (2 * MXU_TILE) == 0 and q.shape[1] % MXU_TILE == 0:
            return _attn_call(mxu_body, f"{name}_{tag}", q, k, vt, extra, *static, tq, n_sub)
        return _attn_call(functools.partial(body, bounded), f"{name}_{tag}", q, k, vt, extra,
                          *static, tq, n_sub)

    return lax.cond(logit_bound <= LOGIT_BOUND, functools.partial(run, True),
                    functools.partial(run, False), q, k, vt, tuple(extra))


def _attn_call(body, name, q, k, vt, extra, q_blk, k_blk, n_blocks, kv_share, tq, n_sub):
    bsz, seq, _ = q.shape
    if kv_share == 1:
        k_spec = pl.BlockSpec((None, seq, k_blk * n_sub), lambda b, h, t: (b, 0, h))
        vt_spec = pl.BlockSpec((None, 2 * VT_ROWS * n_sub, seq), lambda b, h, t: (b, h, 0))
    else:
        assert kv_share % n_sub == 0
        k_spec = pl.BlockSpec((None, seq, k_blk), lambda b, h, t: (b, 0, (h * n_sub) // kv_share))
        vt_spec = pl.BlockSpec((None, 2 * VT_ROWS, seq), lambda b, h, t: (b, (h * n_sub) // kv_share, 0))
    in_specs = [pl.BlockSpec((None, tq, q_blk * n_sub), lambda b, h, t: (b, t, h)), k_spec, vt_spec
                ] + [pl.BlockSpec(e.shape, lambda b, h, t: (0, 0)) for e in extra]
    return pl.pallas_call(
        body, name=name,
        grid=(bsz, n_blocks // n_sub, seq // tq),
        in_specs=in_specs,
        out_specs=pl.BlockSpec((None, tq, LANES * n_sub), lambda b, h, t: (b, t, h)),
        out_shape=jax.ShapeDtypeStruct((bsz, seq, n_blocks * LANES), BF16),
        compiler_params=pltpu.CompilerParams(
            dimension_semantics=("arbitrary", "arbitrary", "arbitrary"), vmem_limit_bytes=VMEM_LIMIT),
    )(q, k, vt, *extra)


def _rope_lane_tables(pos_a, pos_b):
    half = ROPE_DIM // 2
    inv = ROPE_THETA ** (-jnp.arange(0, ROPE_DIM, 2, dtype=F32) / ROPE_DIM)
    lane = np.arange(LANES)
    sign = jnp.asarray(np.where(lane % ROPE_DIM < half, -1.0, 1.0), F32)
    use_b = jnp.asarray((lane // ROPE_DIM) % 2 == 1)
    freq = inv[lane % half]
    ang_a = pos_a.astype(F32)[:, None] * freq[None, :]
    ang_b = pos_b.astype(F32)[:, None] * freq[None, :]
    cos = jnp.where(use_b[None, :], jnp.cos(ang_b), jnp.cos(ang_a))
    sin = jnp.where(use_b[None, :], jnp.sin(ang_b), jnp.sin(ang_a)) * sign[None, :]
    return cos, sin


def _group_sum_matrix(group):
    lane = np.arange(LANES)
    g = (lane[:, None] // group == lane[None, :] // group).astype(np.float32)
    return jnp.asarray(np.concatenate([g, g], axis=0), BF16)


def _row(v):
    return v.reshape(1, -1).astype(F32)


def _max_abs(v):
    return jnp.max(jnp.abs(v.astype(F32)))


def kernel(x, ffn1_norm, ffn1_w_gate, ffn1_w_up, ffn1_w_down, mix_norm, ffn2_norm, ffn2_w_gate, ffn2_w_up, ffn2_w_down, ab_w_in, mla_q_lora_norm, mla_w_uq, mla_kv_lora_norm, mla_w_ukv, mla_q_norm, mla_k_norm, diff_q_norm, diff_k_norm, diff_lambda_q1, diff_lambda_k1, diff_lambda_q2, diff_lambda_k2, diff_subln, ab_w_out, c_w_in, c_q_norm, c_k_norm, c_w_out):
    bsz, seq, d = x.shape
    depth = ffn1_norm.shape[0]
    tm = _tile(seq, TOKEN_TILE)

    pos = jnp.arange(seq, dtype=jnp.int32)
    cos_t, sin_t = _rope_lane_tables(pos, pos)
    cos_ax, sin_ax = _rope_lane_tables(pos // GRID_W, pos % GRID_W)
    lane = np.arange(LANES)
    mla_rot = jnp.asarray((lane >= MLA_NOPE) & (lane < MLA_NOPE + MLA_ROPE))
    cos_m = jnp.where(mla_rot[None, :], cos_t, 1.0)
    sin_m = jnp.where(mla_rot[None, :], sin_t, 0.0)
    gs_all, gs64, gs32 = _group_sum_matrix(LANES), _group_sum_matrix(C_HEAD), _group_sum_matrix(DIFF_HEAD)

    for i in range(depth):
        j = i // 2
        ffn1 = (_row(ffn1_norm[i]), ffn1_w_gate[i].astype(BF16), ffn1_w_up[i].astype(BF16),
                ffn1_w_down[i].astype(BF16))
        ffn2 = (_row(ffn2_norm[i]), ffn2_w_gate[i].astype(BF16), ffn2_w_up[i].astype(BF16),
                ffn2_w_down[i].astype(BF16))
        if i % 2 == 0:
            lam_init = 0.8 - 0.6 * math.exp(-0.3 * i)
            w = ab_w_in[j].astype(BF16)
            o_kr = MLA_Q_RANK + MLA_KV_RANK
            o_dq = o_kr + MLA_ROPE
            o_dv = o_dq + 2 * DIFF_W
            kr_blk = jnp.pad(w[:, o_kr:o_dq], ((0, 0), (MLA_NOPE, LANES - MLA_DIM)))
            w_in = jnp.concatenate([w[:, 0:o_kr], kr_blk, w[:, o_dq:o_dv]], axis=1)
            w_vt = w[:, o_dv:].T
            w_uq = jnp.pad(mla_w_uq[j].reshape(MLA_Q_RANK, MLA_HEADS, MLA_DIM),
                           ((0, 0), (0, 0), (0, LANES - MLA_DIM))
                           ).reshape(MLA_Q_RANK, MLA_HEADS * LANES).astype(BF16)
            ukv = mla_w_ukv[j].reshape(MLA_KV_RANK, MLA_HEADS, MLA_NOPE + MLA_V)
            w_uk = jnp.pad(ukv[:, :, :MLA_NOPE], ((0, 0), (0, 0), (0, LANES - MLA_NOPE))
                           ).reshape(MLA_KV_RANK, MLA_HEADS * LANES).astype(BF16)
            w_uvt = ukv[:, :, MLA_NOPE:].reshape(MLA_KV_RANK, MLA_HEADS * MLA_V).T.astype(BF16)
            pad96 = (0, LANES - MLA_DIM)
            consts = ffn1 + (_row(mix_norm[i]), w_in, w_vt,
                             _row(mla_q_lora_norm[j]), w_uq, _row(mla_kv_lora_norm[j]), w_uk, w_uvt,
                             _row(jnp.pad(mla_q_norm[j], pad96)), _row(jnp.pad(mla_k_norm[j], pad96)),
                             _row(jnp.tile(diff_q_norm[j], LANES // DIFF_HEAD)),
                             _row(jnp.tile(diff_k_norm[j], LANES // DIFF_HEAD)),
                             gs_all, gs32)
            x, q, k, vt, dq, dk, dvt = _token_call(
                _even_in_kernel, f"even_in_{i}", x, consts, (cos_m, sin_m, cos_t, sin_t), (),
                ((d, F32, False), (MLA_HEADS * LANES, BF16, False), (MLA_HEADS * LANES, BF16, False),
                 (VT_ROWS * MLA_HEADS, BF16, True), (DIFF_W, BF16, False), (DIFF_W, BF16, False),
                 (VT_ROWS * DIFF_HEADS, BF16, True)), tm)
            mla_bound = (MLA_DIM ** 0.5 * LOG2E) * _max_abs(mla_q_norm[j]) * _max_abs(mla_k_norm[j])
            o_mla = _attn_dispatch(_mla_attn_kernel, f"mla_attn_{i}", mla_bound, q, k, vt, (),
                                   2 * LANES, 2 * LANES, MLA_HEADS // 2, 1, None)
            lam_p = jnp.stack([diff_lambda_q1[j], diff_lambda_k1[j],
                               diff_lambda_q2[j], diff_lambda_k2[j]]).astype(F32)
            sub = _row(jnp.tile(diff_subln[j], 2))
            diff_bound = (DIFF_HEAD ** 0.5 * LOG2E) * _max_abs(diff_q_norm[j]) * _max_abs(diff_k_norm[j])
            o_diff = _attn_dispatch(functools.partial(_diff_attn_kernel, lam_init), f"diff_attn_{i}",
                                    diff_bound, dq, dk, dvt, (lam_p, sub),
                                    LANES, LANES, DIFF_HEADS // 2, 1, None)
            wo = ab_w_out[j].astype(BF16)
            consts = (wo[:MLA_HEADS * MLA_V], wo[MLA_HEADS * MLA_V:]) + ffn2
            (x,) = _token_call(_even_out_kernel, f"even_out_{i}", x, consts, (), (o_mla, o_diff),
                               ((d, F32, False),), tm)
        else:
            w = c_w_in[j].astype(BF16)
            wq = w[:, :C_Q_W].reshape(d, 2, 2, C_GROUPS, C_HEAD)
            wq = wq.transpose(0, 1, 3, 2, 4).reshape(d, C_Q_W)
            w_in = jnp.concatenate([wq, w[:, C_Q_W:C_Q_W + C_KV_W]], axis=1)
            w_vt = w[:, C_Q_W + C_KV_W:].T
            consts = ffn1 + (_row(mix_norm[i]), w_in, w_vt,
                             _row(jnp.tile(c_q_norm[j], 2)), _row(jnp.tile(c_k_norm[j], 2)), gs64)
            x, q, k, vt = _token_call(
                _odd_in_kernel, f"odd_in_{i}", x, consts, (cos_ax, sin_ax), (),
                ((d, F32, False), (C_HEADS * 2 * LANES, FP8, False), (C_KV_HEADS * 2 * LANES, FP8, False),
                 (VT_ROWS * C_KV_HEADS, BF16, True)), tm)
            gqa_bound = (C_HEAD ** 0.5 * LOG2E) * _max_abs(c_q_norm[j]) * _max_abs(c_k_norm[j])
            o = _attn_dispatch(_gqa_attn_kernel, f"gqa_attn_{i}", gqa_bound, q, k, vt, (),
                               4 * LANES, 4 * LANES, C_HEADS // 2, C_GROUPS, _gqa_attn_mxu_kernel)
            wo = c_w_out[j].astype(BF16).reshape(2, 2, C_GROUPS, C_HEAD, d).transpose(0, 2, 1, 3, 4)
            wo = wo.reshape(C_Q_W, d)
            consts = (wo,) + ffn2
            (x,) = _token_call(_odd_out_kernel, f"odd_out_{i}", x, consts, (), (o,),
                               ((d, F32, False),), tm)
    return x
```

```python
import functools
import math

import jax
import jax.numpy as jnp
import numpy as np
from jax import lax
from jax.experimental import pallas as pl
from jax.experimental.pallas import tpu as pltpu

F32 = jnp.float32
BF16 = jnp.bfloat16
FP8 = jnp.float8_e4m3fn

LANES = 128
EPS = 1e-6
ROPE_THETA = 10000.0
GRID_W = 64
LOG2E = math.log2(math.e)

MLA_HEADS, MLA_Q_RANK, MLA_KV_RANK = 8, 256, 128
MLA_NOPE, MLA_ROPE, MLA_V = 64, 32, 64
MLA_DIM = MLA_NOPE + MLA_ROPE
DIFF_HEADS, DIFF_HEAD = 8, 32
DIFF_W = DIFF_HEADS * 2 * DIFF_HEAD
C_HEADS, C_KV_HEADS, C_HEAD = 16, 4, 64
C_GROUPS = C_HEADS // C_KV_HEADS
C_Q_W, C_KV_W = C_HEADS * C_HEAD, C_KV_HEADS * C_HEAD
ROPE_DIM = 32

Z_CKV = MLA_Q_RANK
Z_KR = Z_CKV + MLA_KV_RANK
Z_DQ = Z_KR + LANES
Z_DK = Z_DQ + DIFF_W

VMEM_LIMIT = 56 * 1024 * 1024
TOKEN_TILE = 512
TOKEN_SPLIT_IN = 2
TOKEN_SPLIT_OUT = 1
QUERY_TILE_BOUNDED = 1024
QUERY_TILE_SHIFTED = 512
OUT_BLOCKS_BOUNDED = 2
OUT_BLOCKS_SHIFTED = 1
OUT_BLOCKS_MXU = 4
QUERY_TILE_MXU = 1024
NT_DIMS = (((1,), (1,)), ((), ()))
LOGIT_BOUND = 64.0
HEAD_V = 64
ONES_ROWS = 64
VT_ROWS = HEAD_V + ONES_ROWS


def _rms(x, g):
    ms = jnp.mean(x * x, axis=-1, keepdims=True)
    return x * lax.rsqrt(ms + EPS) * g


def _bdot(a, b):
    return jnp.dot(a.astype(BF16), b, preferred_element_type=F32)


def _swiglu_half(x, g, wg, wu, wd):
    h = _rms(x, g).astype(BF16)
    gate = jnp.dot(h, wg[...], preferred_element_type=F32)
    up = jnp.dot(h, wu[...], preferred_element_type=F32)
    act = (gate / (1.0 + jnp.exp(-gate))) * up
    return x + 0.5 * jnp.dot(act.astype(BF16), wd[...], preferred_element_type=F32)


def _lane_iota(rows):
    return lax.broadcasted_iota(jnp.int32, (rows, LANES), 1)


def _norm_rope_block(x, gsum, inv_count, gain, cos, sin, post_scale):
    lane = _lane_iota(x.shape[0])
    x2 = x * x
    hi = x2.astype(BF16)
    lo = (x2 - hi.astype(F32)).astype(BF16)
    ss = jnp.dot(jnp.concatenate([hi, lo], axis=1), gsum, preferred_element_type=F32)
    y = x * lax.rsqrt(ss * inv_count + EPS) * gain
    up = pltpu.roll(y, LANES - ROPE_DIM // 2, axis=1)
    dn = pltpu.roll(y, ROPE_DIM // 2, axis=1)
    partner = jnp.where((lane % ROPE_DIM) < ROPE_DIM // 2, up, dn)
    y = y * cos + partner * sin
    if post_scale != 1.0:
        y = y * post_scale
    return y


def _softmax_pv_t(qm, k, vt, bounded):
    st = lax.dot_general(k, qm, NT_DIMS, preferred_element_type=F32)
    if not bounded:
        st = st - jnp.max(st, axis=0, keepdims=True)
    ot = jnp.dot(vt, jnp.exp2(st).astype(BF16), preferred_element_type=F32)
    return ot[:HEAD_V] / ot[HEAD_V:HEAD_V + 1]


def _store_vt_with_ones(vt_ref, cols, vt):
    ones = jnp.ones((ONES_ROWS, vt.shape[1]), BF16)
    for hd in range(vt.shape[0] // HEAD_V):
        vt_ref[hd * VT_ROWS:hd * VT_ROWS + HEAD_V, cols] = vt[hd * HEAD_V:(hd + 1) * HEAD_V].astype(BF16)
        vt_ref[hd * VT_ROWS + HEAD_V:(hd + 1) * VT_ROWS, cols] = ones


def _store_fp8_pairs(ref, rows, head0, y, is_query):
    lane = _lane_iota(y.shape[0])
    first = lane < C_HEAD
    hi = y.astype(FP8).astype(F32)
    lo = (y - hi).astype(FP8).astype(F32)
    hi_sw, lo_sw = pltpu.roll(hi, C_HEAD, axis=1), pltpu.roll(lo, C_HEAD, axis=1)
    zero = jnp.zeros_like(hi)
    for e, (h_own, h_oth, l_own, l_oth) in enumerate(((hi, hi_sw, lo, lo_sw), (hi_sw, hi, lo_sw, lo))):
        if is_query:
            a, b = jnp.where(first, h_own, l_oth), jnp.where(first, h_own, zero)
        else:
            a, b = jnp.where(first, h_own, h_oth), jnp.where(first, l_own, zero)
        base = (head0 + e) * 2 * LANES
        ref[rows, base:base + LANES] = a.astype(FP8)
        ref[rows, base + LANES:base + 2 * LANES] = b.astype(FP8)


def _row_groups(n_rows, split):
    step = n_rows // split
    return [slice(r * step, (r + 1) * step) for r in range(split)]


def _even_in_kernel(x_ref, g1_ref, wg_ref, wu_ref, wd_ref, gm_ref, win_ref, wvt_ref,
                    gq_ref, wuq_ref, gkv_ref, wuk_ref, wuvt_ref, qn_ref, kn_ref, dqn_ref, dkn_ref,
                    gs96_ref, gs32_ref, cm_ref, sm_ref, cd_ref, sd_ref,
                    xo_ref, q_ref, k_ref, vt_ref, dq_ref, dk_ref, dvt_ref):
    q_scale = MLA_DIM ** -0.5 * LOG2E
    d_scale = DIFF_HEAD ** -0.5 * LOG2E
    xo_ref[...] = _swiglu_half(x_ref[...], g1_ref[...], wg_ref, wu_ref, wd_ref)
    for rows in _row_groups(x_ref.shape[0], TOKEN_SPLIT_IN):
        x = xo_ref[rows, :]
        h = _rms(x, gm_ref[...]).astype(BF16)
        z = jnp.dot(h, win_ref[...], preferred_element_type=F32)
        _store_vt_with_ones(dvt_ref, rows, lax.dot_general(wvt_ref[...], h, NT_DIMS,
                                                           preferred_element_type=F32))
        kr = z[:, Z_KR:Z_DQ]
        cm, sm, cd, sd = cm_ref[rows, :], sm_ref[rows, :], cd_ref[rows, :], sd_ref[rows, :]
        ckv = _rms(z[:, Z_CKV:Z_KR], gkv_ref[...]).astype(BF16)
        qf = _bdot(_rms(z[:, 0:Z_CKV], gq_ref[...]), wuq_ref[...])
        kf = jnp.dot(ckv, wuk_ref[...], preferred_element_type=F32)
        _store_vt_with_ones(vt_ref, rows, lax.dot_general(wuvt_ref[...], ckv, NT_DIMS,
                                                          preferred_element_type=F32))
        for hd in range(MLA_HEADS):
            sl = slice(hd * LANES, (hd + 1) * LANES)
            q_ref[rows, sl] = _norm_rope_block(qf[:, sl], gs96_ref[...], 1.0 / MLA_DIM, qn_ref[...],
                                               cm, sm, q_scale).astype(BF16)
            k_ref[rows, sl] = _norm_rope_block(kf[:, sl] + kr, gs96_ref[...], 1.0 / MLA_DIM, kn_ref[...],
                                               cm, sm, 1.0).astype(BF16)
        for b in range(DIFF_W // LANES):
            sl = slice(b * LANES, (b + 1) * LANES)
            dq_ref[rows, sl] = _norm_rope_block(z[:, Z_DQ + b * LANES:Z_DQ + (b + 1) * LANES],
                                                gs32_ref[...], 1.0 / DIFF_HEAD, dqn_ref[...],
                                                cd, sd, d_scale).astype(BF16)
            dk_ref[rows, sl] = _norm_rope_block(z[:, Z_DK + b * LANES:Z_DK + (b + 1) * LANES],
                                                gs32_ref[...], 1.0 / DIFF_HEAD, dkn_ref[...],
                                                cd, sd, 1.0).astype(BF16)


def _odd_in_kernel(x_ref, g1_ref, wg_ref, wu_ref, wd_ref, gm_ref, win_ref, wvt_ref,
                   qn_ref, kn_ref, gs64_ref, ca_ref, sa_ref,
                   xo_ref, q_ref, k_ref, vt_ref):
    q_scale = C_HEAD ** -0.5 * LOG2E
    xo_ref[...] = _swiglu_half(x_ref[...], g1_ref[...], wg_ref, wu_ref, wd_ref)
    for rows in _row_groups(x_ref.shape[0], TOKEN_SPLIT_IN):
        x = xo_ref[rows, :]
        h = _rms(x, gm_ref[...]).astype(BF16)
        z = jnp.dot(h, win_ref[...], preferred_element_type=F32)
        _store_vt_with_ones(vt_ref, rows, lax.dot_general(wvt_ref[...], h, NT_DIMS,
                                                          preferred_element_type=F32))
        ca, sa = ca_ref[rows, :], sa_ref[rows, :]
        for b in range(C_Q_W // LANES):
            y = _norm_rope_block(z[:, b * LANES:(b + 1) * LANES], gs64_ref[...], 1.0 / C_HEAD,
                                 qn_ref[...], ca, sa, q_scale)
            _store_fp8_pairs(q_ref, rows, 2 * b, y, is_query=True)
        for b in range(C_KV_W // LANES):
            y = _norm_rope_block(z[:, C_Q_W + b * LANES:C_Q_W + (b + 1) * LANES], gs64_ref[...],
                                 1.0 / C_HEAD, kn_ref[...], ca, sa, 1.0)
            _store_fp8_pairs(k_ref, rows, 2 * b, y, is_query=False)


def _even_out_kernel(x_ref, oa_ref, ob_ref, woa_ref, wob_ref, g2_ref, wg_ref, wu_ref, wd_ref, xo_ref):
    for rows in _row_groups(x_ref.shape[0], TOKEN_SPLIT_OUT):
        x = x_ref[rows, :]
        x = x + jnp.dot(oa_ref[rows, :], woa_ref[...], preferred_element_type=F32)
        x = x + jnp.dot(ob_ref[rows, :], wob_ref[...], preferred_element_type=F32)
        xo_ref[rows, :] = _swiglu_half(x, g2_ref[...], wg_ref, wu_ref, wd_ref)


def _odd_out_kernel(x_ref, o_ref, wo_ref, g2_ref, wg_ref, wu_ref, wd_ref, xo_ref):
    for rows in _row_groups(x_ref.shape[0], TOKEN_SPLIT_OUT):
        x = x_ref[rows, :] + jnp.dot(o_ref[rows, :], wo_ref[...], preferred_element_type=F32)
        xo_ref[rows, :] = _swiglu_half(x, g2_ref[...], wg_ref, wu_ref, wd_ref)


def _pair_rows(ot0, ot1):
    return jnp.concatenate([ot0, ot1], axis=0).T


def _out_blocks(o_ref):
    return range(o_ref.shape[1] // LANES)


def _mla_attn_kernel(bounded, q_ref, k_ref, vt_ref, o_ref):
    for s in _out_blocks(o_ref):
        outs = []
        for hd in (2 * s, 2 * s + 1):
            sl = slice(hd * LANES, (hd + 1) * LANES)
            outs.append(_softmax_pv_t(q_ref[:, sl], k_ref[:, sl],
                                      vt_ref[hd * VT_ROWS:(hd + 1) * VT_ROWS, :], bounded))
        o_ref[:, s * LANES:(s + 1) * LANES] = _pair_rows(*outs).astype(BF16)


def _gqa_attn_kernel(bounded, q_ref, k_ref, vt_ref, o_ref):
    w = 2 * LANES
    for s in _out_blocks(o_ref):
        outs = []
        for e in range(2):
            hd = 2 * s + e
            outs.append(_softmax_pv_t(q_ref[:, hd * w:(hd + 1) * w], k_ref[:, e * w:(e + 1) * w],
                                      vt_ref[e * VT_ROWS:(e + 1) * VT_ROWS, :], bounded))
        o_ref[:, s * LANES:(s + 1) * LANES] = _pair_rows(*outs).astype(BF16)


MXU_TILE = 256
PIPE_LAG = 2
S_ADDR = (0, 64, 128)
O_ADDR = (192, 224)
PV_ROWS = HEAD_V + 16


def _gqa_attn_mxu_kernel(q_ref, k_ref, vt_ref, o_ref):
    tq, seq = q_ref.shape[0], k_ref.shape[0]
    n_q, n_k = tq // MXU_TILE, seq // MXU_TILE
    units = {0: [], 1: []}
    for s in _out_blocks(o_ref):
        for e in range(2):
            for n in range(n_q):
                for j in range(n_k):
                    units[n % 2].append((s, e, n, j))
    for mxu in (0, 1):
        for a in S_ADDR:
            pltpu.matmul_pop(a, (MXU_TILE, MXU_TILE), F32, mxu)
        for a in O_ADDR:
            pltpu.matmul_pop(a, (PV_ROWS, MXU_TILE), F32, mxu)
    wq = {}
    for s in _out_blocks(o_ref):
        for e in range(2):
            for n in range(n_q):
                q = q_ref[n * MXU_TILE:(n + 1) * MXU_TILE,
                          (2 * s + e) * MXU_TILE:(2 * s + e + 1) * MXU_TILE]
                wq[(s, e, n)] = q.astype(F32).T.astype(FP8)
    results = {}
    n_units = len(units[0])
    o_lag = PIPE_LAG + 2
    for i in range(n_units + o_lag):
        for mxu in (0, 1):
            if i < n_units:
                s, e, n, j = units[mxu][i]
                pltpu.matmul_push_rhs(wq[(s, e, n)], staging_register=0, mxu_index=mxu)
                pltpu.matmul_acc_lhs(S_ADDR[i % 3],
                                     k_ref[j * MXU_TILE:(j + 1) * MXU_TILE, e * MXU_TILE:(e + 1) * MXU_TILE],
                                     mxu, load_staged_rhs=0)
            if PIPE_LAG <= i < n_units + PIPE_LAG:
                s, e, n, j = units[mxu][i - PIPE_LAG]
                st = pltpu.matmul_pop(S_ADDR[(i - PIPE_LAG) % 3], (MXU_TILE, MXU_TILE), F32, mxu)
                pltpu.matmul_push_rhs(jnp.exp2(st).astype(BF16), staging_register=1, mxu_index=mxu)
                o_addr = O_ADDR[((i - PIPE_LAG) // n_k) % 2]
                v_rows = vt_ref[e * VT_ROWS:e * VT_ROWS + PV_ROWS, j * MXU_TILE:(j + 1) * MXU_TILE]
                pltpu.matmul_acc_lhs(o_addr, v_rows, mxu, load_staged_rhs=1)
            if i >= o_lag:
                s, e, n, j = units[mxu][i - o_lag]
                if j == n_k - 1:
                    ot = pltpu.matmul_pop(O_ADDR[((i - o_lag) // n_k) % 2], (PV_ROWS, MXU_TILE), F32, mxu)
                    results[(s, e, n)] = ot[:HEAD_V] / ot[HEAD_V:HEAD_V + 1]
    for s in _out_blocks(o_ref):
        rows = [jnp.concatenate([results[(s, e, n)] for n in range(n_q)], axis=1) for e in range(2)]
        o_ref[:, s * LANES:(s + 1) * LANES] = _pair_rows(*rows).astype(BF16)


def _diff_attn_kernel(lam_init, bounded, q_ref, k_ref, vt_ref, lp_ref, sub_ref, o_ref):
    lp = lp_ref[...]
    lam = (jnp.exp(jnp.sum(lp[0:1] * lp[1:2], axis=-1, keepdims=True))
           - jnp.exp(jnp.sum(lp[2:3] * lp[3:4], axis=-1, keepdims=True)) + lam_init)
    lane = _lane_iota(q_ref.shape[0])
    lo = lane < 2 * DIFF_HEAD
    inv = 1.0 / (2 * DIFF_HEAD)
    for s in _out_blocks(o_ref):
        sl = slice(s * LANES, (s + 1) * LANES)
        q, k = q_ref[:, sl], k_ref[:, sl]
        zero = jnp.zeros_like(q)
        outs = []
        for hd in range(2):
            base = hd * 2 * DIFF_HEAD
            m1 = (lane >= base) & (lane < base + DIFF_HEAD)
            m2 = (lane >= base + DIFF_HEAD) & (lane < base + 2 * DIFF_HEAD)
            vt = vt_ref[(2 * s + hd) * VT_ROWS:(2 * s + hd + 1) * VT_ROWS, :]
            a1 = _softmax_pv_t(jnp.where(m1, q, zero), k, vt, bounded)
            a2 = _softmax_pv_t(jnp.where(m2, q, zero), k, vt, bounded)
            outs.append(a1 - lam * a2)
        o = _pair_rows(outs[0], outs[1])
        o2 = o * o
        r0 = lax.rsqrt(jnp.sum(jnp.where(lo, o2, 0.0), axis=-1, keepdims=True) * inv + EPS)
        r1 = lax.rsqrt(jnp.sum(jnp.where(lo, 0.0, o2), axis=-1, keepdims=True) * inv + EPS)
        o = o * jnp.where(lo, r0, r1) * sub_ref[...] * (1.0 - lam_init)
        o_ref[:, sl] = o.astype(BF16)


def _tile(n, want):
    return want if n % want == 0 else n


def _const_spec(shape):
    nd = len(shape)
    return pl.BlockSpec(shape, lambda *_: (0,) * nd, pipeline_mode=pl.Buffered(1))


def _tok_spec(tm, width):
    return pl.BlockSpec((None, tm, width), lambda b, t: (b, t, 0))


def _tok_t_spec(tm, width):
    return pl.BlockSpec((None, width, tm), lambda b, t: (b, 0, t))


def _tab_spec(tm):
    return pl.BlockSpec((tm, LANES), lambda b, t: (t, 0))


def _token_call(body, name, x, consts, tabs, extra_tok, outs, tm):
    bsz, seq, _ = x.shape
    tok_inputs = [x] + list(extra_tok)
    in_specs = ([_tok_spec(tm, a.shape[-1]) for a in tok_inputs]
                + [_const_spec(c.shape) for c in consts]
                + [_tab_spec(tm) for _ in tabs])
    out_shape = [jax.ShapeDtypeStruct((bsz, w, seq) if tr else (bsz, seq, w), d) for w, d, tr in outs]
    out_specs = [_tok_t_spec(tm, w) if tr else _tok_spec(tm, w) for w, d, tr in outs]
    return pl.pallas_call(
        body, name=name,
        grid=(bsz, seq // tm),
        in_specs=in_specs, out_specs=out_specs, out_shape=out_shape,
        compiler_params=pltpu.CompilerParams(
            dimension_semantics=("arbitrary", "arbitrary"), vmem_limit_bytes=VMEM_LIMIT),
    )(*tok_inputs, *consts, *tabs)


def _attn_dispatch(body, name, logit_bound, *args):
    q, k, vt, extra, *static, mxu_body = args

    def run(bounded, q, k, vt, extra):
        tag = "bounded" if bounded else "shifted"
        tq = _tile(q.shape[1], QUERY_TILE_BOUNDED if bounded else QUERY_TILE_SHIFTED)
        n_sub = OUT_BLOCKS_BOUNDED if bounded else OUT_BLOCKS_SHIFTED
        if bounded and mxu_body is not None and q.shape[1] % QUERY_TILE_MXU == 0:
            return _attn_call(mxu_body, f"{name}_{tag}", q, k, vt, extra, *static,
                              QUERY_TILE_MXU, OUT_BLOCKS_MXU)
        return _attn_call(functools.partial(body, bounded), f"{name}_{tag}", q, k, vt, extra,
                          *static, tq, n_sub)

    return lax.cond(logit_bound <= LOGIT_BOUND, functools.partial(run, True),
                    functools.partial(run, False), q, k, vt, tuple(extra))


def _attn_call(body, name, q, k, vt, extra, q_blk, k_blk, n_blocks, kv_share, tq, n_sub):
    bsz, seq, _ = q.shape
    if kv_share == 1:
        k_spec = pl.BlockSpec((None, seq, k_blk * n_sub), lambda b, h, t: (b, 0, h))
        vt_spec = pl.BlockSpec((None, 2 * VT_ROWS * n_sub, seq), lambda b, h, t: (b, h, 0))
    else:
        assert kv_share % n_sub == 0
        k_spec = pl.BlockSpec((None, seq, k_blk), lambda b, h, t: (b, 0, (h * n_sub) // kv_share))
        vt_spec = pl.BlockSpec((None, 2 * VT_ROWS, seq), lambda b, h, t: (b, (h * n_sub) // kv_share, 0))
    in_specs = [pl.BlockSpec((None, tq, q_blk * n_sub), lambda b, h, t: (b, t, h)), k_spec, vt_spec
                ] + [pl.BlockSpec(e.shape, lambda b, h, t: (0, 0)) for e in extra]
    return pl.pallas_call(
        body, name=name,
        grid=(bsz, n_blocks // n_sub, seq // tq),
        in_specs=in_specs,
        out_specs=pl.BlockSpec((None, tq, LANES * n_sub), lambda b, h, t: (b, t, h)),
        out_shape=jax.ShapeDtypeStruct((bsz, seq, n_blocks * LANES), BF16),
        compiler_params=pltpu.CompilerParams(
            dimension_semantics=("arbitrary", "arbitrary", "arbitrary"), vmem_limit_bytes=VMEM_LIMIT),
    )(q, k, vt, *extra)


def _rope_lane_tables(pos_a, pos_b):
    half = ROPE_DIM // 2
    inv = ROPE_THETA ** (-jnp.arange(0, ROPE_DIM, 2, dtype=F32) / ROPE_DIM)
    lane = np.arange(LANES)
    sign = jnp.asarray(np.where(lane % ROPE_DIM < half, -1.0, 1.0), F32)
    use_b = jnp.asarray((lane // ROPE_DIM) % 2 == 1)
    freq = inv[lane % half]
    ang_a = pos_a.astype(F32)[:, None] * freq[None, :]
    ang_b = pos_b.astype(F32)[:, None] * freq[None, :]
    cos = jnp.where(use_b[None, :], jnp.cos(ang_b), jnp.cos(ang_a))
    sin = jnp.where(use_b[None, :], jnp.sin(ang_b), jnp.sin(ang_a)) * sign[None, :]
    return cos, sin


def _group_sum_matrix(group):
    lane = np.arange(LANES)
    g = (lane[:, None] // group == lane[None, :] // group).astype(np.float32)
    return jnp.asarray(np.concatenate([g, g], axis=0), BF16)


def _row(v):
    return v.reshape(1, -1).astype(F32)


def _max_abs(v):
    return jnp.max(jnp.abs(v.astype(F32)))


def kernel(x, ffn1_norm, ffn1_w_gate, ffn1_w_up, ffn1_w_down, mix_norm, ffn2_norm, ffn2_w_gate, ffn2_w_up, ffn2_w_down, ab_w_in, mla_q_lora_norm, mla_w_uq, mla_kv_lora_norm, mla_w_ukv, mla_q_norm, mla_k_norm, diff_q_norm, diff_k_norm, diff_lambda_q1, diff_lambda_k1, diff_lambda_q2, diff_lambda_k2, diff_subln, ab_w_out, c_w_in, c_q_norm, c_k_norm, c_w_out):
    bsz, seq, d = x.shape
    depth = ffn1_norm.shape[0]
    tm = _tile(seq, TOKEN_TILE)

    pos = jnp.arange(seq, dtype=jnp.int32)
    cos_t, sin_t = _rope_lane_tables(pos, pos)
    cos_ax, sin_ax = _rope_lane_tables(pos // GRID_W, pos % GRID_W)
    lane = np.arange(LANES)
    mla_rot = jnp.asarray((lane >= MLA_NOPE) & (lane < MLA_NOPE + MLA_ROPE))
    cos_m = jnp.where(mla_rot[None, :], cos_t, 1.0)
    sin_m = jnp.where(mla_rot[None, :], sin_t, 0.0)
    gs_all, gs64, gs32 = _group_sum_matrix(LANES), _group_sum_matrix(C_HEAD), _group_sum_matrix(DIFF_HEAD)

    for i in range(depth):
        j = i // 2
        ffn1 = (_row(ffn1_norm[i]), ffn1_w_gate[i].astype(BF16), ffn1_w_up[i].astype(BF16),
                ffn1_w_down[i].astype(BF16))
        ffn2 = (_row(ffn2_norm[i]), ffn2_w_gate[i].astype(BF16), ffn2_w_up[i].astype(BF16),
                ffn2_w_down[i].astype(BF16))
        if i % 2 == 0:
            lam_init = 0.8 - 0.6 * math.exp(-0.3 * i)
            w = ab_w_in[j].astype(BF16)
            o_kr = MLA_Q_RANK + MLA_KV_RANK
            o_dq = o_kr + MLA_ROPE
            o_dv = o_dq + 2 * DIFF_W
            kr_blk = jnp.pad(w[:, o_kr:o_dq], ((0, 0), (MLA_NOPE, LANES - MLA_DIM)))
            w_in = jnp.concatenate([w[:, 0:o_kr], kr_blk, w[:, o_dq:o_dv]], axis=1)
            w_vt = w[:, o_dv:].T
            w_uq = jnp.pad(mla_w_uq[j].reshape(MLA_Q_RANK, MLA_HEADS, MLA_DIM),
                           ((0, 0), (0, 0), (0, LANES - MLA_DIM))
                           ).reshape(MLA_Q_RANK, MLA_HEADS * LANES).astype(BF16)
            ukv = mla_w_ukv[j].reshape(MLA_KV_RANK, MLA_HEADS, MLA_NOPE + MLA_V)
            w_uk = jnp.pad(ukv[:, :, :MLA_NOPE], ((0, 0), (0, 0), (0, LANES - MLA_NOPE))
                           ).reshape(MLA_KV_RANK, MLA_HEADS * LANES).astype(BF16)
            w_uvt = ukv[:, :, MLA_NOPE:].reshape(MLA_KV_RANK, MLA_HEADS * MLA_V).T.astype(BF16)
            pad96 = (0, LANES - MLA_DIM)
            consts = ffn1 + (_row(mix_norm[i]), w_in, w_vt,
                             _row(mla_q_lora_norm[j]), w_uq, _row(mla_kv_lora_norm[j]), w_uk, w_uvt,
                             _row(jnp.pad(mla_q_norm[j], pad96)), _row(jnp.pad(mla_k_norm[j], pad96)),
                             _row(jnp.tile(diff_q_norm[j], LANES // DIFF_HEAD)),
                             _row(jnp.tile(diff_k_norm[j], LANES // DIFF_HEAD)),
                             gs_all, gs32)
            x, q, k, vt, dq, dk, dvt = _token_call(
                _even_in_kernel, f"even_in_{i}", x, consts, (cos_m, sin_m, cos_t, sin_t), (),
                ((d, F32, False), (MLA_HEADS * LANES, BF16, False), (MLA_HEADS * LANES, BF16, False),
                 (VT_ROWS * MLA_HEADS, BF16, True), (DIFF_W, BF16, False), (DIFF_W, BF16, False),
                 (VT_ROWS * DIFF_HEADS, BF16, True)), tm)
            mla_bound = (MLA_DIM ** 0.5 * LOG2E) * _max_abs(mla_q_norm[j]) * _max_abs(mla_k_norm[j])
            o_mla = _attn_dispatch(_mla_attn_kernel, f"mla_attn_{i}", mla_bound, q, k, vt, (),
                                   2 * LANES, 2 * LANES, MLA_HEADS // 2, 1, None)
            lam_p = jnp.stack([diff_lambda_q1[j], diff_lambda_k1[j],
                               diff_lambda_q2[j], diff_lambda_k2[j]]).astype(F32)
            sub = _row(jnp.tile(diff_subln[j], 2))
            diff_bound = (DIFF_HEAD ** 0.5 * LOG2E) * _max_abs(diff_q_norm[j]) * _max_abs(diff_k_norm[j])
            o_diff = _attn_dispatch(functools.partial(_diff_attn_kernel, lam_init), f"diff_attn_{i}",
                                    diff_bound, dq, dk, dvt, (lam_p, sub),
                                    LANES, LANES, DIFF_HEADS // 2, 1, None)
            wo = ab_w_out[j].astype(BF16)
            consts = (wo[:MLA_HEADS * MLA_V], wo[MLA_HEADS * MLA_V:]) + ffn2
            (x,) = _token_call(_even_out_kernel, f"even_out_{i}", x, consts, (), (o_mla, o_diff),
                               ((d, F32, False),), tm)
        else:
            w = c_w_in[j].astype(BF16)
            wq = w[:, :C_Q_W].reshape(d, 2, 2, C_GROUPS, C_HEAD)
            wq = wq.transpose(0, 1, 3, 2, 4).reshape(d, C_Q_W)
            w_in = jnp.concatenate([wq, w[:, C_Q_W:C_Q_W + C_KV_W]], axis=1)
            w_vt = w[:, C_Q_W + C_KV_W:].T
            consts = ffn1 + (_row(mix_norm[i]), w_in, w_vt,
                             _row(jnp.tile(c_q_norm[j], 2)), _row(jnp.tile(c_k_norm[j], 2)), gs64)
            x, q, k, vt = _token_call(
                _odd_in_kernel, f"odd_in_{i}", x, consts, (cos_ax, sin_ax), (),
                ((d, F32, False), (C_HEADS * 2 * LANES, FP8, False), (C_KV_HEADS * 2 * LANES, FP8, False),
                 (VT_ROWS * C_KV_HEADS, BF16, True)), tm)
            gqa_bound = (C_HEAD ** 0.5 * LOG2E) * _max_abs(c_q_norm[j]) * _max_abs(c_k_norm[j])
            o = _attn_dispatch(_gqa_attn_kernel, f"gqa_attn_{i}", gqa_bound, q, k, vt, (),
                               4 * LANES, 4 * LANES, C_HEADS // 2, C_GROUPS, _gqa_attn_mxu_kernel)
            wo = c_w_out[j].astype(BF16).reshape(2, 2, C_GROUPS, C_HEAD, d).transpose(0, 2, 1, 3, 4)
            wo = wo.reshape(C_Q_W, d)
            consts = (wo,) + ffn2
            (x,) = _token_call(_odd_out_kernel, f"odd_out_{i}", x, consts, (), (o,),
                               ((d, F32, False),), tm)
    return x
```

```python
import functools
import math

import jax
import jax.numpy as jnp
import numpy as np
from jax import lax
from jax.experimental import pallas as pl
from jax.experimental.pallas import tpu as pltpu

F32 = jnp.float32
BF16 = jnp.bfloat16
FP8 = jnp.float8_e4m3fn

LANES = 128
EPS = 1e-6
ROPE_THETA = 10000.0
GRID_W = 64
LOG2E = math.log2(math.e)

MLA_HEADS, MLA_Q_RANK, MLA_KV_RANK = 8, 256, 128
MLA_NOPE, MLA_ROPE, MLA_V = 64, 32, 64
MLA_DIM = MLA_NOPE + MLA_ROPE
DIFF_HEADS, DIFF_HEAD = 8, 32
DIFF_W = DIFF_HEADS * 2 * DIFF_HEAD
C_HEADS, C_KV_HEADS, C_HEAD = 16, 4, 64
C_GROUPS = C_HEADS // C_KV_HEADS
C_Q_W, C_KV_W = C_HEADS * C_HEAD, C_KV_HEADS * C_HEAD
ROPE_DIM = 32

Z_CKV = MLA_Q_RANK
Z_KR = Z_CKV + MLA_KV_RANK
Z_DQ = Z_KR + LANES
Z_DK = Z_DQ + DIFF_W

VMEM_LIMIT = 56 * 1024 * 1024
TOKEN_TILE = 512
TOKEN_SPLIT_IN = 2
TOKEN_SPLIT_OUT = 1
QUERY_TILE_BOUNDED = 1024
QUERY_TILE_SHIFTED = 512
OUT_BLOCKS_BOUNDED = 2
OUT_BLOCKS_SHIFTED = 1
QUERY_TILE_MXU = 1024
NT_DIMS = (((1,), (1,)), ((), ()))
LOGIT_BOUND = 64.0
HEAD_V = 64
ONES_ROWS = 64
VT_ROWS = HEAD_V + ONES_ROWS


def _rms(x, g):
    ms = jnp.mean(x * x, axis=-1, keepdims=True)
    return x * lax.rsqrt(ms + EPS) * g


def _bdot(a, b):
    return jnp.dot(a.astype(BF16), b, preferred_element_type=F32)


def _swiglu_half(x, g, wg, wu, wd):
    h = _rms(x, g).astype(BF16)
    gate = jnp.dot(h, wg[...], preferred_element_type=F32)
    up = jnp.dot(h, wu[...], preferred_element_type=F32)
    act = (gate / (1.0 + jnp.exp(-gate))) * up
    return x + 0.5 * jnp.dot(act.astype(BF16), wd[...], preferred_element_type=F32)


def _lane_iota(rows):
    return lax.broadcasted_iota(jnp.int32, (rows, LANES), 1)


def _norm_rope_block(x, gsum, inv_count, gain, cos, sin, post_scale):
    lane = _lane_iota(x.shape[0])
    x2 = x * x
    hi = x2.astype(BF16)
    lo = (x2 - hi.astype(F32)).astype(BF16)
    ss = jnp.dot(jnp.concatenate([hi, lo], axis=1), gsum, preferred_element_type=F32)
    y = x * lax.rsqrt(ss * inv_count + EPS) * gain
    up = pltpu.roll(y, LANES - ROPE_DIM // 2, axis=1)
    dn = pltpu.roll(y, ROPE_DIM // 2, axis=1)
    partner = jnp.where((lane % ROPE_DIM) < ROPE_DIM // 2, up, dn)
    y = y * cos + partner * sin
    if post_scale != 1.0:
        y = y * post_scale
    return y


def _softmax_pv_t(qm, k, vt, bounded):
    st = lax.dot_general(k, qm, NT_DIMS, preferred_element_type=F32)
    if not bounded:
        st = st - jnp.max(st, axis=0, keepdims=True)
    ot = jnp.dot(vt, jnp.exp2(st).astype(BF16), preferred_element_type=F32)
    return ot[:HEAD_V] / ot[HEAD_V:HEAD_V + 1]


def _store_vt_with_ones(vt_ref, cols, vt):
    ones = jnp.ones((ONES_ROWS, vt.shape[1]), BF16)
    for hd in range(vt.shape[0] // HEAD_V):
        vt_ref[hd * VT_ROWS:hd * VT_ROWS + HEAD_V, cols] = vt[hd * HEAD_V:(hd + 1) * HEAD_V].astype(BF16)
        vt_ref[hd * VT_ROWS + HEAD_V:(hd + 1) * VT_ROWS, cols] = ones


def _fp8_split(y):
    hi = y.astype(FP8).astype(F32)
    return hi, (y - hi).astype(FP8).astype(F32)


def _pair_operands(hi, lo, is_query):
    first = _lane_iota(hi.shape[0]) < C_HEAD
    hi_sw, lo_sw = pltpu.roll(hi, C_HEAD, axis=1), pltpu.roll(lo, C_HEAD, axis=1)
    zero = jnp.zeros_like(hi)
    out = []
    for h_own, h_oth, l_own, l_oth in ((hi, hi_sw, lo, lo_sw), (hi_sw, hi, lo_sw, lo)):
        if is_query:
            a, b = jnp.where(first, h_own, l_oth), jnp.where(first, h_own, zero)
        else:
            a, b = jnp.where(first, h_own, h_oth), jnp.where(first, l_own, zero)
        out.append(jnp.concatenate([a, b], axis=1).astype(FP8))
    return out


def _row_groups(n_rows, split):
    step = n_rows // split
    return [slice(r * step, (r + 1) * step) for r in range(split)]


def _even_in_kernel(x_ref, g1_ref, wg_ref, wu_ref, wd_ref, gm_ref, win_ref, wvt_ref,
                    gq_ref, wuq_ref, gkv_ref, wuk_ref, wuvt_ref, qn_ref, kn_ref, dqn_ref, dkn_ref,
                    gs96_ref, gs32_ref, cm_ref, sm_ref, cd_ref, sd_ref,
                    xo_ref, q_ref, k_ref, vt_ref, dq_ref, dk_ref, dvt_ref):
    q_scale = MLA_DIM ** -0.5 * LOG2E
    d_scale = DIFF_HEAD ** -0.5 * LOG2E
    xo_ref[...] = _swiglu_half(x_ref[...], g1_ref[...], wg_ref, wu_ref, wd_ref)
    for rows in _row_groups(x_ref.shape[0], TOKEN_SPLIT_IN):
        x = xo_ref[rows, :]
        h = _rms(x, gm_ref[...]).astype(BF16)
        z = jnp.dot(h, win_ref[...], preferred_element_type=F32)
        _store_vt_with_ones(dvt_ref, rows, lax.dot_general(wvt_ref[...], h, NT_DIMS,
                                                           preferred_element_type=F32))
        kr = z[:, Z_KR:Z_DQ]
        cm, sm, cd, sd = cm_ref[rows, :], sm_ref[rows, :], cd_ref[rows, :], sd_ref[rows, :]
        ckv = _rms(z[:, Z_CKV:Z_KR], gkv_ref[...]).astype(BF16)
        qf = _bdot(_rms(z[:, 0:Z_CKV], gq_ref[...]), wuq_ref[...])
        kf = jnp.dot(ckv, wuk_ref[...], preferred_element_type=F32)
        _store_vt_with_ones(vt_ref, rows, lax.dot_general(wuvt_ref[...], ckv, NT_DIMS,
                                                          preferred_element_type=F32))
        for hd in range(MLA_HEADS):
            sl = slice(hd * LANES, (hd + 1) * LANES)
            q_ref[rows, sl] = _norm_rope_block(qf[:, sl], gs96_ref[...], 1.0 / MLA_DIM, qn_ref[...],
                                               cm, sm, q_scale).astype(BF16)
            k_ref[rows, sl] = _norm_rope_block(kf[:, sl] + kr, gs96_ref[...], 1.0 / MLA_DIM, kn_ref[...],
                                               cm, sm, 1.0).astype(BF16)
        for b in range(DIFF_W // LANES):
            sl = slice(b * LANES, (b + 1) * LANES)
            dq_ref[rows, sl] = _norm_rope_block(z[:, Z_DQ + b * LANES:Z_DQ + (b + 1) * LANES],
                                                gs32_ref[...], 1.0 / DIFF_HEAD, dqn_ref[...],
                                                cd, sd, d_scale).astype(BF16)
            dk_ref[rows, sl] = _norm_rope_block(z[:, Z_DK + b * LANES:Z_DK + (b + 1) * LANES],
                                                gs32_ref[...], 1.0 / DIFF_HEAD, dkn_ref[...],
                                                cd, sd, 1.0).astype(BF16)


def _odd_in_kernel(x_ref, g1_ref, wg_ref, wu_ref, wd_ref, gm_ref, win_ref, wvt_ref,
                   qn_ref, kn_ref, gs64_ref, ca_ref, sa_ref,
                   xo_ref, q_ref, k_ref, vt_ref):
    q_scale = C_HEAD ** -0.5 * LOG2E
    xo_ref[...] = _swiglu_half(x_ref[...], g1_ref[...], wg_ref, wu_ref, wd_ref)
    for rows in _row_groups(x_ref.shape[0], TOKEN_SPLIT_IN):
        x = xo_ref[rows, :]
        h = _rms(x, gm_ref[...]).astype(BF16)
        z = jnp.dot(h, win_ref[...], preferred_element_type=F32)
        _store_vt_with_ones(vt_ref, rows, lax.dot_general(wvt_ref[...], h, NT_DIMS,
                                                          preferred_element_type=F32))
        ca, sa = ca_ref[rows, :], sa_ref[rows, :]
        for b in range(C_Q_W // LANES):
            y = _norm_rope_block(z[:, b * LANES:(b + 1) * LANES], gs64_ref[...], 1.0 / C_HEAD,
                                 qn_ref[...], ca, sa, q_scale)
            hi, lo = _fp8_split(y)
            q_ref[rows, 2 * b * LANES:(2 * b + 1) * LANES] = hi.astype(FP8)
            q_ref[rows, (2 * b + 1) * LANES:(2 * b + 2) * LANES] = lo.astype(FP8)
        for b in range(C_KV_W // LANES):
            y = _norm_rope_block(z[:, C_Q_W + b * LANES:C_Q_W + (b + 1) * LANES], gs64_ref[...],
                                 1.0 / C_HEAD, kn_ref[...], ca, sa, 1.0)
            for e, op in enumerate(_pair_operands(*_fp8_split(y), is_query=False)):
                k_ref[rows, (2 * b + e) * 2 * LANES:(2 * b + e + 1) * 2 * LANES] = op


def _even_out_kernel(x_ref, oa_ref, ob_ref, woa_ref, wob_ref, g2_ref, wg_ref, wu_ref, wd_ref, xo_ref):
    for rows in _row_groups(x_ref.shape[0], TOKEN_SPLIT_OUT):
        x = x_ref[rows, :]
        x = x + jnp.dot(oa_ref[rows, :], woa_ref[...], preferred_element_type=F32)
        x = x + jnp.dot(ob_ref[rows, :], wob_ref[...], preferred_element_type=F32)
        xo_ref[rows, :] = _swiglu_half(x, g2_ref[...], wg_ref, wu_ref, wd_ref)


def _odd_out_kernel(x_ref, o_ref, wo_ref, g2_ref, wg_ref, wu_ref, wd_ref, xo_ref):
    for rows in _row_groups(x_ref.shape[0], TOKEN_SPLIT_OUT):
        x = x_ref[rows, :] + jnp.dot(o_ref[rows, :], wo_ref[...], preferred_element_type=F32)
        xo_ref[rows, :] = _swiglu_half(x, g2_ref[...], wg_ref, wu_ref, wd_ref)


def _pair_rows(ot0, ot1):
    return jnp.concatenate([ot0, ot1], axis=0).T


def _out_blocks(o_ref):
    return range(o_ref.shape[1] // LANES)


def _mla_attn_kernel(bounded, q_ref, k_ref, vt_ref, o_ref):
    for s in _out_blocks(o_ref):
        outs = []
        for hd in (2 * s, 2 * s + 1):
            sl = slice(hd * LANES, (hd + 1) * LANES)
            outs.append(_softmax_pv_t(q_ref[:, sl], k_ref[:, sl],
                                      vt_ref[hd * VT_ROWS:(hd + 1) * VT_ROWS, :], bounded))
        o_ref[:, s * LANES:(s + 1) * LANES] = _pair_rows(*outs).astype(BF16)


def _gqa_attn_kernel(bounded, q_ref, k_ref, vt_ref, o_ref):
    w = 2 * LANES
    for s in _out_blocks(o_ref):
        hi = q_ref[:, s * w:s * w + LANES].astype(F32)
        lo = q_ref[:, s * w + LANES:(s + 1) * w].astype(F32)
        outs = []
        for e, q3 in enumerate(_pair_operands(hi, lo, is_query=True)):
            outs.append(_softmax_pv_t(q3, k_ref[:, e * w:(e + 1) * w],
                                      vt_ref[e * VT_ROWS:(e + 1) * VT_ROWS, :], bounded))
        o_ref[:, s * LANES:(s + 1) * LANES] = _pair_rows(*outs).astype(BF16)


MXU_TILE = 256
PIPE_LAG = 2
S_ADDR = (0, 64, 128)
O_ADDR = (192, 224)
PV_ROWS = HEAD_V + 16


def _mxu_softmax_pv(maps, k_ref, vt_ref):
    n_k = k_ref.shape[0] // MXU_TILE
    for mxu in (0, 1):
        for a in S_ADDR:
            pltpu.matmul_pop(a, (MXU_TILE, MXU_TILE), F32, mxu)
        for a in O_ADDR:
            pltpu.matmul_pop(a, (PV_ROWS, MXU_TILE), F32, mxu)
    units = {m: [(g, j) for g in range(len(maps[m])) for j in range(n_k)] for m in (0, 1)}
    n_units = len(units[0])
    o_lag = PIPE_LAG + 2
    results = {}
    for i in range(n_units + o_lag):
        for mxu in (0, 1):
            if i < n_units:
                g, j = units[mxu][i]
                _, wq, k_col, _ = maps[mxu][g]
                pltpu.matmul_push_rhs(wq, staging_register=0, mxu_index=mxu)
                pltpu.matmul_acc_lhs(S_ADDR[i % 3],
                                     k_ref[j * MXU_TILE:(j + 1) * MXU_TILE, k_col:k_col + MXU_TILE],
                                     mxu, load_staged_rhs=0)
            if PIPE_LAG <= i < n_units + PIPE_LAG:
                g, j = units[mxu][i - PIPE_LAG]
                v_row = maps[mxu][g][3]
                st = pltpu.matmul_pop(S_ADDR[(i - PIPE_LAG) % 3], (MXU_TILE, MXU_TILE), F32, mxu)
                pltpu.matmul_push_rhs(jnp.exp2(st).astype(BF16), staging_register=1, mxu_index=mxu)
                v_rows = vt_ref[v_row:v_row + PV_ROWS, j * MXU_TILE:(j + 1) * MXU_TILE]
                pltpu.matmul_acc_lhs(O_ADDR[g % 2], v_rows, mxu, load_staged_rhs=1)
            if i >= o_lag:
                g, j = units[mxu][i - o_lag]
                if j == n_k - 1:
                    ot = pltpu.matmul_pop(O_ADDR[g % 2], (PV_ROWS, MXU_TILE), F32, mxu)
                    results[maps[mxu][g][0]] = ot[:HEAD_V] / ot[HEAD_V:HEAD_V + 1]
    return results


def _transposed_tile(q, keep, dtype):
    if keep is not None:
        q = jnp.where(keep, q, jnp.zeros_like(q))
    return q.astype(F32).T.astype(dtype)


def _gqa_attn_mxu_kernel(q_ref, k_ref, vt_ref, o_ref):
    n_q = q_ref.shape[0] // MXU_TILE
    zeros = jnp.zeros((C_HEAD, MXU_TILE), F32)
    tiles = {}
    for s in _out_blocks(o_ref):
        for n in range(n_q):
            t = q_ref[n * MXU_TILE:(n + 1) * MXU_TILE, s * MXU_TILE:(s + 1) * MXU_TILE].astype(F32).T
            for e in range(2):
                h, l = t[e * C_HEAD:(e + 1) * C_HEAD], t[LANES + e * C_HEAD:LANES + (e + 1) * C_HEAD]
                tiles[(s, e, n)] = jnp.concatenate([h, l, h, zeros], axis=0).astype(FP8)
    maps = {0: [], 1: []}
    for s in _out_blocks(o_ref):
        for e in range(2):
            for n in range(n_q):
                maps[n % 2].append(((s, e, n), tiles[(s, e, n)], e * MXU_TILE, e * VT_ROWS))
    res = _mxu_softmax_pv(maps, k_ref, vt_ref)
    for s in _out_blocks(o_ref):
        rows = [jnp.concatenate([res[(s, e, n)] for n in range(n_q)], axis=1) for e in range(2)]
        o_ref[:, s * LANES:(s + 1) * LANES] = _pair_rows(*rows).astype(BF16)


def _mla_attn_mxu_kernel(q_ref, k_ref, vt_ref, o_ref):
    n_q = q_ref.shape[0] // MXU_TILE
    lane = lax.broadcasted_iota(jnp.int32, (MXU_TILE, MXU_TILE), 1)
    maps = {0: [], 1: []}
    for s in _out_blocks(o_ref):
        for e in range(2):
            keep = (lane >= e * LANES) & (lane < (e + 1) * LANES)
            for n in range(n_q):
                q = q_ref[n * MXU_TILE:(n + 1) * MXU_TILE, s * MXU_TILE:(s + 1) * MXU_TILE]
                maps[n % 2].append(((s, e, n), _transposed_tile(q, keep, BF16), s * MXU_TILE,
                                    (2 * s + e) * VT_ROWS))
    res = _mxu_softmax_pv(maps, k_ref, vt_ref)
    for s in _out_blocks(o_ref):
        rows = [jnp.concatenate([res[(s, e, n)] for n in range(n_q)], axis=1) for e in range(2)]
        o_ref[:, s * LANES:(s + 1) * LANES] = _pair_rows(*rows).astype(BF16)


def _diff_attn_mxu_kernel(lam_init, q_ref, k_ref, vt_ref, lp_ref, sub_ref, o_ref):
    lp = lp_ref[...]
    lam = (jnp.exp(jnp.sum(lp[0:1] * lp[1:2], axis=-1, keepdims=True))
           - jnp.exp(jnp.sum(lp[2:3] * lp[3:4], axis=-1, keepdims=True)) + lam_init)
    n_q = q_ref.shape[0] // MXU_TILE
    lane = lax.broadcasted_iota(jnp.int32, (MXU_TILE, MXU_TILE), 1)
    n_win = o_ref.shape[1] // MXU_TILE
    maps = {0: [], 1: []}
    for w in range(n_win):
        for hq in range(4):
            for t in range(2):
                lo_lane = hq * 2 * DIFF_HEAD + t * DIFF_HEAD
                keep = (lane >= lo_lane) & (lane < lo_lane + DIFF_HEAD)
                for n in range(n_q):
                    q = q_ref[n * MXU_TILE:(n + 1) * MXU_TILE, w * MXU_TILE:(w + 1) * MXU_TILE]
                    maps[n % 2].append(((w, hq, t, n), _transposed_tile(q, keep, BF16), w * MXU_TILE,
                                        (4 * w + hq) * VT_ROWS))
    res = _mxu_softmax_pv(maps, k_ref, vt_ref)
    lane128 = _lane_iota(q_ref.shape[0])
    lo = lane128 < 2 * DIFF_HEAD
    inv = 1.0 / (2 * DIFF_HEAD)
    for s in _out_blocks(o_ref):
        w, b = divmod(s, 2)
        heads = []
        for hq in (2 * b, 2 * b + 1):
            a1 = jnp.concatenate([res[(w, hq, 0, n)] for n in range(n_q)], axis=1)
            a2 = jnp.concatenate([res[(w, hq, 1, n)] for n in range(n_q)], axis=1)
            heads.append(a1 - lam * a2)
        o = _pair_rows(*heads)
        o2 = o * o
        r0 = lax.rsqrt(jnp.sum(jnp.where(lo, o2, 0.0), axis=-1, keepdims=True) * inv + EPS)
        r1 = lax.rsqrt(jnp.sum(jnp.where(lo, 0.0, o2), axis=-1, keepdims=True) * inv + EPS)
        o = o * jnp.where(lo, r0, r1) * sub_ref[...] * (1.0 - lam_init)
        o_ref[:, s * LANES:(s + 1) * LANES] = o.astype(BF16)


def _diff_attn_kernel(lam_init, bounded, q_ref, k_ref, vt_ref, lp_ref, sub_ref, o_ref):
    lp = lp_ref[...]
    lam = (jnp.exp(jnp.sum(lp[0:1] * lp[1:2], axis=-1, keepdims=True))
           - jnp.exp(jnp.sum(lp[2:3] * lp[3:4], axis=-1, keepdims=True)) + lam_init)
    lane = _lane_iota(q_ref.shape[0])
    lo = lane < 2 * DIFF_HEAD
    inv = 1.0 / (2 * DIFF_HEAD)
    for s in _out_blocks(o_ref):
        sl = slice(s * LANES, (s + 1) * LANES)
        q, k = q_ref[:, sl], k_ref[:, sl]
        zero = jnp.zeros_like(q)
        outs = []
        for hd in range(2):
            base = hd * 2 * DIFF_HEAD
            m1 = (lane >= base) & (lane < base + DIFF_HEAD)
            m2 = (lane >= base + DIFF_HEAD) & (lane < base + 2 * DIFF_HEAD)
            vt = vt_ref[(2 * s + hd) * VT_ROWS:(2 * s + hd + 1) * VT_ROWS, :]
            a1 = _softmax_pv_t(jnp.where(m1, q, zero), k, vt, bounded)
            a2 = _softmax_pv_t(jnp.where(m2, q, zero), k, vt, bounded)
            outs.append(a1 - lam * a2)
        o = _pair_rows(outs[0], outs[1])
        o2 = o * o
        r0 = lax.rsqrt(jnp.sum(jnp.where(lo, o2, 0.0), axis=-1, keepdims=True) * inv + EPS)
        r1 = lax.rsqrt(jnp.sum(jnp.where(lo, 0.0, o2), axis=-1, keepdims=True) * inv + EPS)
        o = o * jnp.where(lo, r0, r1) * sub_ref[...] * (1.0 - lam_init)
        o_ref[:, sl] = o.astype(BF16)


def _tile(n, want):
    return want if n % want == 0 else n


def _const_spec(shape):
    nd = len(shape)
    return pl.BlockSpec(shape, lambda *_: (0,) * nd, pipeline_mode=pl.Buffered(1))


def _tok_spec(tm, width):
    return pl.BlockSpec((None, tm, width), lambda b, t: (b, t, 0))


def _tok_t_spec(tm, width):
    return pl.BlockSpec((None, width, tm), lambda b, t: (b, 0, t))


def _tab_spec(tm):
    return pl.BlockSpec((tm, LANES), lambda b, t: (t, 0))


def _token_call(body, name, x, consts, tabs, extra_tok, outs, tm):
    bsz, seq, _ = x.shape
    tok_inputs = [x] + list(extra_tok)
    in_specs = ([_tok_spec(tm, a.shape[-1]) for a in tok_inputs]
                + [_const_spec(c.shape) for c in consts]
                + [_tab_spec(tm) for _ in tabs])
    out_shape = [jax.ShapeDtypeStruct((bsz, w, seq) if tr else (bsz, seq, w), d) for w, d, tr in outs]
    out_specs = [_tok_t_spec(tm, w) if tr else _tok_spec(tm, w) for w, d, tr in outs]
    return pl.pallas_call(
        body, name=name,
        grid=(bsz, seq // tm),
        in_specs=in_specs, out_specs=out_specs, out_shape=out_shape,
        compiler_params=pltpu.CompilerParams(
            dimension_semantics=("arbitrary", "arbitrary"), vmem_limit_bytes=VMEM_LIMIT),
    )(*tok_inputs, *consts, *tabs)


def _attn_dispatch(body, name, logit_bound, *args):
    q, k, vt, extra, *static, mxu_body = args

    def run(bounded, q, k, vt, extra):
        tag = "bounded" if bounded else "shifted"
        tq = _tile(q.shape[1], QUERY_TILE_BOUNDED if bounded else QUERY_TILE_SHIFTED)
        n_sub = OUT_BLOCKS_BOUNDED if bounded else OUT_BLOCKS_SHIFTED
        if bounded and mxu_body is not None and q.shape[1] % QUERY_TILE_MXU == 0:
            return _attn_call(mxu_body[0], f"{name}_{tag}", q, k, vt, extra, *static,
                              QUERY_TILE_MXU, mxu_body[1])
        return _attn_call(functools.partial(body, bounded), f"{name}_{tag}", q, k, vt, extra,
                          *static, tq, n_sub)

    return lax.cond(logit_bound <= LOGIT_BOUND, functools.partial(run, True),
                    functools.partial(run, False), q, k, vt, tuple(extra))


def _attn_call(body, name, q, k, vt, extra, q_blk, k_blk, n_blocks, kv_share, tq, n_sub):
    bsz, seq, _ = q.shape
    if kv_share == 1:
        k_spec = pl.BlockSpec((None, seq, k_blk * n_sub), lambda b, h, t: (b, 0, h))
        vt_spec = pl.BlockSpec((None, 2 * VT_ROWS * n_sub, seq), lambda b, h, t: (b, h, 0))
    else:
        assert kv_share % n_sub == 0
        k_spec = pl.BlockSpec((None, seq, k_blk), lambda b, h, t: (b, 0, (h * n_sub) // kv_share))
        vt_spec = pl.BlockSpec((None, 2 * VT_ROWS, seq), lambda b, h, t: (b, (h * n_sub) // kv_share, 0))
    in_specs = [pl.BlockSpec((None, tq, q_blk * n_sub), lambda b, h, t: (b, t, h)), k_spec, vt_spec
                ] + [pl.BlockSpec(e.shape, lambda b, h, t: (0, 0)) for e in extra]
    return pl.pallas_call(
        body, name=name,
        grid=(bsz, n_blocks // n_sub, seq // tq),
        in_specs=in_specs,
        out_specs=pl.BlockSpec((None, tq, LANES * n_sub), lambda b, h, t: (b, t, h)),
        out_shape=jax.ShapeDtypeStruct((bsz, seq, n_blocks * LANES), BF16),
        compiler_params=pltpu.CompilerParams(
            dimension_semantics=("arbitrary", "arbitrary", "arbitrary"), vmem_limit_bytes=VMEM_LIMIT),
    )(q, k, vt, *extra)


def _rope_lane_tables(pos_a, pos_b):
    half = ROPE_DIM // 2
    inv = ROPE_THETA ** (-jnp.arange(0, ROPE_DIM, 2, dtype=F32) / ROPE_DIM)
    lane = np.arange(LANES)
    sign = jnp.asarray(np.where(lane % ROPE_DIM < half, -1.0, 1.0), F32)
    use_b = jnp.asarray((lane // ROPE_DIM) % 2 == 1)
    freq = inv[lane % half]
    ang_a = pos_a.astype(F32)[:, None] * freq[None, :]
    ang_b = pos_b.astype(F32)[:, None] * freq[None, :]
    cos = jnp.where(use_b[None, :], jnp.cos(ang_b), jnp.cos(ang_a))
    sin = jnp.where(use_b[None, :], jnp.sin(ang_b), jnp.sin(ang_a)) * sign[None, :]
    return cos, sin


def _group_sum_matrix(group):
    lane = np.arange(LANES)
    g = (lane[:, None] // group == lane[None, :] // group).astype(np.float32)
    return jnp.asarray(np.concatenate([g, g], axis=0), BF16)


def _row(v):
    return v.reshape(1, -1).astype(F32)


def _max_abs(v):
    return jnp.max(jnp.abs(v.astype(F32)))


def kernel(x, ffn1_norm, ffn1_w_gate, ffn1_w_up, ffn1_w_down, mix_norm, ffn2_norm, ffn2_w_gate, ffn2_w_up, ffn2_w_down, ab_w_in, mla_q_lora_norm, mla_w_uq, mla_kv_lora_norm, mla_w_ukv, mla_q_norm, mla_k_norm, diff_q_norm, diff_k_norm, diff_lambda_q1, diff_lambda_k1, diff_lambda_q2, diff_lambda_k2, diff_subln, ab_w_out, c_w_in, c_q_norm, c_k_norm, c_w_out):
    bsz, seq, d = x.shape
    depth = ffn1_norm.shape[0]
    tm = _tile(seq, TOKEN_TILE)

    pos = jnp.arange(seq, dtype=jnp.int32)
    cos_t, sin_t = _rope_lane_tables(pos, pos)
    cos_ax, sin_ax = _rope_lane_tables(pos // GRID_W, pos % GRID_W)
    lane = np.arange(LANES)
    mla_rot = jnp.asarray((lane >= MLA_NOPE) & (lane < MLA_NOPE + MLA_ROPE))
    cos_m = jnp.where(mla_rot[None, :], cos_t, 1.0)
    sin_m = jnp.where(mla_rot[None, :], sin_t, 0.0)
    gs_all, gs64, gs32 = _group_sum_matrix(LANES), _group_sum_matrix(C_HEAD), _group_sum_matrix(DIFF_HEAD)

    for i in range(depth):
        j = i // 2
        ffn1 = (_row(ffn1_norm[i]), ffn1_w_gate[i].astype(BF16), ffn1_w_up[i].astype(BF16),
                ffn1_w_down[i].astype(BF16))
        ffn2 = (_row(ffn2_norm[i]), ffn2_w_gate[i].astype(BF16), ffn2_w_up[i].astype(BF16),
                ffn2_w_down[i].astype(BF16))
        if i % 2 == 0:
            lam_init = 0.8 - 0.6 * math.exp(-0.3 * i)
            w = ab_w_in[j].astype(BF16)
            o_kr = MLA_Q_RANK + MLA_KV_RANK
            o_dq = o_kr + MLA_ROPE
            o_dv = o_dq + 2 * DIFF_W
            kr_blk = jnp.pad(w[:, o_kr:o_dq], ((0, 0), (MLA_NOPE, LANES - MLA_DIM)))
            w_in = jnp.concatenate([w[:, 0:o_kr], kr_blk, w[:, o_dq:o_dv]], axis=1)
            w_vt = w[:, o_dv:].T
            w_uq = jnp.pad(mla_w_uq[j].reshape(MLA_Q_RANK, MLA_HEADS, MLA_DIM),
                           ((0, 0), (0, 0), (0, LANES - MLA_DIM))
                           ).reshape(MLA_Q_RANK, MLA_HEADS * LANES).astype(BF16)
            ukv = mla_w_ukv[j].reshape(MLA_KV_RANK, MLA_HEADS, MLA_NOPE + MLA_V)
            w_uk = jnp.pad(ukv[:, :, :MLA_NOPE], ((0, 0), (0, 0), (0, LANES - MLA_NOPE))
                           ).reshape(MLA_KV_RANK, MLA_HEADS * LANES).astype(BF16)
            w_uvt = ukv[:, :, MLA_NOPE:].reshape(MLA_KV_RANK, MLA_HEADS * MLA_V).T.astype(BF16)
            pad96 = (0, LANES - MLA_DIM)
            consts = ffn1 + (_row(mix_norm[i]), w_in, w_vt,
                             _row(mla_q_lora_norm[j]), w_uq, _row(mla_kv_lora_norm[j]), w_uk, w_uvt,
                             _row(jnp.pad(mla_q_norm[j], pad96)), _row(jnp.pad(mla_k_norm[j], pad96)),
                             _row(jnp.tile(diff_q_norm[j], LANES // DIFF_HEAD)),
                             _row(jnp.tile(diff_k_norm[j], LANES // DIFF_HEAD)),
                             gs_all, gs32)
            x, q, k, vt, dq, dk, dvt = _token_call(
                _even_in_kernel, f"even_in_{i}", x, consts, (cos_m, sin_m, cos_t, sin_t), (),
                ((d, F32, False), (MLA_HEADS * LANES, BF16, False), (MLA_HEADS * LANES, BF16, False),
                 (VT_ROWS * MLA_HEADS, BF16, True), (DIFF_W, BF16, False), (DIFF_W, BF16, False),
                 (VT_ROWS * DIFF_HEADS, BF16, True)), tm)
            mla_bound = (MLA_DIM ** 0.5 * LOG2E) * _max_abs(mla_q_norm[j]) * _max_abs(mla_k_norm[j])
            o_mla = _attn_dispatch(_mla_attn_kernel, f"mla_attn_{i}", mla_bound, q, k, vt, (),
                                   2 * LANES, 2 * LANES, MLA_HEADS // 2, 1, (_mla_attn_mxu_kernel, 2))
            lam_p = jnp.stack([diff_lambda_q1[j], diff_lambda_k1[j],
                               diff_lambda_q2[j], diff_lambda_k2[j]]).astype(F32)
            sub = _row(jnp.tile(diff_subln[j], 2))
            diff_bound = (DIFF_HEAD ** 0.5 * LOG2E) * _max_abs(diff_q_norm[j]) * _max_abs(diff_k_norm[j])
            o_diff = _attn_dispatch(functools.partial(_diff_attn_kernel, lam_init), f"diff_attn_{i}",
                                    diff_bound, dq, dk, dvt, (lam_p, sub),
                                    LANES, LANES, DIFF_HEADS // 2, 1,
                                    (functools.partial(_diff_attn_mxu_kernel, lam_init), 2))
            wo = ab_w_out[j].astype(BF16)
            consts = (wo[:MLA_HEADS * MLA_V], wo[MLA_HEADS * MLA_V:]) + ffn2
            (x,) = _token_call(_even_out_kernel, f"even_out_{i}", x, consts, (), (o_mla, o_diff),
                               ((d, F32, False),), tm)
        else:
            w = c_w_in[j].astype(BF16)
            wq = w[:, :C_Q_W].reshape(d, 2, 2, C_GROUPS, C_HEAD)
            wq = wq.transpose(0, 1, 3, 2, 4).reshape(d, C_Q_W)
            w_in = jnp.concatenate([wq, w[:, C_Q_W:C_Q_W + C_KV_W]], axis=1)
            w_vt = w[:, C_Q_W + C_KV_W:].T
            consts = ffn1 + (_row(mix_norm[i]), w_in, w_vt,
                             _row(jnp.tile(c_q_norm[j], 2)), _row(jnp.tile(c_k_norm[j], 2)), gs64)
            x, q, k, vt = _token_call(
                _odd_in_kernel, f"odd_in_{i}", x, consts, (cos_ax, sin_ax), (),
                ((d, F32, False), (2 * C_Q_W, FP8, False), (C_KV_HEADS * 2 * LANES, FP8, False),
                 (VT_ROWS * C_KV_HEADS, BF16, True)), tm)
            gqa_bound = (C_HEAD ** 0.5 * LOG2E) * _max_abs(c_q_norm[j]) * _max_abs(c_k_norm[j])
            o = _attn_dispatch(_gqa_attn_kernel, f"gqa_attn_{i}", gqa_bound, q, k, vt, (),
                               2 * LANES, 4 * LANES, C_HEADS // 2, C_GROUPS, (_gqa_attn_mxu_kernel, 4))
            wo = c_w_out[j].astype(BF16).reshape(2, 2, C_GROUPS, C_HEAD, d).transpose(0, 2, 1, 3, 4)
            wo = wo.reshape(C_Q_W, d)
            consts = (wo,) + ffn2
            (x,) = _token_call(_odd_out_kernel, f"odd_out_{i}", x, consts, (), (o,),
                               ((d, F32, False),), tm)
    return x
```

```python
import functools
import math

import jax
import jax.numpy as jnp
import numpy as np
from jax import lax
from jax.experimental import pallas as pl
from jax.experimental.pallas import tpu as pltpu

F32 = jnp.float32
BF16 = jnp.bfloat16
FP8 = jnp.float8_e4m3fn

LANES = 128
EPS = 1e-6
ROPE_THETA = 10000.0
GRID_W = 64
LOG2E = math.log2(math.e)

MLA_HEADS, MLA_Q_RANK, MLA_KV_RANK = 8, 256, 128
MLA_NOPE, MLA_ROPE, MLA_V = 64, 32, 64
MLA_DIM = MLA_NOPE + MLA_ROPE
DIFF_HEADS, DIFF_HEAD = 8, 32
DIFF_W = DIFF_HEADS * 2 * DIFF_HEAD
C_HEADS, C_KV_HEADS, C_HEAD = 16, 4, 64
C_GROUPS = C_HEADS // C_KV_HEADS
C_Q_W, C_KV_W = C_HEADS * C_HEAD, C_KV_HEADS * C_HEAD
ROPE_DIM = 32

Z_CKV = MLA_Q_RANK
Z_KR = Z_CKV + MLA_KV_RANK
Z_DQ = Z_KR + LANES
Z_DK = Z_DQ + DIFF_W

VMEM_LIMIT = 56 * 1024 * 1024
TOKEN_TILE = 512
TOKEN_SPLIT_IN = 2
TOKEN_SPLIT_OUT = 1
QUERY_TILE_BOUNDED = 1024
QUERY_TILE_SHIFTED = 512
OUT_BLOCKS_BOUNDED = 2
OUT_BLOCKS_SHIFTED = 1
QUERY_TILE_MXU = 1024
MXU_BLOCKS_MLA, MXU_BLOCKS_DIFF, MXU_BLOCKS_GQA = 2, 2, 4
NT_DIMS = (((1,), (1,)), ((), ()))
LOGIT_BOUND = 64.0
HEAD_V = 64
ONES_ROWS = 64
VT_ROWS = HEAD_V + ONES_ROWS


def _rms(x, g):
    ms = jnp.mean(x * x, axis=-1, keepdims=True)
    return x * lax.rsqrt(ms + EPS) * g


def _bdot(a, b):
    return jnp.dot(a.astype(BF16), b, preferred_element_type=F32)


def _swiglu_half(x, g, wg, wu, wd):
    h = _rms(x, g).astype(BF16)
    gate = jnp.dot(h, wg[...], preferred_element_type=F32)
    up = jnp.dot(h, wu[...], preferred_element_type=F32)
    act = (gate / (1.0 + jnp.exp(-gate))) * up
    return x + 0.5 * jnp.dot(act.astype(BF16), wd[...], preferred_element_type=F32)


def _lane_iota(rows):
    return lax.broadcasted_iota(jnp.int32, (rows, LANES), 1)


def _norm_rope_block(x, gsum, inv_count, gain, cos, sin, post_scale):
    lane = _lane_iota(x.shape[0])
    x2 = x * x
    hi = x2.astype(BF16)
    lo = (x2 - hi.astype(F32)).astype(BF16)
    ss = jnp.dot(jnp.concatenate([hi, lo], axis=1), gsum, preferred_element_type=F32)
    y = x * lax.rsqrt(ss * inv_count + EPS) * gain
    up = pltpu.roll(y, LANES - ROPE_DIM // 2, axis=1)
    dn = pltpu.roll(y, ROPE_DIM // 2, axis=1)
    partner = jnp.where((lane % ROPE_DIM) < ROPE_DIM // 2, up, dn)
    y = y * cos + partner * sin
    if post_scale != 1.0:
        y = y * post_scale
    return y


def _softmax_pv_t(qm, k, vt, bounded):
    st = lax.dot_general(k, qm, NT_DIMS, preferred_element_type=F32)
    if not bounded:
        st = st - jnp.max(st, axis=0, keepdims=True)
    ot = jnp.dot(vt, jnp.exp2(st).astype(BF16), preferred_element_type=F32)
    return ot[:HEAD_V] / ot[HEAD_V:HEAD_V + 1]


def _store_vt_with_ones(vt_ref, cols, vt):
    ones = jnp.ones((ONES_ROWS, vt.shape[1]), BF16)
    for hd in range(vt.shape[0] // HEAD_V):
        vt_ref[hd * VT_ROWS:hd * VT_ROWS + HEAD_V, cols] = vt[hd * HEAD_V:(hd + 1) * HEAD_V].astype(BF16)
        vt_ref[hd * VT_ROWS + HEAD_V:(hd + 1) * VT_ROWS, cols] = ones


def _fp8_split(y):
    hi = y.astype(FP8).astype(F32)
    return hi, (y - hi).astype(FP8).astype(F32)


def _pair_operands(hi, lo, is_query):
    first = _lane_iota(hi.shape[0]) < C_HEAD
    hi_sw, lo_sw = pltpu.roll(hi, C_HEAD, axis=1), pltpu.roll(lo, C_HEAD, axis=1)
    zero = jnp.zeros_like(hi)
    out = []
    for h_own, h_oth, l_own, l_oth in ((hi, hi_sw, lo, lo_sw), (hi_sw, hi, lo_sw, lo)):
        if is_query:
            a, b = jnp.where(first, h_own, l_oth), jnp.where(first, h_own, zero)
        else:
            a, b = jnp.where(first, h_own, h_oth), jnp.where(first, l_own, zero)
        out.append(jnp.concatenate([a, b], axis=1).astype(FP8))
    return out


def _row_groups(n_rows, split):
    step = n_rows // split
    return [slice(r * step, (r + 1) * step) for r in range(split)]


def _even_in_kernel(x_ref, g1_ref, wg_ref, wu_ref, wd_ref, gm_ref, win_ref, wvt_ref,
                    gq_ref, wuq_ref, gkv_ref, wuk_ref, wuvt_ref, qn_ref, kn_ref, dqn_ref, dkn_ref,
                    gs96_ref, gs32_ref, cm_ref, sm_ref, cd_ref, sd_ref,
                    xo_ref, q_ref, k_ref, vt_ref, dq_ref, dk_ref, dvt_ref):
    q_scale = MLA_DIM ** -0.5 * LOG2E
    d_scale = DIFF_HEAD ** -0.5 * LOG2E
    xo_ref[...] = _swiglu_half(x_ref[...], g1_ref[...], wg_ref, wu_ref, wd_ref)
    for rows in _row_groups(x_ref.shape[0], TOKEN_SPLIT_IN):
        x = xo_ref[rows, :]
        h = _rms(x, gm_ref[...]).astype(BF16)
        z = jnp.dot(h, win_ref[...], preferred_element_type=F32)
        _store_vt_with_ones(dvt_ref, rows, lax.dot_general(wvt_ref[...], h, NT_DIMS,
                                                           preferred_element_type=F32))
        kr = z[:, Z_KR:Z_DQ]
        cm, sm, cd, sd = cm_ref[rows, :], sm_ref[rows, :], cd_ref[rows, :], sd_ref[rows, :]
        ckv = _rms(z[:, Z_CKV:Z_KR], gkv_ref[...]).astype(BF16)
        qf = _bdot(_rms(z[:, 0:Z_CKV], gq_ref[...]), wuq_ref[...])
        kf = jnp.dot(ckv, wuk_ref[...], preferred_element_type=F32)
        _store_vt_with_ones(vt_ref, rows, lax.dot_general(wuvt_ref[...], ckv, NT_DIMS,
                                                          preferred_element_type=F32))
        for hd in range(MLA_HEADS):
            sl = slice(hd * LANES, (hd + 1) * LANES)
            q_ref[rows, sl] = _norm_rope_block(qf[:, sl], gs96_ref[...], 1.0 / MLA_DIM, qn_ref[...],
                                               cm, sm, q_scale).astype(BF16)
            k_ref[rows, sl] = _norm_rope_block(kf[:, sl] + kr, gs96_ref[...], 1.0 / MLA_DIM, kn_ref[...],
                                               cm, sm, 1.0).astype(BF16)
        for b in range(DIFF_W // LANES):
            sl = slice(b * LANES, (b + 1) * LANES)
            dq_ref[rows, sl] = _norm_rope_block(z[:, Z_DQ + b * LANES:Z_DQ + (b + 1) * LANES],
                                                gs32_ref[...], 1.0 / DIFF_HEAD, dqn_ref[...],
                                                cd, sd, d_scale).astype(BF16)
            dk_ref[rows, sl] = _norm_rope_block(z[:, Z_DK + b * LANES:Z_DK + (b + 1) * LANES],
                                                gs32_ref[...], 1.0 / DIFF_HEAD, dkn_ref[...],
                                                cd, sd, 1.0).astype(BF16)


def _odd_in_kernel(x_ref, g1_ref, wg_ref, wu_ref, wd_ref, gm_ref, win_ref, wvt_ref,
                   qn_ref, kn_ref, gs64_ref, ca_ref, sa_ref,
                   xo_ref, q_ref, k_ref, vt_ref):
    q_scale = C_HEAD ** -0.5 * LOG2E
    xo_ref[...] = _swiglu_half(x_ref[...], g1_ref[...], wg_ref, wu_ref, wd_ref)
    for rows in _row_groups(x_ref.shape[0], TOKEN_SPLIT_IN):
        x = xo_ref[rows, :]
        h = _rms(x, gm_ref[...]).astype(BF16)
        z = jnp.dot(h, win_ref[...], preferred_element_type=F32)
        _store_vt_with_ones(vt_ref, rows, lax.dot_general(wvt_ref[...], h, NT_DIMS,
                                                          preferred_element_type=F32))
        ca, sa = ca_ref[rows, :], sa_ref[rows, :]
        for b in range(C_Q_W // LANES):
            y = _norm_rope_block(z[:, b * LANES:(b + 1) * LANES], gs64_ref[...], 1.0 / C_HEAD,
                                 qn_ref[...], ca, sa, q_scale)
            hi, lo = _fp8_split(y)
            q_ref[rows, 2 * b * LANES:(2 * b + 1) * LANES] = hi.astype(FP8)
            q_ref[rows, (2 * b + 1) * LANES:(2 * b + 2) * LANES] = lo.astype(FP8)
        for b in range(C_KV_W // LANES):
            y = _norm_rope_block(z[:, C_Q_W + b * LANES:C_Q_W + (b + 1) * LANES], gs64_ref[...],
                                 1.0 / C_HEAD, kn_ref[...], ca, sa, 1.0)
            for e, op in enumerate(_pair_operands(*_fp8_split(y), is_query=False)):
                k_ref[rows, (2 * b + e) * 2 * LANES:(2 * b + e + 1) * 2 * LANES] = op


def _even_out_kernel(x_ref, oa_ref, ob_ref, woa_ref, wob_ref, g2_ref, wg_ref, wu_ref, wd_ref, xo_ref):
    for rows in _row_groups(x_ref.shape[0], TOKEN_SPLIT_OUT):
        x = x_ref[rows, :]
        x = x + jnp.dot(oa_ref[rows, :], woa_ref[...], preferred_element_type=F32)
        x = x + jnp.dot(ob_ref[rows, :], wob_ref[...], preferred_element_type=F32)
        xo_ref[rows, :] = _swiglu_half(x, g2_ref[...], wg_ref, wu_ref, wd_ref)


def _odd_out_kernel(x_ref, o_ref, wo_ref, g2_ref, wg_ref, wu_ref, wd_ref, xo_ref):
    for rows in _row_groups(x_ref.shape[0], TOKEN_SPLIT_OUT):
        x = x_ref[rows, :] + jnp.dot(o_ref[rows, :], wo_ref[...], preferred_element_type=F32)
        xo_ref[rows, :] = _swiglu_half(x, g2_ref[...], wg_ref, wu_ref, wd_ref)


def _pair_rows(ot0, ot1):
    return jnp.concatenate([ot0, ot1], axis=0).T


def _out_blocks(o_ref):
    return range(o_ref.shape[1] // LANES)


def _mla_attn_kernel(bounded, q_ref, k_ref, vt_ref, o_ref):
    for s in _out_blocks(o_ref):
        outs = []
        for hd in (2 * s, 2 * s + 1):
            sl = slice(hd * LANES, (hd + 1) * LANES)
            outs.append(_softmax_pv_t(q_ref[:, sl], k_ref[:, sl],
                                      vt_ref[hd * VT_ROWS:(hd + 1) * VT_ROWS, :], bounded))
        o_ref[:, s * LANES:(s + 1) * LANES] = _pair_rows(*outs).astype(BF16)


def _gqa_attn_kernel(bounded, q_ref, k_ref, vt_ref, o_ref):
    w = 2 * LANES
    for s in _out_blocks(o_ref):
        hi = q_ref[:, s * w:s * w + LANES].astype(F32)
        lo = q_ref[:, s * w + LANES:(s + 1) * w].astype(F32)
        outs = []
        for e, q3 in enumerate(_pair_operands(hi, lo, is_query=True)):
            outs.append(_softmax_pv_t(q3, k_ref[:, e * w:(e + 1) * w],
                                      vt_ref[e * VT_ROWS:(e + 1) * VT_ROWS, :], bounded))
        o_ref[:, s * LANES:(s + 1) * LANES] = _pair_rows(*outs).astype(BF16)


MXU_TILE = 256
PIPE_LAG = 2
S_ADDR = (0, 64, 128)
O_ADDR = (192, 224)
PV_ROWS = HEAD_V + 16
DIFF_WIN_HEADS = MXU_TILE // (2 * DIFF_HEAD)


def _mxu_softmax_pv(maps, k_ref, vt_ref):
    n_k = k_ref.shape[0] // MXU_TILE
    for mxu in (0, 1):
        for a in S_ADDR:
            pltpu.matmul_pop(a, (MXU_TILE, MXU_TILE), F32, mxu)
        for a in O_ADDR:
            pltpu.matmul_pop(a, (PV_ROWS, MXU_TILE), F32, mxu)
    units = {m: [(g, j) for g in range(len(maps[m])) for j in range(n_k)] for m in (0, 1)}
    n_units = len(units[0])
    o_lag = PIPE_LAG + 2
    results = {}
    for i in range(n_units + o_lag):
        for mxu in (0, 1):
            if i < n_units:
                g, j = units[mxu][i]
                _, wq, k_col, _ = maps[mxu][g]
                pltpu.matmul_push_rhs(wq, staging_register=0, mxu_index=mxu)
                pltpu.matmul_acc_lhs(S_ADDR[i % 3],
                                     k_ref[j * MXU_TILE:(j + 1) * MXU_TILE, k_col:k_col + MXU_TILE],
                                     mxu, load_staged_rhs=0)
            if PIPE_LAG <= i < n_units + PIPE_LAG:
                g, j = units[mxu][i - PIPE_LAG]
                v_row = maps[mxu][g][3]
                st = pltpu.matmul_pop(S_ADDR[(i - PIPE_LAG) % 3], (MXU_TILE, MXU_TILE), F32, mxu)
                pltpu.matmul_push_rhs(jnp.exp2(st).astype(BF16), staging_register=1, mxu_index=mxu)
                v_rows = vt_ref[v_row:v_row + PV_ROWS, j * MXU_TILE:(j + 1) * MXU_TILE]
                pltpu.matmul_acc_lhs(O_ADDR[g % 2], v_rows, mxu, load_staged_rhs=1)
            if i >= o_lag:
                g, j = units[mxu][i - o_lag]
                if j == n_k - 1:
                    ot = pltpu.matmul_pop(O_ADDR[g % 2], (PV_ROWS, MXU_TILE), F32, mxu)
                    results[maps[mxu][g][0]] = ot[:HEAD_V] / ot[HEAD_V:HEAD_V + 1]
    return results


def _transposed_tile(q, keep, dtype):
    if keep is not None:
        q = jnp.where(keep, q, jnp.zeros_like(q))
    return q.astype(F32).T.astype(dtype)


def _gqa_attn_mxu_kernel(q_ref, k_ref, vt_ref, o_ref):
    n_q = q_ref.shape[0] // MXU_TILE
    zeros = jnp.zeros((C_HEAD, MXU_TILE), F32)
    tiles = {}
    for s in _out_blocks(o_ref):
        for n in range(n_q):
            t = q_ref[n * MXU_TILE:(n + 1) * MXU_TILE, s * MXU_TILE:(s + 1) * MXU_TILE].astype(F32).T
            for e in range(2):
                h, l = t[e * C_HEAD:(e + 1) * C_HEAD], t[LANES + e * C_HEAD:LANES + (e + 1) * C_HEAD]
                tiles[(s, e, n)] = jnp.concatenate([h, l, h, zeros], axis=0).astype(FP8)
    maps = {0: [], 1: []}
    for s in _out_blocks(o_ref):
        for e in range(2):
            for n in range(n_q):
                maps[n % 2].append(((s, e, n), tiles[(s, e, n)], e * MXU_TILE, e * VT_ROWS))
    res = _mxu_softmax_pv(maps, k_ref, vt_ref)
    for s in _out_blocks(o_ref):
        rows = [jnp.concatenate([res[(s, e, n)] for n in range(n_q)], axis=1) for e in range(2)]
        o_ref[:, s * LANES:(s + 1) * LANES] = _pair_rows(*rows).astype(BF16)


def _mla_attn_mxu_kernel(q_ref, k_ref, vt_ref, o_ref):
    n_q = q_ref.shape[0] // MXU_TILE
    lane = lax.broadcasted_iota(jnp.int32, (MXU_TILE, MXU_TILE), 1)
    maps = {0: [], 1: []}
    for s in _out_blocks(o_ref):
        for e in range(2):
            keep = (lane >= e * LANES) & (lane < (e + 1) * LANES)
            for n in range(n_q):
                q = q_ref[n * MXU_TILE:(n + 1) * MXU_TILE, s * MXU_TILE:(s + 1) * MXU_TILE]
                maps[n % 2].append(((s, e, n), _transposed_tile(q, keep, BF16), s * MXU_TILE,
                                    (2 * s + e) * VT_ROWS))
    res = _mxu_softmax_pv(maps, k_ref, vt_ref)
    for s in _out_blocks(o_ref):
        rows = [jnp.concatenate([res[(s, e, n)] for n in range(n_q)], axis=1) for e in range(2)]
        o_ref[:, s * LANES:(s + 1) * LANES] = _pair_rows(*rows).astype(BF16)


def _diff_attn_mxu_kernel(lam_init, q_ref, k_ref, vt_ref, lp_ref, sub_ref, o_ref):
    lp = lp_ref[...]
    lam = (jnp.exp(jnp.sum(lp[0:1] * lp[1:2], axis=-1, keepdims=True))
           - jnp.exp(jnp.sum(lp[2:3] * lp[3:4], axis=-1, keepdims=True)) + lam_init)
    n_q = q_ref.shape[0] // MXU_TILE
    lane = lax.broadcasted_iota(jnp.int32, (MXU_TILE, MXU_TILE), 1)
    n_win = o_ref.shape[1] // MXU_TILE
    maps = {0: [], 1: []}
    for w in range(n_win):
        for hq in range(DIFF_WIN_HEADS):
            for t in range(2):
                lo_lane = hq * 2 * DIFF_HEAD + t * DIFF_HEAD
                keep = (lane >= lo_lane) & (lane < lo_lane + DIFF_HEAD)
                for n in range(n_q):
                    q = q_ref[n * MXU_TILE:(n + 1) * MXU_TILE, w * MXU_TILE:(w + 1) * MXU_TILE]
                    maps[n % 2].append(((w, hq, t, n), _transposed_tile(q, keep, BF16), w * MXU_TILE,
                                        (DIFF_WIN_HEADS * w + hq) * VT_ROWS))
    res = _mxu_softmax_pv(maps, k_ref, vt_ref)
    lane128 = _lane_iota(q_ref.shape[0])
    lo = lane128 < 2 * DIFF_HEAD
    inv = 1.0 / (2 * DIFF_HEAD)
    for s in _out_blocks(o_ref):
        w, b = divmod(s, 2)
        heads = []
        for hq in (2 * b, 2 * b + 1):
            a1 = jnp.concatenate([res[(w, hq, 0, n)] for n in range(n_q)], axis=1)
            a2 = jnp.concatenate([res[(w, hq, 1, n)] for n in range(n_q)], axis=1)
            heads.append(a1 - lam * a2)
        o = _pair_rows(*heads)
        o2 = o * o
        r0 = lax.rsqrt(jnp.sum(jnp.where(lo, o2, 0.0), axis=-1, keepdims=True) * inv + EPS)
        r1 = lax.rsqrt(jnp.sum(jnp.where(lo, 0.0, o2), axis=-1, keepdims=True) * inv + EPS)
        o = o * jnp.where(lo, r0, r1) * sub_ref[...] * (1.0 - lam_init)
        o_ref[:, s * LANES:(s + 1) * LANES] = o.astype(BF16)


def _diff_attn_kernel(lam_init, bounded, q_ref, k_ref, vt_ref, lp_ref, sub_ref, o_ref):
    lp = lp_ref[...]
    lam = (jnp.exp(jnp.sum(lp[0:1] * lp[1:2], axis=-1, keepdims=True))
           - jnp.exp(jnp.sum(lp[2:3] * lp[3:4], axis=-1, keepdims=True)) + lam_init)
    lane = _lane_iota(q_ref.shape[0])
    lo = lane < 2 * DIFF_HEAD
    inv = 1.0 / (2 * DIFF_HEAD)
    for s in _out_blocks(o_ref):
        sl = slice(s * LANES, (s + 1) * LANES)
        q, k = q_ref[:, sl], k_ref[:, sl]
        zero = jnp.zeros_like(q)
        outs = []
        for hd in range(2):
            base = hd * 2 * DIFF_HEAD
            m1 = (lane >= base) & (lane < base + DIFF_HEAD)
            m2 = (lane >= base + DIFF_HEAD) & (lane < base + 2 * DIFF_HEAD)
            vt = vt_ref[(2 * s + hd) * VT_ROWS:(2 * s + hd + 1) * VT_ROWS, :]
            a1 = _softmax_pv_t(jnp.where(m1, q, zero), k, vt, bounded)
            a2 = _softmax_pv_t(jnp.where(m2, q, zero), k, vt, bounded)
            outs.append(a1 - lam * a2)
        o = _pair_rows(outs[0], outs[1])
        o2 = o * o
        r0 = lax.rsqrt(jnp.sum(jnp.where(lo, o2, 0.0), axis=-1, keepdims=True) * inv + EPS)
        r1 = lax.rsqrt(jnp.sum(jnp.where(lo, 0.0, o2), axis=-1, keepdims=True) * inv + EPS)
        o = o * jnp.where(lo, r0, r1) * sub_ref[...] * (1.0 - lam_init)
        o_ref[:, sl] = o.astype(BF16)


def _tile(n, want):
    return want if n % want == 0 else n


def _const_spec(shape):
    nd = len(shape)
    return pl.BlockSpec(shape, lambda *_: (0,) * nd, pipeline_mode=pl.Buffered(1))


def _tok_spec(tm, width):
    return pl.BlockSpec((None, tm, width), lambda b, t: (b, t, 0))


def _tok_t_spec(tm, width):
    return pl.BlockSpec((None, width, tm), lambda b, t: (b, 0, t))


def _tab_spec(tm):
    return pl.BlockSpec((tm, LANES), lambda b, t: (t, 0))


def _token_call(body, name, x, consts, tabs, extra_tok, outs, tm):
    bsz, seq, _ = x.shape
    tok_inputs = [x] + list(extra_tok)
    in_specs = ([_tok_spec(tm, a.shape[-1]) for a in tok_inputs]
                + [_const_spec(c.shape) for c in consts]
                + [_tab_spec(tm) for _ in tabs])
    out_shape = [jax.ShapeDtypeStruct((bsz, w, seq) if tr else (bsz, seq, w), d) for w, d, tr in outs]
    out_specs = [_tok_t_spec(tm, w) if tr else _tok_spec(tm, w) for w, d, tr in outs]
    return pl.pallas_call(
        body, name=name,
        grid=(bsz, seq // tm),
        in_specs=in_specs, out_specs=out_specs, out_shape=out_shape,
        compiler_params=pltpu.CompilerParams(
            dimension_semantics=("arbitrary", "arbitrary"), vmem_limit_bytes=VMEM_LIMIT),
    )(*tok_inputs, *consts, *tabs)


def _attn_dispatch(body, name, logit_bound, *args):
    q, k, vt, extra, *static, mxu_body = args

    def run(bounded, q, k, vt, extra):
        tag = "bounded" if bounded else "shifted"
        tq = _tile(q.shape[1], QUERY_TILE_BOUNDED if bounded else QUERY_TILE_SHIFTED)
        n_sub = OUT_BLOCKS_BOUNDED if bounded else OUT_BLOCKS_SHIFTED
        if bounded and mxu_body is not None and q.shape[1] % QUERY_TILE_MXU == 0:
            return _attn_call(mxu_body[0], f"{name}_{tag}", q, k, vt, extra, *static,
                              QUERY_TILE_MXU, mxu_body[1])
        return _attn_call(functools.partial(body, bounded), f"{name}_{tag}", q, k, vt, extra,
                          *static, tq, n_sub)

    return lax.cond(logit_bound <= LOGIT_BOUND, functools.partial(run, True),
                    functools.partial(run, False), q, k, vt, tuple(extra))


def _attn_call(body, name, q, k, vt, extra, q_blk, k_blk, n_blocks, kv_share, tq, n_sub):
    bsz, seq, _ = q.shape
    if kv_share == 1:
        k_spec = pl.BlockSpec((None, seq, k_blk * n_sub), lambda b, h, t: (b, 0, h))
        vt_spec = pl.BlockSpec((None, 2 * VT_ROWS * n_sub, seq), lambda b, h, t: (b, h, 0))
    else:
        assert kv_share % n_sub == 0
        k_spec = pl.BlockSpec((None, seq, k_blk), lambda b, h, t: (b, 0, (h * n_sub) // kv_share))
        vt_spec = pl.BlockSpec((None, 2 * VT_ROWS, seq), lambda b, h, t: (b, (h * n_sub) // kv_share, 0))
    in_specs = [pl.BlockSpec((None, tq, q_blk * n_sub), lambda b, h, t: (b, t, h)), k_spec, vt_spec
                ] + [pl.BlockSpec(e.shape, lambda b, h, t: (0, 0)) for e in extra]
    return pl.pallas_call(
        body, name=name,
        grid=(bsz, n_blocks // n_sub, seq // tq),
        in_specs=in_specs,
        out_specs=pl.BlockSpec((None, tq, LANES * n_sub), lambda b, h, t: (b, t, h)),
        out_shape=jax.ShapeDtypeStruct((bsz, seq, n_blocks * LANES), BF16),
        compiler_params=pltpu.CompilerParams(
            dimension_semantics=("arbitrary", "arbitrary", "arbitrary"), vmem_limit_bytes=VMEM_LIMIT),
    )(q, k, vt, *extra)


def _rope_lane_tables(pos_a, pos_b):
    half = ROPE_DIM // 2
    inv = ROPE_THETA ** (-jnp.arange(0, ROPE_DIM, 2, dtype=F32) / ROPE_DIM)
    lane = np.arange(LANES)
    sign = jnp.asarray(np.where(lane % ROPE_DIM < half, -1.0, 1.0), F32)
    use_b = jnp.asarray((lane // ROPE_DIM) % 2 == 1)
    freq = inv[lane % half]
    ang_a = pos_a.astype(F32)[:, None] * freq[None, :]
    ang_b = pos_b.astype(F32)[:, None] * freq[None, :]
    cos = jnp.where(use_b[None, :], jnp.cos(ang_b), jnp.cos(ang_a))
    sin = jnp.where(use_b[None, :], jnp.sin(ang_b), jnp.sin(ang_a)) * sign[None, :]
    return cos, sin


def _group_sum_matrix(group):
    lane = np.arange(LANES)
    g = (lane[:, None] // group == lane[None, :] // group).astype(np.float32)
    return jnp.asarray(np.concatenate([g, g], axis=0), BF16)


def _row(v):
    return v.reshape(1, -1).astype(F32)


def _max_abs(v):
    return jnp.max(jnp.abs(v.astype(F32)))


def kernel(x, ffn1_norm, ffn1_w_gate, ffn1_w_up, ffn1_w_down, mix_norm, ffn2_norm, ffn2_w_gate, ffn2_w_up, ffn2_w_down, ab_w_in, mla_q_lora_norm, mla_w_uq, mla_kv_lora_norm, mla_w_ukv, mla_q_norm, mla_k_norm, diff_q_norm, diff_k_norm, diff_lambda_q1, diff_lambda_k1, diff_lambda_q2, diff_lambda_k2, diff_subln, ab_w_out, c_w_in, c_q_norm, c_k_norm, c_w_out):
    bsz, seq, d = x.shape
    depth = ffn1_norm.shape[0]
    tm = _tile(seq, TOKEN_TILE)

    pos = jnp.arange(seq, dtype=jnp.int32)
    cos_t, sin_t = _rope_lane_tables(pos, pos)
    cos_ax, sin_ax = _rope_lane_tables(pos // GRID_W, pos % GRID_W)
    lane = np.arange(LANES)
    mla_rot = jnp.asarray((lane >= MLA_NOPE) & (lane < MLA_NOPE + MLA_ROPE))
    cos_m = jnp.where(mla_rot[None, :], cos_t, 1.0)
    sin_m = jnp.where(mla_rot[None, :], sin_t, 0.0)
    gs_all, gs64, gs32 = _group_sum_matrix(LANES), _group_sum_matrix(C_HEAD), _group_sum_matrix(DIFF_HEAD)

    for i in range(depth):
        j = i // 2
        ffn1 = (_row(ffn1_norm[i]), ffn1_w_gate[i].astype(BF16), ffn1_w_up[i].astype(BF16),
                ffn1_w_down[i].astype(BF16))
        ffn2 = (_row(ffn2_norm[i]), ffn2_w_gate[i].astype(BF16), ffn2_w_up[i].astype(BF16),
                ffn2_w_down[i].astype(BF16))
        if i % 2 == 0:
            lam_init = 0.8 - 0.6 * math.exp(-0.3 * i)
            w = ab_w_in[j].astype(BF16)
            o_kr = MLA_Q_RANK + MLA_KV_RANK
            o_dq = o_kr + MLA_ROPE
            o_dv = o_dq + 2 * DIFF_W
            kr_blk = jnp.pad(w[:, o_kr:o_dq], ((0, 0), (MLA_NOPE, LANES - MLA_DIM)))
            w_in = jnp.concatenate([w[:, 0:o_kr], kr_blk, w[:, o_dq:o_dv]], axis=1)
            w_vt = w[:, o_dv:].T
            w_uq = jnp.pad(mla_w_uq[j].reshape(MLA_Q_RANK, MLA_HEADS, MLA_DIM),
                           ((0, 0), (0, 0), (0, LANES - MLA_DIM))
                           ).reshape(MLA_Q_RANK, MLA_HEADS * LANES).astype(BF16)
            ukv = mla_w_ukv[j].reshape(MLA_KV_RANK, MLA_HEADS, MLA_NOPE + MLA_V)
            w_uk = jnp.pad(ukv[:, :, :MLA_NOPE], ((0, 0), (0, 0), (0, LANES - MLA_NOPE))
                           ).reshape(MLA_KV_RANK, MLA_HEADS * LANES).astype(BF16)
            w_uvt = ukv[:, :, MLA_NOPE:].reshape(MLA_KV_RANK, MLA_HEADS * MLA_V).T.astype(BF16)
            pad96 = (0, LANES - MLA_DIM)
            consts = ffn1 + (_row(mix_norm[i]), w_in, w_vt,
                             _row(mla_q_lora_norm[j]), w_uq, _row(mla_kv_lora_norm[j]), w_uk, w_uvt,
                             _row(jnp.pad(mla_q_norm[j], pad96)), _row(jnp.pad(mla_k_norm[j], pad96)),
                             _row(jnp.tile(diff_q_norm[j], LANES // DIFF_HEAD)),
                             _row(jnp.tile(diff_k_norm[j], LANES // DIFF_HEAD)),
                             gs_all, gs32)
            x, q, k, vt, dq, dk, dvt = _token_call(
                _even_in_kernel, f"even_in_{i}", x, consts, (cos_m, sin_m, cos_t, sin_t), (),
                ((d, F32, False), (MLA_HEADS * LANES, BF16, False), (MLA_HEADS * LANES, BF16, False),
                 (VT_ROWS * MLA_HEADS, BF16, True), (DIFF_W, BF16, False), (DIFF_W, BF16, False),
                 (VT_ROWS * DIFF_HEADS, BF16, True)), tm)
            mla_bound = (MLA_DIM ** 0.5 * LOG2E) * _max_abs(mla_q_norm[j]) * _max_abs(mla_k_norm[j])
            o_mla = _attn_dispatch(_mla_attn_kernel, f"mla_attn_{i}", mla_bound, q, k, vt, (),
                                   2 * LANES, 2 * LANES, MLA_HEADS // 2, 1, (_mla_attn_mxu_kernel, MXU_BLOCKS_MLA))
            lam_p = jnp.stack([diff_lambda_q1[j], diff_lambda_k1[j],
                               diff_lambda_q2[j], diff_lambda_k2[j]]).astype(F32)
            sub = _row(jnp.tile(diff_subln[j], 2))
            diff_bound = (DIFF_HEAD ** 0.5 * LOG2E) * _max_abs(diff_q_norm[j]) * _max_abs(diff_k_norm[j])
            o_diff = _attn_dispatch(functools.partial(_diff_attn_kernel, lam_init), f"diff_attn_{i}",
                                    diff_bound, dq, dk, dvt, (lam_p, sub),
                                    LANES, LANES, DIFF_HEADS // 2, 1,
                                    (functools.partial(_diff_attn_mxu_kernel, lam_init), MXU_BLOCKS_DIFF))
            wo = ab_w_out[j].astype(BF16)
            consts = (wo[:MLA_HEADS * MLA_V], wo[MLA_HEADS * MLA_V:]) + ffn2
            (x,) = _token_call(_even_out_kernel, f"even_out_{i}", x, consts, (), (o_mla, o_diff),
                               ((d, F32, False),), tm)
        else:
            w = c_w_in[j].astype(BF16)
            wq = w[:, :C_Q_W].reshape(d, 2, 2, C_GROUPS, C_HEAD)
            wq = wq.transpose(0, 1, 3, 2, 4).reshape(d, C_Q_W)
            w_in = jnp.concatenate([wq, w[:, C_Q_W:C_Q_W + C_KV_W]], axis=1)
            w_vt = w[:, C_Q_W + C_KV_W:].T
            consts = ffn1 + (_row(mix_norm[i]), w_in, w_vt,
                             _row(jnp.tile(c_q_norm[j], 2)), _row(jnp.tile(c_k_norm[j], 2)), gs64)
            x, q, k, vt = _token_call(
                _odd_in_kernel, f"odd_in_{i}", x, consts, (cos_ax, sin_ax), (),
                ((d, F32, False), (2 * C_Q_W, FP8, False), (C_KV_HEADS * 2 * LANES, FP8, False),
                 (VT_ROWS * C_KV_HEADS, BF16, True)), tm)
            gqa_bound = (C_HEAD ** 0.5 * LOG2E) * _max_abs(c_q_norm[j]) * _max_abs(c_k_norm[j])
            o = _attn_dispatch(_gqa_attn_kernel, f"gqa_attn_{i}", gqa_bound, q, k, vt, (),
                               2 * LANES, 4 * LANES, C_HEADS // 2, C_GROUPS, (_gqa_attn_mxu_kernel, MXU_BLOCKS_GQA))
            wo = c_w_out[j].astype(BF16).reshape(2, 2, C_GROUPS, C_HEAD, d).transpose(0, 2, 1, 3, 4)
            wo = wo.reshape(C_Q_W, d)
            consts = (wo,) + ffn2
            (x,) = _token_call(_odd_out_kernel, f"odd_out_{i}", x, consts, (), (o,),
                               ((d, F32, False),), tm)
    return x
```

```python
import functools
import math

import jax
import jax.numpy as jnp
import numpy as np
from jax import lax
from jax.experimental import pallas as pl
from jax.experimental.pallas import tpu as pltpu

F32 = jnp.float32
BF16 = jnp.bfloat16
FP8 = jnp.float8_e4m3fn

LANES = 128
EPS = 1e-6
ROPE_THETA = 10000.0
GRID_W = 64
LOG2E = math.log2(math.e)

MLA_HEADS, MLA_Q_RANK, MLA_KV_RANK = 8, 256, 128
MLA_NOPE, MLA_ROPE, MLA_V = 64, 32, 64
MLA_DIM = MLA_NOPE + MLA_ROPE
DIFF_HEADS, DIFF_HEAD = 8, 32
DIFF_W = DIFF_HEADS * 2 * DIFF_HEAD
C_HEADS, C_KV_HEADS, C_HEAD = 16, 4, 64
C_GROUPS = C_HEADS // C_KV_HEADS
C_Q_W, C_KV_W = C_HEADS * C_HEAD, C_KV_HEADS * C_HEAD
ROPE_DIM = 32

Z_CKV = MLA_Q_RANK
Z_KR = Z_CKV + MLA_KV_RANK
Z_DQ = Z_KR + LANES
Z_DK = Z_DQ + DIFF_W

VMEM_LIMIT = 56 * 1024 * 1024
TOKEN_TILE = 512
FFN_CHUNKS = 11
FFN_CHUNK_ALIGN = 256
TOKEN_SPLIT_IN = 2
TOKEN_SPLIT_OUT = 1
TOKEN_TILE_OUT = 1024
TOKEN_TILE_ODD_IN = 1024
QUERY_TILE_BOUNDED = 1024
QUERY_TILE_SHIFTED = 512
OUT_BLOCKS_BOUNDED = 2
OUT_BLOCKS_SHIFTED = 1
QUERY_TILE_MXU = 1024
MXU_BLOCKS_MLA, MXU_BLOCKS_DIFF, MXU_BLOCKS_GQA = 2, 2, 4
NT_DIMS = (((1,), (1,)), ((), ()))
LOGIT_BOUND = 64.0
HEAD_V = 64
ONES_ROWS = 64
VT_ROWS = HEAD_V + ONES_ROWS


def _rms(x, g):
    ms = jnp.mean(x * x, axis=-1, keepdims=True)
    return x * lax.rsqrt(ms + EPS) * g


def _bdot(a, b):
    return jnp.dot(a.astype(BF16), b, preferred_element_type=F32)


def _swiglu_half(x, g, wg, wu, wd):
    h = _rms(x, g).astype(BF16)
    d_ff = wg.shape[1]
    bounds = [round(i * d_ff / FFN_CHUNKS / FFN_CHUNK_ALIGN) * FFN_CHUNK_ALIGN for i in range(FFN_CHUNKS)] + [d_ff]
    acc = None
    for c0, c1 in zip(bounds[:-1], bounds[1:]):
        gate = jnp.dot(h, wg[:, c0:c1], preferred_element_type=F32)
        up = jnp.dot(h, wu[:, c0:c1], preferred_element_type=F32)
        act = (gate / (1.0 + jnp.exp(-gate))) * up
        part = jnp.dot(act.astype(BF16), wd[c0:c1, :], preferred_element_type=F32)
        acc = part if acc is None else acc + part
    return x + 0.5 * acc


def _lane_iota(rows):
    return lax.broadcasted_iota(jnp.int32, (rows, LANES), 1)


def _norm_rope_block(x, gsum, inv_count, gain, cos, sin, post_scale):
    lane = _lane_iota(x.shape[0])
    x2 = x * x
    hi = x2.astype(BF16)
    lo = (x2 - hi.astype(F32)).astype(BF16)
    ss = jnp.dot(jnp.concatenate([hi, lo], axis=1), gsum, preferred_element_type=F32)
    y = x * lax.rsqrt(ss * inv_count + EPS) * gain
    up = pltpu.roll(y, LANES - ROPE_DIM // 2, axis=1)
    dn = pltpu.roll(y, ROPE_DIM // 2, axis=1)
    partner = jnp.where((lane % ROPE_DIM) < ROPE_DIM // 2, up, dn)
    y = y * cos + partner * sin
    if post_scale != 1.0:
        y = y * post_scale
    return y


def _softmax_pv_t(qm, k, vt, bounded):
    st = lax.dot_general(k, qm, NT_DIMS, preferred_element_type=F32)
    if not bounded:
        st = st - jnp.max(st, axis=0, keepdims=True)
    ot = jnp.dot(vt, jnp.exp2(st).astype(BF16), preferred_element_type=F32)
    return ot[:HEAD_V] / ot[HEAD_V:HEAD_V + 1]


def _store_vt_with_ones(vt_ref, cols, vt):
    ones = jnp.ones((ONES_ROWS, vt.shape[1]), BF16)
    for hd in range(vt.shape[0] // HEAD_V):
        vt_ref[hd * VT_ROWS:hd * VT_ROWS + HEAD_V, cols] = vt[hd * HEAD_V:(hd + 1) * HEAD_V].astype(BF16)
        vt_ref[hd * VT_ROWS + HEAD_V:(hd + 1) * VT_ROWS, cols] = ones


def _fp8_split(y):
    hi = y.astype(FP8).astype(F32)
    return hi, (y - hi).astype(FP8).astype(F32)


def _pair_operands(hi, lo, is_query):
    first = _lane_iota(hi.shape[0]) < C_HEAD
    hi_sw, lo_sw = pltpu.roll(hi, C_HEAD, axis=1), pltpu.roll(lo, C_HEAD, axis=1)
    zero = jnp.zeros_like(hi)
    out = []
    for h_own, h_oth, l_own, l_oth in ((hi, hi_sw, lo, lo_sw), (hi_sw, hi, lo_sw, lo)):
        if is_query:
            a, b = jnp.where(first, h_own, l_oth), jnp.where(first, h_own, zero)
        else:
            a, b = jnp.where(first, h_own, h_oth), jnp.where(first, l_own, zero)
        out.append(jnp.concatenate([a, b], axis=1).astype(FP8))
    return out


def _row_groups(n_rows, split):
    step = n_rows // split
    return [slice(r * step, (r + 1) * step) for r in range(split)]


def _even_in_kernel(x_ref, g1_ref, wg_ref, wu_ref, wd_ref, gm_ref, win_ref, wvt_ref,
                    gq_ref, wuq_ref, gkv_ref, wuk_ref, wuvt_ref, qn_ref, kn_ref, dqn_ref, dkn_ref,
                    gs96_ref, gs32_ref, cm_ref, sm_ref, cd_ref, sd_ref,
                    xo_ref, q_ref, k_ref, vt_ref, dq_ref, dk_ref, dvt_ref):
    q_scale = MLA_DIM ** -0.5 * LOG2E
    d_scale = DIFF_HEAD ** -0.5 * LOG2E
    xo_ref[...] = _swiglu_half(x_ref[...], g1_ref[...], wg_ref, wu_ref, wd_ref)
    for rows in _row_groups(x_ref.shape[0], TOKEN_SPLIT_IN):
        x = xo_ref[rows, :]
        h = _rms(x, gm_ref[...]).astype(BF16)
        z = jnp.dot(h, win_ref[...], preferred_element_type=F32)
        _store_vt_with_ones(dvt_ref, rows, lax.dot_general(wvt_ref[...], h, NT_DIMS,
                                                           preferred_element_type=F32))
        kr = z[:, Z_KR:Z_DQ]
        cm, sm, cd, sd = cm_ref[rows, :], sm_ref[rows, :], cd_ref[rows, :], sd_ref[rows, :]
        ckv = _rms(z[:, Z_CKV:Z_KR], gkv_ref[...]).astype(BF16)
        qf = _bdot(_rms(z[:, 0:Z_CKV], gq_ref[...]), wuq_ref[...])
        kf = jnp.dot(ckv, wuk_ref[...], preferred_element_type=F32)
        _store_vt_with_ones(vt_ref, rows, lax.dot_general(wuvt_ref[...], ckv, NT_DIMS,
                                                          preferred_element_type=F32))
        for hd in range(MLA_HEADS):
            sl = slice(hd * LANES, (hd + 1) * LANES)
            q_ref[rows, sl] = _norm_rope_block(qf[:, sl], gs96_ref[...], 1.0 / MLA_DIM, qn_ref[...],
                                               cm, sm, q_scale).astype(BF16)
            k_ref[rows, sl] = _norm_rope_block(kf[:, sl] + kr, gs96_ref[...], 1.0 / MLA_DIM, kn_ref[...],
                                               cm, sm, 1.0).astype(BF16)
        for b in range(DIFF_W // LANES):
            sl = slice(b * LANES, (b + 1) * LANES)
            dq_ref[rows, sl] = _norm_rope_block(z[:, Z_DQ + b * LANES:Z_DQ + (b + 1) * LANES],
                                                gs32_ref[...], 1.0 / DIFF_HEAD, dqn_ref[...],
                                                cd, sd, d_scale).astype(BF16)
            dk_ref[rows, sl] = _norm_rope_block(z[:, Z_DK + b * LANES:Z_DK + (b + 1) * LANES],
                                                gs32_ref[...], 1.0 / DIFF_HEAD, dkn_ref[...],
                                                cd, sd, 1.0).astype(BF16)


def _odd_in_kernel(x_ref, g1_ref, wg_ref, wu_ref, wd_ref, gm_ref, win_ref, wvt_ref,
                   qn_ref, kn_ref, gs64_ref, ca_ref, sa_ref,
                   xo_ref, q_ref, k_ref, vt_ref):
    q_scale = C_HEAD ** -0.5 * LOG2E
    xo_ref[...] = _swiglu_half(x_ref[...], g1_ref[...], wg_ref, wu_ref, wd_ref)
    for rows in _row_groups(x_ref.shape[0], TOKEN_SPLIT_IN):
        x = xo_ref[rows, :]
        h = _rms(x, gm_ref[...]).astype(BF16)
        z = jnp.dot(h, win_ref[...], preferred_element_type=F32)
        _store_vt_with_ones(vt_ref, rows, lax.dot_general(wvt_ref[...], h, NT_DIMS,
                                                          preferred_element_type=F32))
        ca, sa = ca_ref[rows, :], sa_ref[rows, :]
        for b in range(C_Q_W // LANES):
            y = _norm_rope_block(z[:, b * LANES:(b + 1) * LANES], gs64_ref[...], 1.0 / C_HEAD,
                                 qn_ref[...], ca, sa, q_scale)
            hi, lo = _fp8_split(y)
            q_ref[rows, 2 * b * LANES:(2 * b + 1) * LANES] = hi.astype(FP8)
            q_ref[rows, (2 * b + 1) * LANES:(2 * b + 2) * LANES] = lo.astype(FP8)
        for b in range(C_KV_W // LANES):
            y = _norm_rope_block(z[:, C_Q_W + b * LANES:C_Q_W + (b + 1) * LANES], gs64_ref[...],
                                 1.0 / C_HEAD, kn_ref[...], ca, sa, 1.0)
            for e, op in enumerate(_pair_operands(*_fp8_split(y), is_query=False)):
                k_ref[rows, (2 * b + e) * 2 * LANES:(2 * b + e + 1) * 2 * LANES] = op


def _even_out_kernel(x_ref, oa_ref, ob_ref, woa_ref, wob_ref, g2_ref, wg_ref, wu_ref, wd_ref, xo_ref):
    for rows in _row_groups(x_ref.shape[0], TOKEN_SPLIT_OUT):
        x = x_ref[rows, :]
        x = x + jnp.dot(oa_ref[rows, :], woa_ref[...], preferred_element_type=F32)
        x = x + jnp.dot(ob_ref[rows, :], wob_ref[...], preferred_element_type=F32)
        xo_ref[rows, :] = _swiglu_half(x, g2_ref[...], wg_ref, wu_ref, wd_ref)


def _odd_out_kernel(x_ref, o_ref, wo_ref, g2_ref, wg_ref, wu_ref, wd_ref, xo_ref):
    for rows in _row_groups(x_ref.shape[0], TOKEN_SPLIT_OUT):
        x = x_ref[rows, :] + jnp.dot(o_ref[rows, :], wo_ref[...], preferred_element_type=F32)
        xo_ref[rows, :] = _swiglu_half(x, g2_ref[...], wg_ref, wu_ref, wd_ref)


def _pair_rows(ot0, ot1):
    return jnp.concatenate([ot0, ot1], axis=0).T


def _out_blocks(o_ref):
    return range(o_ref.shape[1] // LANES)


def _mla_attn_kernel(bounded, q_ref, k_ref, vt_ref, o_ref):
    for s in _out_blocks(o_ref):
        outs = []
        for hd in (2 * s, 2 * s + 1):
            sl = slice(hd * LANES, (hd + 1) * LANES)
            outs.append(_softmax_pv_t(q_ref[:, sl], k_ref[:, sl],
                                      vt_ref[hd * VT_ROWS:(hd + 1) * VT_ROWS, :], bounded))
        o_ref[:, s * LANES:(s + 1) * LANES] = _pair_rows(*outs).astype(BF16)


def _gqa_attn_kernel(bounded, q_ref, k_ref, vt_ref, o_ref):
    w = 2 * LANES
    for s in _out_blocks(o_ref):
        hi = q_ref[:, s * w:s * w + LANES].astype(F32)
        lo = q_ref[:, s * w + LANES:(s + 1) * w].astype(F32)
        outs = []
        for e, q3 in enumerate(_pair_operands(hi, lo, is_query=True)):
            outs.append(_softmax_pv_t(q3, k_ref[:, e * w:(e + 1) * w],
                                      vt_ref[e * VT_ROWS:(e + 1) * VT_ROWS, :], bounded))
        o_ref[:, s * LANES:(s + 1) * LANES] = _pair_rows(*outs).astype(BF16)


MXU_TILE = 256
PIPE_LAG = 2
S_ADDR = (0, 64, 128)
O_ADDR = (192, 224)
PV_ROWS = HEAD_V + 16
DIFF_WIN_HEADS = MXU_TILE // (2 * DIFF_HEAD)


def _mxu_softmax_pv(maps, k_ref, vt_ref):
    n_k = k_ref.shape[0] // MXU_TILE
    for mxu in (0, 1):
        for a in S_ADDR:
            pltpu.matmul_pop(a, (MXU_TILE, MXU_TILE), F32, mxu)
        for a in O_ADDR:
            pltpu.matmul_pop(a, (PV_ROWS, MXU_TILE), F32, mxu)
    units = {m: [(g, j) for g in range(len(maps[m])) for j in range(n_k)] for m in (0, 1)}
    n_units = len(units[0])
    o_lag = PIPE_LAG + 2
    results = {}
    for i in range(n_units + o_lag):
        for mxu in (0, 1):
            if i < n_units:
                g, j = units[mxu][i]
                _, wq, k_col, _ = maps[mxu][g]
                pltpu.matmul_push_rhs(wq, staging_register=0, mxu_index=mxu)
                pltpu.matmul_acc_lhs(S_ADDR[i % 3],
                                     k_ref[j * MXU_TILE:(j + 1) * MXU_TILE, k_col:k_col + MXU_TILE],
                                     mxu, load_staged_rhs=0)
            if PIPE_LAG <= i < n_units + PIPE_LAG:
                g, j = units[mxu][i - PIPE_LAG]
                v_row = maps[mxu][g][3]
                st = pltpu.matmul_pop(S_ADDR[(i - PIPE_LAG) % 3], (MXU_TILE, MXU_TILE), F32, mxu)
                pltpu.matmul_push_rhs(jnp.exp2(st).astype(BF16), staging_register=1, mxu_index=mxu)
                v_rows = vt_ref[v_row:v_row + PV_ROWS, j * MXU_TILE:(j + 1) * MXU_TILE]
                pltpu.matmul_acc_lhs(O_ADDR[g % 2], v_rows, mxu, load_staged_rhs=1)
            if i >= o_lag:
                g, j = units[mxu][i - o_lag]
                if j == n_k - 1:
                    ot = pltpu.matmul_pop(O_ADDR[g % 2], (PV_ROWS, MXU_TILE), F32, mxu)
                    results[maps[mxu][g][0]] = ot[:HEAD_V] / ot[HEAD_V:HEAD_V + 1]
    return results


def _transposed_tile(q, keep, dtype):
    if keep is not None:
        q = jnp.where(keep, q, jnp.zeros_like(q))
    return q.astype(F32).T.astype(dtype)


def _gqa_attn_mxu_kernel(q_ref, k_ref, vt_ref, o_ref):
    n_q = q_ref.shape[0] // MXU_TILE
    zeros = jnp.zeros((C_HEAD, MXU_TILE), F32)
    tiles = {}
    for s in _out_blocks(o_ref):
        for n in range(n_q):
            t = q_ref[n * MXU_TILE:(n + 1) * MXU_TILE, s * MXU_TILE:(s + 1) * MXU_TILE].astype(F32).T
            for e in range(2):
                h, l = t[e * C_HEAD:(e + 1) * C_HEAD], t[LANES + e * C_HEAD:LANES + (e + 1) * C_HEAD]
                tiles[(s, e, n)] = jnp.concatenate([h, l, h, zeros], axis=0).astype(FP8)
    maps = {0: [], 1: []}
    for s in _out_blocks(o_ref):
        for e in range(2):
            for n in range(n_q):
                maps[n % 2].append(((s, e, n), tiles[(s, e, n)], e * MXU_TILE, e * VT_ROWS))
    res = _mxu_softmax_pv(maps, k_ref, vt_ref)
    for s in _out_blocks(o_ref):
        rows = [jnp.concatenate([res[(s, e, n)] for n in range(n_q)], axis=1) for e in range(2)]
        o_ref[:, s * LANES:(s + 1) * LANES] = _pair_rows(*rows).astype(BF16)


def _mla_attn_mxu_kernel(q_ref, k_ref, vt_ref, o_ref):
    n_q = q_ref.shape[0] // MXU_TILE
    lane = lax.broadcasted_iota(jnp.int32, (MXU_TILE, MXU_TILE), 1)
    maps = {0: [], 1: []}
    for s in _out_blocks(o_ref):
        for e in range(2):
            keep = (lane >= e * LANES) & (lane < (e + 1) * LANES)
            for n in range(n_q):
                q = q_ref[n * MXU_TILE:(n + 1) * MXU_TILE, s * MXU_TILE:(s + 1) * MXU_TILE]
                maps[n % 2].append(((s, e, n), _transposed_tile(q, keep, BF16), s * MXU_TILE,
                                    (2 * s + e) * VT_ROWS))
    res = _mxu_softmax_pv(maps, k_ref, vt_ref)
    for s in _out_blocks(o_ref):
        rows = [jnp.concatenate([res[(s, e, n)] for n in range(n_q)], axis=1) for e in range(2)]
        o_ref[:, s * LANES:(s + 1) * LANES] = _pair_rows(*rows).astype(BF16)


def _diff_attn_mxu_kernel(lam_init, q_ref, k_ref, vt_ref, lp_ref, sub_ref, o_ref):
    lp = lp_ref[...]
    lam = (jnp.exp(jnp.sum(lp[0:1] * lp[1:2], axis=-1, keepdims=True))
           - jnp.exp(jnp.sum(lp[2:3] * lp[3:4], axis=-1, keepdims=True)) + lam_init)
    n_q = q_ref.shape[0] // MXU_TILE
    lane = lax.broadcasted_iota(jnp.int32, (MXU_TILE, MXU_TILE), 1)
    n_win = o_ref.shape[1] // MXU_TILE
    maps = {0: [], 1: []}
    for w in range(n_win):
        for hq in range(DIFF_WIN_HEADS):
            for t in range(2):
                lo_lane = hq * 2 * DIFF_HEAD + t * DIFF_HEAD
                keep = (lane >= lo_lane) & (lane < lo_lane + DIFF_HEAD)
                for n in range(n_q):
                    q = q_ref[n * MXU_TILE:(n + 1) * MXU_TILE, w * MXU_TILE:(w + 1) * MXU_TILE]
                    maps[n % 2].append(((w, hq, t, n), _transposed_tile(q, keep, BF16), w * MXU_TILE,
                                        (DIFF_WIN_HEADS * w + hq) * VT_ROWS))
    res = _mxu_softmax_pv(maps, k_ref, vt_ref)
    lane128 = _lane_iota(q_ref.shape[0])
    lo = lane128 < 2 * DIFF_HEAD
    inv = 1.0 / (2 * DIFF_HEAD)
    for s in _out_blocks(o_ref):
        w, b = divmod(s, 2)
        heads = []
        for hq in (2 * b, 2 * b + 1):
            a1 = jnp.concatenate([res[(w, hq, 0, n)] for n in range(n_q)], axis=1)
            a2 = jnp.concatenate([res[(w, hq, 1, n)] for n in range(n_q)], axis=1)
            heads.append(a1 - lam * a2)
        o = _pair_rows(*heads)
        o2 = o * o
        r0 = lax.rsqrt(jnp.sum(jnp.where(lo, o2, 0.0), axis=-1, keepdims=True) * inv + EPS)
        r1 = lax.rsqrt(jnp.sum(jnp.where(lo, 0.0, o2), axis=-1, keepdims=True) * inv + EPS)
        o = o * jnp.where(lo, r0, r1) * sub_ref[...] * (1.0 - lam_init)
        o_ref[:, s * LANES:(s + 1) * LANES] = o.astype(BF16)


def _diff_attn_kernel(lam_init, bounded, q_ref, k_ref, vt_ref, lp_ref, sub_ref, o_ref):
    lp = lp_ref[...]
    lam = (jnp.exp(jnp.sum(lp[0:1] * lp[1:2], axis=-1, keepdims=True))
           - jnp.exp(jnp.sum(lp[2:3] * lp[3:4], axis=-1, keepdims=True)) + lam_init)
    lane = _lane_iota(q_ref.shape[0])
    lo = lane < 2 * DIFF_HEAD
    inv = 1.0 / (2 * DIFF_HEAD)
    for s in _out_blocks(o_ref):
        sl = slice(s * LANES, (s + 1) * LANES)
        q, k = q_ref[:, sl], k_ref[:, sl]
        zero = jnp.zeros_like(q)
        outs = []
        for hd in range(2):
            base = hd * 2 * DIFF_HEAD
            m1 = (lane >= base) & (lane < base + DIFF_HEAD)
            m2 = (lane >= base + DIFF_HEAD) & (lane < base + 2 * DIFF_HEAD)
            vt = vt_ref[(2 * s + hd) * VT_ROWS:(2 * s + hd + 1) * VT_ROWS, :]
            a1 = _softmax_pv_t(jnp.where(m1, q, zero), k, vt, bounded)
            a2 = _softmax_pv_t(jnp.where(m2, q, zero), k, vt, bounded)
            outs.append(a1 - lam * a2)
        o = _pair_rows(outs[0], outs[1])
        o2 = o * o
        r0 = lax.rsqrt(jnp.sum(jnp.where(lo, o2, 0.0), axis=-1, keepdims=True) * inv + EPS)
        r1 = lax.rsqrt(jnp.sum(jnp.where(lo, 0.0, o2), axis=-1, keepdims=True) * inv + EPS)
        o = o * jnp.where(lo, r0, r1) * sub_ref[...] * (1.0 - lam_init)
        o_ref[:, sl] = o.astype(BF16)


def _tile(n, want):
    return want if n % want == 0 else n


def _const_spec(shape):
    nd = len(shape)
    return pl.BlockSpec(shape, lambda *_: (0,) * nd, pipeline_mode=pl.Buffered(1))


def _tok_spec(tm, width):
    return pl.BlockSpec((None, tm, width), lambda b, t: (b, t, 0))


def _tok_t_spec(tm, width):
    return pl.BlockSpec((None, width, tm), lambda b, t: (b, 0, t))


def _tab_spec(tm):
    return pl.BlockSpec((tm, LANES), lambda b, t: (t, 0))


def _token_call(body, name, x, consts, tabs, extra_tok, outs, tm):
    bsz, seq, _ = x.shape
    tok_inputs = [x] + list(extra_tok)
    in_specs = ([_tok_spec(tm, a.shape[-1]) for a in tok_inputs]
                + [_const_spec(c.shape) for c in consts]
                + [_tab_spec(tm) for _ in tabs])
    out_shape = [jax.ShapeDtypeStruct((bsz, w, seq) if tr else (bsz, seq, w), d) for w, d, tr in outs]
    out_specs = [_tok_t_spec(tm, w) if tr else _tok_spec(tm, w) for w, d, tr in outs]
    return pl.pallas_call(
        body, name=name,
        grid=(bsz, seq // tm),
        in_specs=in_specs, out_specs=out_specs, out_shape=out_shape,
        compiler_params=pltpu.CompilerParams(
            dimension_semantics=("arbitrary", "arbitrary"), vmem_limit_bytes=VMEM_LIMIT),
    )(*tok_inputs, *consts, *tabs)


def _attn_dispatch(body, name, logit_bound, *args):
    q, k, vt, extra, *static, mxu_body = args

    def run(bounded, q, k, vt, extra):
        tag = "bounded" if bounded else "shifted"
        tq = _tile(q.shape[1], QUERY_TILE_BOUNDED if bounded else QUERY_TILE_SHIFTED)
        n_sub = OUT_BLOCKS_BOUNDED if bounded else OUT_BLOCKS_SHIFTED
        if bounded and mxu_body is not None and q.shape[1] % QUERY_TILE_MXU == 0:
            return _attn_call(mxu_body[0], f"{name}_{tag}", q, k, vt, extra, *static,
                              QUERY_TILE_MXU, mxu_body[1])
        return _attn_call(functools.partial(body, bounded), f"{name}_{tag}", q, k, vt, extra,
                          *static, tq, n_sub)

    return lax.cond(logit_bound <= LOGIT_BOUND, functools.partial(run, True),
                    functools.partial(run, False), q, k, vt, tuple(extra))


def _attn_call(body, name, q, k, vt, extra, q_blk, k_blk, n_blocks, kv_share, tq, n_sub):
    bsz, seq, _ = q.shape
    if kv_share == 1:
        k_spec = pl.BlockSpec((None, seq, k_blk * n_sub), lambda b, h, t: (b, 0, h))
        vt_spec = pl.BlockSpec((None, 2 * VT_ROWS * n_sub, seq), lambda b, h, t: (b, h, 0))
    else:
        assert kv_share % n_sub == 0
        k_spec = pl.BlockSpec((None, seq, k_blk), lambda b, h, t: (b, 0, (h * n_sub) // kv_share))
        vt_spec = pl.BlockSpec((None, 2 * VT_ROWS, seq), lambda b, h, t: (b, (h * n_sub) // kv_share, 0))
    in_specs = [pl.BlockSpec((None, tq, q_blk * n_sub), lambda b, h, t: (b, t, h)), k_spec, vt_spec
                ] + [pl.BlockSpec(e.shape, lambda b, h, t: (0, 0)) for e in extra]
    return pl.pallas_call(
        body, name=name,
        grid=(bsz, n_blocks // n_sub, seq // tq),
        in_specs=in_specs,
        out_specs=pl.BlockSpec((None, tq, LANES * n_sub), lambda b, h, t: (b, t, h)),
        out_shape=jax.ShapeDtypeStruct((bsz, seq, n_blocks * LANES), BF16),
        compiler_params=pltpu.CompilerParams(
            dimension_semantics=("arbitrary", "arbitrary", "arbitrary"), vmem_limit_bytes=VMEM_LIMIT),
    )(q, k, vt, *extra)


def _rope_lane_tables(pos_a, pos_b):
    half = ROPE_DIM // 2
    inv = ROPE_THETA ** (-jnp.arange(0, ROPE_DIM, 2, dtype=F32) / ROPE_DIM)
    lane = np.arange(LANES)
    sign = jnp.asarray(np.where(lane % ROPE_DIM < half, -1.0, 1.0), F32)
    use_b = jnp.asarray((lane // ROPE_DIM) % 2 == 1)
    freq = inv[lane % half]
    ang_a = pos_a.astype(F32)[:, None] * freq[None, :]
    ang_b = pos_b.astype(F32)[:, None] * freq[None, :]
    cos = jnp.where(use_b[None, :], jnp.cos(ang_b), jnp.cos(ang_a))
    sin = jnp.where(use_b[None, :], jnp.sin(ang_b), jnp.sin(ang_a)) * sign[None, :]
    return cos, sin


def _group_sum_matrix(group):
    lane = np.arange(LANES)
    g = (lane[:, None] // group == lane[None, :] // group).astype(np.float32)
    return jnp.asarray(np.concatenate([g, g], axis=0), BF16)


def _row(v):
    return v.reshape(1, -1).astype(F32)


def _max_abs(v):
    return jnp.max(jnp.abs(v.astype(F32)))


def kernel(x, ffn1_norm, ffn1_w_gate, ffn1_w_up, ffn1_w_down, mix_norm, ffn2_norm, ffn2_w_gate, ffn2_w_up, ffn2_w_down, ab_w_in, mla_q_lora_norm, mla_w_uq, mla_kv_lora_norm, mla_w_ukv, mla_q_norm, mla_k_norm, diff_q_norm, diff_k_norm, diff_lambda_q1, diff_lambda_k1, diff_lambda_q2, diff_lambda_k2, diff_subln, ab_w_out, c_w_in, c_q_norm, c_k_norm, c_w_out):
    bsz, seq, d = x.shape
    depth = ffn1_norm.shape[0]
    tm = _tile(seq, TOKEN_TILE)
    tm_out = _tile(seq, TOKEN_TILE_OUT)
    tm_odd = _tile(seq, TOKEN_TILE_ODD_IN)

    pos = jnp.arange(seq, dtype=jnp.int32)
    cos_t, sin_t = _rope_lane_tables(pos, pos)
    cos_ax, sin_ax = _rope_lane_tables(pos // GRID_W, pos % GRID_W)
    lane = np.arange(LANES)
    mla_rot = jnp.asarray((lane >= MLA_NOPE) & (lane < MLA_NOPE + MLA_ROPE))
    cos_m = jnp.where(mla_rot[None, :], cos_t, 1.0)
    sin_m = jnp.where(mla_rot[None, :], sin_t, 0.0)
    gs_all, gs64, gs32 = _group_sum_matrix(LANES), _group_sum_matrix(C_HEAD), _group_sum_matrix(DIFF_HEAD)

    for i in range(depth):
        j = i // 2
        ffn1 = (_row(ffn1_norm[i]), ffn1_w_gate[i].astype(BF16), ffn1_w_up[i].astype(BF16),
                ffn1_w_down[i].astype(BF16))
        ffn2 = (_row(ffn2_norm[i]), ffn2_w_gate[i].astype(BF16), ffn2_w_up[i].astype(BF16),
                ffn2_w_down[i].astype(BF16))
        if i % 2 == 0:
            lam_init = 0.8 - 0.6 * math.exp(-0.3 * i)
            w = ab_w_in[j].astype(BF16)
            o_kr = MLA_Q_RANK + MLA_KV_RANK
            o_dq = o_kr + MLA_ROPE
            o_dv = o_dq + 2 * DIFF_W
            kr_blk = jnp.pad(w[:, o_kr:o_dq], ((0, 0), (MLA_NOPE, LANES - MLA_DIM)))
            w_in = jnp.concatenate([w[:, 0:o_kr], kr_blk, w[:, o_dq:o_dv]], axis=1)
            w_vt = w[:, o_dv:].T
            w_uq = jnp.pad(mla_w_uq[j].reshape(MLA_Q_RANK, MLA_HEADS, MLA_DIM),
                           ((0, 0), (0, 0), (0, LANES - MLA_DIM))
                           ).reshape(MLA_Q_RANK, MLA_HEADS * LANES).astype(BF16)
            ukv = mla_w_ukv[j].reshape(MLA_KV_RANK, MLA_HEADS, MLA_NOPE + MLA_V)
            w_uk = jnp.pad(ukv[:, :, :MLA_NOPE], ((0, 0), (0, 0), (0, LANES - MLA_NOPE))
                           ).reshape(MLA_KV_RANK, MLA_HEADS * LANES).astype(BF16)
            w_uvt = ukv[:, :, MLA_NOPE:].reshape(MLA_KV_RANK, MLA_HEADS * MLA_V).T.astype(BF16)
            pad96 = (0, LANES - MLA_DIM)
            consts = ffn1 + (_row(mix_norm[i]), w_in, w_vt,
                             _row(mla_q_lora_norm[j]), w_uq, _row(mla_kv_lora_norm[j]), w_uk, w_uvt,
                             _row(jnp.pad(mla_q_norm[j], pad96)), _row(jnp.pad(mla_k_norm[j], pad96)),
                             _row(jnp.tile(diff_q_norm[j], LANES // DIFF_HEAD)),
                             _row(jnp.tile(diff_k_norm[j], LANES // DIFF_HEAD)),
                             gs_all, gs32)
            x, q, k, vt, dq, dk, dvt = _token_call(
                _even_in_kernel, f"even_in_{i}", x, consts, (cos_m, sin_m, cos_t, sin_t), (),
                ((d, F32, False), (MLA_HEADS * LANES, BF16, False), (MLA_HEADS * LANES, BF16, False),
                 (VT_ROWS * MLA_HEADS, BF16, True), (DIFF_W, BF16, False), (DIFF_W, BF16, False),
                 (VT_ROWS * DIFF_HEADS, BF16, True)), tm)
            mla_bound = (MLA_DIM ** 0.5 * LOG2E) * _max_abs(mla_q_norm[j]) * _max_abs(mla_k_norm[j])
            o_mla = _attn_dispatch(_mla_attn_kernel, f"mla_attn_{i}", mla_bound, q, k, vt, (),
                                   2 * LANES, 2 * LANES, MLA_HEADS // 2, 1, (_mla_attn_mxu_kernel, MXU_BLOCKS_MLA))
            lam_p = jnp.stack([diff_lambda_q1[j], diff_lambda_k1[j],
                               diff_lambda_q2[j], diff_lambda_k2[j]]).astype(F32)
            sub = _row(jnp.tile(diff_subln[j], 2))
            diff_bound = (DIFF_HEAD ** 0.5 * LOG2E) * _max_abs(diff_q_norm[j]) * _max_abs(diff_k_norm[j])
            o_diff = _attn_dispatch(functools.partial(_diff_attn_kernel, lam_init), f"diff_attn_{i}",
                                    diff_bound, dq, dk, dvt, (lam_p, sub),
                                    LANES, LANES, DIFF_HEADS // 2, 1,
                                    (functools.partial(_diff_attn_mxu_kernel, lam_init), MXU_BLOCKS_DIFF))
            wo = ab_w_out[j].astype(BF16)
            consts = (wo[:MLA_HEADS * MLA_V], wo[MLA_HEADS * MLA_V:]) + ffn2
            (x,) = _token_call(_even_out_kernel, f"even_out_{i}", x, consts, (), (o_mla, o_diff),
                               ((d, F32, False),), tm_out)
        else:
            w = c_w_in[j].astype(BF16)
            wq = w[:, :C_Q_W].reshape(d, 2, 2, C_GROUPS, C_HEAD)
            wq = wq.transpose(0, 1, 3, 2, 4).reshape(d, C_Q_W)
            w_in = jnp.concatenate([wq, w[:, C_Q_W:C_Q_W + C_KV_W]], axis=1)
            w_vt = w[:, C_Q_W + C_KV_W:].T
            consts = ffn1 + (_row(mix_norm[i]), w_in, w_vt,
                             _row(jnp.tile(c_q_norm[j], 2)), _row(jnp.tile(c_k_norm[j], 2)), gs64)
            x, q, k, vt = _token_call(
                _odd_in_kernel, f"odd_in_{i}", x, consts, (cos_ax, sin_ax), (),
                ((d, F32, False), (2 * C_Q_W, FP8, False), (C_KV_HEADS * 2 * LANES, FP8, False),
                 (VT_ROWS * C_KV_HEADS, BF16, True)), tm_odd)
            gqa_bound = (C_HEAD ** 0.5 * LOG2E) * _max_abs(c_q_norm[j]) * _max_abs(c_k_norm[j])
            o = _attn_dispatch(_gqa_attn_kernel, f"gqa_attn_{i}", gqa_bound, q, k, vt, (),
                               2 * LANES, 4 * LANES, C_HEADS // 2, C_GROUPS, (_gqa_attn_mxu_kernel, MXU_BLOCKS_GQA))
            wo = c_w_out[j].astype(BF16).reshape(2, 2, C_GROUPS, C_HEAD, d).transpose(0, 2, 1, 3, 4)
            wo = wo.reshape(C_Q_W, d)
            consts = (wo,) + ffn2
            (x,) = _token_call(_odd_out_kernel, f"odd_out_{i}", x, consts, (), (o,),
                               ((d, F32, False),), tm_out)
    return x
```

```python
import functools
import math

import jax
import jax.numpy as jnp
import numpy as np
from jax import lax
from jax.experimental import pallas as pl
from jax.experimental.pallas import tpu as pltpu

F32 = jnp.float32
BF16 = jnp.bfloat16
FP8 = jnp.float8_e4m3fn

LANES = 128
EPS = 1e-6
ROPE_THETA = 10000.0
GRID_W = 64
LOG2E = math.log2(math.e)

MLA_HEADS, MLA_Q_RANK, MLA_KV_RANK = 8, 256, 128
MLA_NOPE, MLA_ROPE, MLA_V = 64, 32, 64
MLA_DIM = MLA_NOPE + MLA_ROPE
DIFF_HEADS, DIFF_HEAD = 8, 32
DIFF_W = DIFF_HEADS * 2 * DIFF_HEAD
C_HEADS, C_KV_HEADS, C_HEAD = 16, 4, 64
C_GROUPS = C_HEADS // C_KV_HEADS
C_Q_W, C_KV_W = C_HEADS * C_HEAD, C_KV_HEADS * C_HEAD
ROPE_DIM = 32

Z_CKV = MLA_Q_RANK
Z_KR = Z_CKV + MLA_KV_RANK
Z_DQ = Z_KR + LANES
Z_DK = Z_DQ + DIFF_W

VMEM_LIMIT = 56 * 1024 * 1024
TOKEN_TILE = 512
FFN_CHUNKS = 11
FFN_CHUNK_ALIGN = 256
TOKEN_SPLIT_IN = 2
TOKEN_SPLIT_OUT = 1
TOKEN_TILE_OUT = 1024
TOKEN_TILE_ODD_IN = 1024
QUERY_TILE_BOUNDED = 1024
QUERY_TILE_SHIFTED = 512
OUT_BLOCKS_BOUNDED = 2
OUT_BLOCKS_SHIFTED = 1
QUERY_TILE_MXU = 1024
MXU_BLOCKS_MLA, MXU_BLOCKS_DIFF, MXU_BLOCKS_GQA = 2, 2, 4
NT_DIMS = (((1,), (1,)), ((), ()))
LOGIT_BOUND = 64.0
HEAD_V = 64
ONES_ROWS = 64
VT_ROWS = HEAD_V + ONES_ROWS


def _rms(x, g):
    ms = jnp.mean(x * x, axis=-1, keepdims=True)
    return x * lax.rsqrt(ms + EPS) * g


def _bdot(a, b):
    return jnp.dot(a.astype(BF16), b, preferred_element_type=F32)


def _swiglu_half(x, g, wg, wu, wd):
    h = _rms(x, g).astype(BF16)
    d_ff = wg.shape[1]
    bounds = [round(i * d_ff / FFN_CHUNKS / FFN_CHUNK_ALIGN) * FFN_CHUNK_ALIGN for i in range(FFN_CHUNKS)] + [d_ff]
    acc = None
    for c0, c1 in zip(bounds[:-1], bounds[1:]):
        gate = jnp.dot(h, wg[:, c0:c1], preferred_element_type=F32)
        up = jnp.dot(h, wu[:, c0:c1], preferred_element_type=F32)
        act = (0.5 * gate) * (1.0 + jnp.tanh(0.5 * gate)) * up
        part = jnp.dot(act.astype(BF16), wd[c0:c1, :], preferred_element_type=F32)
        acc = part if acc is None else acc + part
    return x + 0.5 * acc


def _lane_iota(rows):
    return lax.broadcasted_iota(jnp.int32, (rows, LANES), 1)


def _norm_rope_block(x, gsum, inv_count, gain, cos, sin, post_scale):
    lane = _lane_iota(x.shape[0])
    x2 = x * x
    hi = x2.astype(BF16)
    lo = (x2 - hi.astype(F32)).astype(BF16)
    ss = jnp.dot(jnp.concatenate([hi, lo], axis=1), gsum, preferred_element_type=F32)
    y = x * lax.rsqrt(ss * inv_count + EPS) * gain
    up = pltpu.roll(y, LANES - ROPE_DIM // 2, axis=1)
    dn = pltpu.roll(y, ROPE_DIM // 2, axis=1)
    partner = jnp.where((lane % ROPE_DIM) < ROPE_DIM // 2, up, dn)
    y = y * cos + partner * sin
    if post_scale != 1.0:
        y = y * post_scale
    return y


def _softmax_pv_t(qm, k, vt, bounded):
    st = lax.dot_general(k, qm, NT_DIMS, preferred_element_type=F32)
    if not bounded:
        st = st - jnp.max(st, axis=0, keepdims=True)
    ot = jnp.dot(vt, jnp.exp2(st).astype(BF16), preferred_element_type=F32)
    return ot[:HEAD_V] / ot[HEAD_V:HEAD_V + 1]


def _store_vt_with_ones(vt_ref, cols, vt):
    ones = jnp.ones((ONES_ROWS, vt.shape[1]), BF16)
    for hd in range(vt.shape[0] // HEAD_V):
        vt_ref[hd * VT_ROWS:hd * VT_ROWS + HEAD_V, cols] = vt[hd * HEAD_V:(hd + 1) * HEAD_V].astype(BF16)
        vt_ref[hd * VT_ROWS + HEAD_V:(hd + 1) * VT_ROWS, cols] = ones


def _fp8_split(y):
    hi = y.astype(FP8).astype(F32)
    return hi, (y - hi).astype(FP8).astype(F32)


def _pair_operands(hi, lo, is_query):
    first = _lane_iota(hi.shape[0]) < C_HEAD
    hi_sw, lo_sw = pltpu.roll(hi, C_HEAD, axis=1), pltpu.roll(lo, C_HEAD, axis=1)
    zero = jnp.zeros_like(hi)
    out = []
    for h_own, h_oth, l_own, l_oth in ((hi, hi_sw, lo, lo_sw), (hi_sw, hi, lo_sw, lo)):
        if is_query:
            a, b = jnp.where(first, h_own, l_oth), jnp.where(first, h_own, zero)
        else:
            a, b = jnp.where(first, h_own, h_oth), jnp.where(first, l_own, zero)
        out.append(jnp.concatenate([a, b], axis=1).astype(FP8))
    return out


def _row_groups(n_rows, split):
    step = n_rows // split
    return [slice(r * step, (r + 1) * step) for r in range(split)]


def _even_in_kernel(x_ref, g1_ref, wg_ref, wu_ref, wd_ref, gm_ref, win_ref, wvt_ref,
                    gq_ref, wuq_ref, gkv_ref, wuk_ref, wuvt_ref, qn_ref, kn_ref, dqn_ref, dkn_ref,
                    gs96_ref, gs32_ref, cm_ref, sm_ref, cd_ref, sd_ref,
                    xo_ref, q_ref, k_ref, vt_ref, dq_ref, dk_ref, dvt_ref):
    q_scale = MLA_DIM ** -0.5 * LOG2E
    d_scale = DIFF_HEAD ** -0.5 * LOG2E
    xo_ref[...] = _swiglu_half(x_ref[...], g1_ref[...], wg_ref, wu_ref, wd_ref)
    for rows in _row_groups(x_ref.shape[0], TOKEN_SPLIT_IN):
        x = xo_ref[rows, :]
        h = _rms(x, gm_ref[...]).astype(BF16)
        z = jnp.dot(h, win_ref[...], preferred_element_type=F32)
        _store_vt_with_ones(dvt_ref, rows, lax.dot_general(wvt_ref[...], h, NT_DIMS,
                                                           preferred_element_type=F32))
        kr = z[:, Z_KR:Z_DQ]
        cm, sm, cd, sd = cm_ref[rows, :], sm_ref[rows, :], cd_ref[rows, :], sd_ref[rows, :]
        ckv = _rms(z[:, Z_CKV:Z_KR], gkv_ref[...]).astype(BF16)
        qf = _bdot(_rms(z[:, 0:Z_CKV], gq_ref[...]), wuq_ref[...])
        kf = jnp.dot(ckv, wuk_ref[...], preferred_element_type=F32)
        _store_vt_with_ones(vt_ref, rows, lax.dot_general(wuvt_ref[...], ckv, NT_DIMS,
                                                          preferred_element_type=F32))
        for hd in range(MLA_HEADS):
            sl = slice(hd * LANES, (hd + 1) * LANES)
            q_ref[rows, sl] = _norm_rope_block(qf[:, sl], gs96_ref[...], 1.0 / MLA_DIM, qn_ref[...],
                                               cm, sm, q_scale).astype(BF16)
            k_ref[rows, sl] = _norm_rope_block(kf[:, sl] + kr, gs96_ref[...], 1.0 / MLA_DIM, kn_ref[...],
                                               cm, sm, 1.0).astype(BF16)
        for b in range(DIFF_W // LANES):
            sl = slice(b * LANES, (b + 1) * LANES)
            dq_ref[rows, sl] = _norm_rope_block(z[:, Z_DQ + b * LANES:Z_DQ + (b + 1) * LANES],
                                                gs32_ref[...], 1.0 / DIFF_HEAD, dqn_ref[...],
                                                cd, sd, d_scale).astype(BF16)
            dk_ref[rows, sl] = _norm_rope_block(z[:, Z_DK + b * LANES:Z_DK + (b + 1) * LANES],
                                                gs32_ref[...], 1.0 / DIFF_HEAD, dkn_ref[...],
                                                cd, sd, 1.0).astype(BF16)


def _odd_in_kernel(x_ref, g1_ref, wg_ref, wu_ref, wd_ref, gm_ref, win_ref, wvt_ref,
                   qn_ref, kn_ref, gs64_ref, ca_ref, sa_ref,
                   xo_ref, q_ref, k_ref, vt_ref):
    q_scale = C_HEAD ** -0.5 * LOG2E
    xo_ref[...] = _swiglu_half(x_ref[...], g1_ref[...], wg_ref, wu_ref, wd_ref)
    for rows in _row_groups(x_ref.shape[0], TOKEN_SPLIT_IN):
        x = xo_ref[rows, :]
        h = _rms(x, gm_ref[...]).astype(BF16)
        z = jnp.dot(h, win_ref[...], preferred_element_type=F32)
        _store_vt_with_ones(vt_ref, rows, lax.dot_general(wvt_ref[...], h, NT_DIMS,
                                                          preferred_element_type=F32))
        ca, sa = ca_ref[rows, :], sa_ref[rows, :]
        for b in range(C_Q_W // LANES):
            y = _norm_rope_block(z[:, b * LANES:(b + 1) * LANES], gs64_ref[...], 1.0 / C_HEAD,
                                 qn_ref[...], ca, sa, q_scale)
            hi, lo = _fp8_split(y)
            q_ref[rows, 2 * b * LANES:(2 * b + 1) * LANES] = hi.astype(FP8)
            q_ref[rows, (2 * b + 1) * LANES:(2 * b + 2) * LANES] = lo.astype(FP8)
        for b in range(C_KV_W // LANES):
            y = _norm_rope_block(z[:, C_Q_W + b * LANES:C_Q_W + (b + 1) * LANES], gs64_ref[...],
                                 1.0 / C_HEAD, kn_ref[...], ca, sa, 1.0)
            for e, op in enumerate(_pair_operands(*_fp8_split(y), is_query=False)):
                k_ref[rows, (2 * b + e) * 2 * LANES:(2 * b + e + 1) * 2 * LANES] = op


def _even_out_kernel(x_ref, oa_ref, ob_ref, woa_ref, wob_ref, g2_ref, wg_ref, wu_ref, wd_ref, xo_ref):
    for rows in _row_groups(x_ref.shape[0], TOKEN_SPLIT_OUT):
        x = x_ref[rows, :]
        x = x + jnp.dot(oa_ref[rows, :], woa_ref[...], preferred_element_type=F32)
        x = x + jnp.dot(ob_ref[rows, :], wob_ref[...], preferred_element_type=F32)
        xo_ref[rows, :] = _swiglu_half(x, g2_ref[...], wg_ref, wu_ref, wd_ref)


def _odd_out_kernel(x_ref, o_ref, wo_ref, g2_ref, wg_ref, wu_ref, wd_ref, xo_ref):
    for rows in _row_groups(x_ref.shape[0], TOKEN_SPLIT_OUT):
        x = x_ref[rows, :] + jnp.dot(o_ref[rows, :], wo_ref[...], preferred_element_type=F32)
        xo_ref[rows, :] = _swiglu_half(x, g2_ref[...], wg_ref, wu_ref, wd_ref)


def _pair_rows(ot0, ot1):
    return jnp.concatenate([ot0, ot1], axis=0).T


def _out_blocks(o_ref):
    return range(o_ref.shape[1] // LANES)


def _mla_attn_kernel(bounded, q_ref, k_ref, vt_ref, o_ref):
    for s in _out_blocks(o_ref):
        outs = []
        for hd in (2 * s, 2 * s + 1):
            sl = slice(hd * LANES, (hd + 1) * LANES)
            outs.append(_softmax_pv_t(q_ref[:, sl], k_ref[:, sl],
                                      vt_ref[hd * VT_ROWS:(hd + 1) * VT_ROWS, :], bounded))
        o_ref[:, s * LANES:(s + 1) * LANES] = _pair_rows(*outs).astype(BF16)


def _gqa_attn_kernel(bounded, q_ref, k_ref, vt_ref, o_ref):
    w = 2 * LANES
    for s in _out_blocks(o_ref):
        hi = q_ref[:, s * w:s * w + LANES].astype(F32)
        lo = q_ref[:, s * w + LANES:(s + 1) * w].astype(F32)
        outs = []
        for e, q3 in enumerate(_pair_operands(hi, lo, is_query=True)):
            outs.append(_softmax_pv_t(q3, k_ref[:, e * w:(e + 1) * w],
                                      vt_ref[e * VT_ROWS:(e + 1) * VT_ROWS, :], bounded))
        o_ref[:, s * LANES:(s + 1) * LANES] = _pair_rows(*outs).astype(BF16)


MXU_TILE = 256
PIPE_LAG = 2
S_ADDR = (0, 64, 128)
O_ADDR = (192, 224)
PV_ROWS = HEAD_V + 16
DIFF_WIN_HEADS = MXU_TILE // (2 * DIFF_HEAD)


def _mxu_softmax_pv(maps, k_ref, vt_ref):
    n_k = k_ref.shape[0] // MXU_TILE
    for mxu in (0, 1):
        for a in S_ADDR:
            pltpu.matmul_pop(a, (MXU_TILE, MXU_TILE), F32, mxu)
        for a in O_ADDR:
            pltpu.matmul_pop(a, (PV_ROWS, MXU_TILE), F32, mxu)
    units = {m: [(g, j) for g in range(len(maps[m])) for j in range(n_k)] for m in (0, 1)}
    n_units = len(units[0])
    o_lag = PIPE_LAG + 2
    results = {}
    for i in range(n_units + o_lag):
        for mxu in (0, 1):
            if i < n_units:
                g, j = units[mxu][i]
                _, wq, k_col, _ = maps[mxu][g]
                pltpu.matmul_push_rhs(wq, staging_register=0, mxu_index=mxu)
                pltpu.matmul_acc_lhs(S_ADDR[i % 3],
                                     k_ref[j * MXU_TILE:(j + 1) * MXU_TILE, k_col:k_col + MXU_TILE],
                                     mxu, load_staged_rhs=0)
            if PIPE_LAG <= i < n_units + PIPE_LAG:
                g, j = units[mxu][i - PIPE_LAG]
                v_row = maps[mxu][g][3]
                st = pltpu.matmul_pop(S_ADDR[(i - PIPE_LAG) % 3], (MXU_TILE, MXU_TILE), F32, mxu)
                pltpu.matmul_push_rhs(jnp.exp2(st).astype(BF16), staging_register=1, mxu_index=mxu)
                v_rows = vt_ref[v_row:v_row + PV_ROWS, j * MXU_TILE:(j + 1) * MXU_TILE]
                pltpu.matmul_acc_lhs(O_ADDR[g % 2], v_rows, mxu, load_staged_rhs=1)
            if i >= o_lag:
                g, j = units[mxu][i - o_lag]
                if j == n_k - 1:
                    ot = pltpu.matmul_pop(O_ADDR[g % 2], (PV_ROWS, MXU_TILE), F32, mxu)
                    results[maps[mxu][g][0]] = ot[:HEAD_V] / ot[HEAD_V:HEAD_V + 1]
    return results


def _transposed_tile(q, keep, dtype):
    if keep is not None:
        q = jnp.where(keep, q, jnp.zeros_like(q))
    return q.astype(F32).T.astype(dtype)


def _gqa_attn_mxu_kernel(q_ref, k_ref, vt_ref, o_ref):
    n_q = q_ref.shape[0] // MXU_TILE
    zeros = jnp.zeros((C_HEAD, MXU_TILE), F32)
    tiles = {}
    for s in _out_blocks(o_ref):
        for n in range(n_q):
            t = q_ref[n * MXU_TILE:(n + 1) * MXU_TILE, s * MXU_TILE:(s + 1) * MXU_TILE].astype(F32).T
            for e in range(2):
                h, l = t[e * C_HEAD:(e + 1) * C_HEAD], t[LANES + e * C_HEAD:LANES + (e + 1) * C_HEAD]
                tiles[(s, e, n)] = jnp.concatenate([h, l, h, zeros], axis=0).astype(FP8)
    maps = {0: [], 1: []}
    for s in _out_blocks(o_ref):
        for e in range(2):
            for n in range(n_q):
                maps[n % 2].append(((s, e, n), tiles[(s, e, n)], e * MXU_TILE, e * VT_ROWS))
    res = _mxu_softmax_pv(maps, k_ref, vt_ref)
    for s in _out_blocks(o_ref):
        rows = [jnp.concatenate([res[(s, e, n)] for n in range(n_q)], axis=1) for e in range(2)]
        o_ref[:, s * LANES:(s + 1) * LANES] = _pair_rows(*rows).astype(BF16)


def _mla_attn_mxu_kernel(q_ref, k_ref, vt_ref, o_ref):
    n_q = q_ref.shape[0] // MXU_TILE
    lane = lax.broadcasted_iota(jnp.int32, (MXU_TILE, MXU_TILE), 1)
    maps = {0: [], 1: []}
    for s in _out_blocks(o_ref):
        for e in range(2):
            keep = (lane >= e * LANES) & (lane < (e + 1) * LANES)
            for n in range(n_q):
                q = q_ref[n * MXU_TILE:(n + 1) * MXU_TILE, s * MXU_TILE:(s + 1) * MXU_TILE]
                maps[n % 2].append(((s, e, n), _transposed_tile(q, keep, BF16), s * MXU_TILE,
                                    (2 * s + e) * VT_ROWS))
    res = _mxu_softmax_pv(maps, k_ref, vt_ref)
    for s in _out_blocks(o_ref):
        rows = [jnp.concatenate([res[(s, e, n)] for n in range(n_q)], axis=1) for e in range(2)]
        o_ref[:, s * LANES:(s + 1) * LANES] = _pair_rows(*rows).astype(BF16)


def _diff_attn_mxu_kernel(lam_init, q_ref, k_ref, vt_ref, lp_ref, sub_ref, o_ref):
    lp = lp_ref[...]
    lam = (jnp.exp(jnp.sum(lp[0:1] * lp[1:2], axis=-1, keepdims=True))
           - jnp.exp(jnp.sum(lp[2:3] * lp[3:4], axis=-1, keepdims=True)) + lam_init)
    n_q = q_ref.shape[0] // MXU_TILE
    lane = lax.broadcasted_iota(jnp.int32, (MXU_TILE, MXU_TILE), 1)
    n_win = o_ref.shape[1] // MXU_TILE
    maps = {0: [], 1: []}
    for w in range(n_win):
        for hq in range(DIFF_WIN_HEADS):
            for t in range(2):
                lo_lane = hq * 2 * DIFF_HEAD + t * DIFF_HEAD
                keep = (lane >= lo_lane) & (lane < lo_lane + DIFF_HEAD)
                for n in range(n_q):
                    q = q_ref[n * MXU_TILE:(n + 1) * MXU_TILE, w * MXU_TILE:(w + 1) * MXU_TILE]
                    maps[n % 2].append(((w, hq, t, n), _transposed_tile(q, keep, BF16), w * MXU_TILE,
                                        (DIFF_WIN_HEADS * w + hq) * VT_ROWS))
    res = _mxu_softmax_pv(maps, k_ref, vt_ref)
    lane128 = _lane_iota(q_ref.shape[0])
    lo = lane128 < 2 * DIFF_HEAD
    inv = 1.0 / (2 * DIFF_HEAD)
    for s in _out_blocks(o_ref):
        w, b = divmod(s, 2)
        heads = []
        for hq in (2 * b, 2 * b + 1):
            a1 = jnp.concatenate([res[(w, hq, 0, n)] for n in range(n_q)], axis=1)
            a2 = jnp.concatenate([res[(w, hq, 1, n)] for n in range(n_q)], axis=1)
            heads.append(a1 - lam * a2)
        o = _pair_rows(*heads)
        o2 = o * o
        r0 = lax.rsqrt(jnp.sum(jnp.where(lo, o2, 0.0), axis=-1, keepdims=True) * inv + EPS)
        r1 = lax.rsqrt(jnp.sum(jnp.where(lo, 0.0, o2), axis=-1, keepdims=True) * inv + EPS)
        o = o * jnp.where(lo, r0, r1) * sub_ref[...] * (1.0 - lam_init)
        o_ref[:, s * LANES:(s + 1) * LANES] = o.astype(BF16)


def _diff_attn_kernel(lam_init, bounded, q_ref, k_ref, vt_ref, lp_ref, sub_ref, o_ref):
    lp = lp_ref[...]
    lam = (jnp.exp(jnp.sum(lp[0:1] * lp[1:2], axis=-1, keepdims=True))
           - jnp.exp(jnp.sum(lp[2:3] * lp[3:4], axis=-1, keepdims=True)) + lam_init)
    lane = _lane_iota(q_ref.shape[0])
    lo = lane < 2 * DIFF_HEAD
    inv = 1.0 / (2 * DIFF_HEAD)
    for s in _out_blocks(o_ref):
        sl = slice(s * LANES, (s + 1) * LANES)
        q, k = q_ref[:, sl], k_ref[:, sl]
        zero = jnp.zeros_like(q)
        outs = []
        for hd in range(2):
            base = hd * 2 * DIFF_HEAD
            m1 = (lane >= base) & (lane < base + DIFF_HEAD)
            m2 = (lane >= base + DIFF_HEAD) & (lane < base + 2 * DIFF_HEAD)
            vt = vt_ref[(2 * s + hd) * VT_ROWS:(2 * s + hd + 1) * VT_ROWS, :]
            a1 = _softmax_pv_t(jnp.where(m1, q, zero), k, vt, bounded)
            a2 = _softmax_pv_t(jnp.where(m2, q, zero), k, vt, bounded)
            outs.append(a1 - lam * a2)
        o = _pair_rows(outs[0], outs[1])
        o2 = o * o
        r0 = lax.rsqrt(jnp.sum(jnp.where(lo, o2, 0.0), axis=-1, keepdims=True) * inv + EPS)
        r1 = lax.rsqrt(jnp.sum(jnp.where(lo, 0.0, o2), axis=-1, keepdims=True) * inv + EPS)
        o = o * jnp.where(lo, r0, r1) * sub_ref[...] * (1.0 - lam_init)
        o_ref[:, sl] = o.astype(BF16)


def _tile(n, want):
    return want if n % want == 0 else n


def _const_spec(shape):
    nd = len(shape)
    return pl.BlockSpec(shape, lambda *_: (0,) * nd, pipeline_mode=pl.Buffered(1))


def _tok_spec(tm, width):
    return pl.BlockSpec((None, tm, width), lambda b, t: (b, t, 0))


def _tok_t_spec(tm, width):
    return pl.BlockSpec((None, width, tm), lambda b, t: (b, 0, t))


def _tab_spec(tm):
    return pl.BlockSpec((tm, LANES), lambda b, t: (t, 0))


def _token_call(body, name, x, consts, tabs, extra_tok, outs, tm):
    bsz, seq, _ = x.shape
    tok_inputs = [x] + list(extra_tok)
    in_specs = ([_tok_spec(tm, a.shape[-1]) for a in tok_inputs]
                + [_const_spec(c.shape) for c in consts]
                + [_tab_spec(tm) for _ in tabs])
    out_shape = [jax.ShapeDtypeStruct((bsz, w, seq) if tr else (bsz, seq, w), d) for w, d, tr in outs]
    out_specs = [_tok_t_spec(tm, w) if tr else _tok_spec(tm, w) for w, d, tr in outs]
    return pl.pallas_call(
        body, name=name,
        grid=(bsz, seq // tm),
        in_specs=in_specs, out_specs=out_specs, out_shape=out_shape,
        compiler_params=pltpu.CompilerParams(
            dimension_semantics=("arbitrary", "arbitrary"), vmem_limit_bytes=VMEM_LIMIT),
    )(*tok_inputs, *consts, *tabs)


def _attn_dispatch(body, name, logit_bound, *args):
    q, k, vt, extra, *static, mxu_body = args

    def run(bounded, q, k, vt, extra):
        tag = "bounded" if bounded else "shifted"
        tq = _tile(q.shape[1], QUERY_TILE_BOUNDED if bounded else QUERY_TILE_SHIFTED)
        n_sub = OUT_BLOCKS_BOUNDED if bounded else OUT_BLOCKS_SHIFTED
        if bounded and mxu_body is not None and q.shape[1] % QUERY_TILE_MXU == 0:
            return _attn_call(mxu_body[0], f"{name}_{tag}", q, k, vt, extra, *static,
                              QUERY_TILE_MXU, mxu_body[1])
        return _attn_call(functools.partial(body, bounded), f"{name}_{tag}", q, k, vt, extra,
                          *static, tq, n_sub)

    return lax.cond(logit_bound <= LOGIT_BOUND, functools.partial(run, True),
                    functools.partial(run, False), q, k, vt, tuple(extra))


def _attn_call(body, name, q, k, vt, extra, q_blk, k_blk, n_blocks, kv_share, tq, n_sub):
    bsz, seq, _ = q.shape
    if kv_share == 1:
        k_spec = pl.BlockSpec((None, seq, k_blk * n_sub), lambda b, h, t: (b, 0, h))
        vt_spec = pl.BlockSpec((None, 2 * VT_ROWS * n_sub, seq), lambda b, h, t: (b, h, 0))
    else:
        assert kv_share % n_sub == 0
        k_spec = pl.BlockSpec((None, seq, k_blk), lambda b, h, t: (b, 0, (h * n_sub) // kv_share))
        vt_spec = pl.BlockSpec((None, 2 * VT_ROWS, seq), lambda b, h, t: (b, (h * n_sub) // kv_share, 0))
    in_specs = [pl.BlockSpec((None, tq, q_blk * n_sub), lambda b, h, t: (b, t, h)), k_spec, vt_spec
                ] + [pl.BlockSpec(e.shape, lambda b, h, t: (0, 0)) for e in extra]
    return pl.pallas_call(
        body, name=name,
        grid=(bsz, n_blocks // n_sub, seq // tq),
        in_specs=in_specs,
        out_specs=pl.BlockSpec((None, tq, LANES * n_sub), lambda b, h, t: (b, t, h)),
        out_shape=jax.ShapeDtypeStruct((bsz, seq, n_blocks * LANES), BF16),
        compiler_params=pltpu.CompilerParams(
            dimension_semantics=("arbitrary", "arbitrary", "arbitrary"), vmem_limit_bytes=VMEM_LIMIT),
    )(q, k, vt, *extra)


def _rope_lane_tables(pos_a, pos_b):
    half = ROPE_DIM // 2
    inv = ROPE_THETA ** (-jnp.arange(0, ROPE_DIM, 2, dtype=F32) / ROPE_DIM)
    lane = np.arange(LANES)
    sign = jnp.asarray(np.where(lane % ROPE_DIM < half, -1.0, 1.0), F32)
    use_b = jnp.asarray((lane // ROPE_DIM) % 2 == 1)
    freq = inv[lane % half]
    ang_a = pos_a.astype(F32)[:, None] * freq[None, :]
    ang_b = pos_b.astype(F32)[:, None] * freq[None, :]
    cos = jnp.where(use_b[None, :], jnp.cos(ang_b), jnp.cos(ang_a))
    sin = jnp.where(use_b[None, :], jnp.sin(ang_b), jnp.sin(ang_a)) * sign[None, :]
    return cos, sin


def _group_sum_matrix(group):
    lane = np.arange(LANES)
    g = (lane[:, None] // group == lane[None, :] // group).astype(np.float32)
    return jnp.asarray(np.concatenate([g, g], axis=0), BF16)


def _row(v):
    return v.reshape(1, -1).astype(F32)


def _max_abs(v):
    return jnp.max(jnp.abs(v.astype(F32)))


def kernel(x, ffn1_norm, ffn1_w_gate, ffn1_w_up, ffn1_w_down, mix_norm, ffn2_norm, ffn2_w_gate, ffn2_w_up, ffn2_w_down, ab_w_in, mla_q_lora_norm, mla_w_uq, mla_kv_lora_norm, mla_w_ukv, mla_q_norm, mla_k_norm, diff_q_norm, diff_k_norm, diff_lambda_q1, diff_lambda_k1, diff_lambda_q2, diff_lambda_k2, diff_subln, ab_w_out, c_w_in, c_q_norm, c_k_norm, c_w_out):
    bsz, seq, d = x.shape
    depth = ffn1_norm.shape[0]
    tm = _tile(seq, TOKEN_TILE)
    tm_out = _tile(seq, TOKEN_TILE_OUT)
    tm_odd = _tile(seq, TOKEN_TILE_ODD_IN)

    pos = jnp.arange(seq, dtype=jnp.int32)
    cos_t, sin_t = _rope_lane_tables(pos, pos)
    cos_ax, sin_ax = _rope_lane_tables(pos // GRID_W, pos % GRID_W)
    lane = np.arange(LANES)
    mla_rot = jnp.asarray((lane >= MLA_NOPE) & (lane < MLA_NOPE + MLA_ROPE))
    cos_m = jnp.where(mla_rot[None, :], cos_t, 1.0)
    sin_m = jnp.where(mla_rot[None, :], sin_t, 0.0)
    gs_all, gs64, gs32 = _group_sum_matrix(LANES), _group_sum_matrix(C_HEAD), _group_sum_matrix(DIFF_HEAD)

    for i in range(depth):
        j = i // 2
        ffn1 = (_row(ffn1_norm[i]), ffn1_w_gate[i].astype(BF16), ffn1_w_up[i].astype(BF16),
                ffn1_w_down[i].astype(BF16))
        ffn2 = (_row(ffn2_norm[i]), ffn2_w_gate[i].astype(BF16), ffn2_w_up[i].astype(BF16),
                ffn2_w_down[i].astype(BF16))
        if i % 2 == 0:
            lam_init = 0.8 - 0.6 * math.exp(-0.3 * i)
            w = ab_w_in[j].astype(BF16)
            o_kr = MLA_Q_RANK + MLA_KV_RANK
            o_dq = o_kr + MLA_ROPE
            o_dv = o_dq + 2 * DIFF_W
            kr_blk = jnp.pad(w[:, o_kr:o_dq], ((0, 0), (MLA_NOPE, LANES - MLA_DIM)))
            w_in = jnp.concatenate([w[:, 0:o_kr], kr_blk, w[:, o_dq:o_dv]], axis=1)
            w_vt = w[:, o_dv:].T
            w_uq = jnp.pad(mla_w_uq[j].reshape(MLA_Q_RANK, MLA_HEADS, MLA_DIM),
                           ((0, 0), (0, 0), (0, LANES - MLA_DIM))
                           ).reshape(MLA_Q_RANK, MLA_HEADS * LANES).astype(BF16)
            ukv = mla_w_ukv[j].reshape(MLA_KV_RANK, MLA_HEADS, MLA_NOPE + MLA_V)
            w_uk = jnp.pad(ukv[:, :, :MLA_NOPE], ((0, 0), (0, 0), (0, LANES - MLA_NOPE))
                           ).reshape(MLA_KV_RANK, MLA_HEADS * LANES).astype(BF16)
            w_uvt = ukv[:, :, MLA_NOPE:].reshape(MLA_KV_RANK, MLA_HEADS * MLA_V).T.astype(BF16)
            pad96 = (0, LANES - MLA_DIM)
            consts = ffn1 + (_row(mix_norm[i]), w_in, w_vt,
                             _row(mla_q_lora_norm[j]), w_uq, _row(mla_kv_lora_norm[j]), w_uk, w_uvt,
                             _row(jnp.pad(mla_q_norm[j], pad96)), _row(jnp.pad(mla_k_norm[j], pad96)),
                             _row(jnp.tile(diff_q_norm[j], LANES // DIFF_HEAD)),
                             _row(jnp.tile(diff_k_norm[j], LANES // DIFF_HEAD)),
                             gs_all, gs32)
            x, q, k, vt, dq, dk, dvt = _token_call(
                _even_in_kernel, f"even_in_{i}", x, consts, (cos_m, sin_m, cos_t, sin_t), (),
                ((d, F32, False), (MLA_HEADS * LANES, BF16, False), (MLA_HEADS * LANES, BF16, False),
                 (VT_ROWS * MLA_HEADS, BF16, True), (DIFF_W, BF16, False), (DIFF_W, BF16, False),
                 (VT_ROWS * DIFF_HEADS, BF16, True)), tm)
            mla_bound = (MLA_DIM ** 0.5 * LOG2E) * _max_abs(mla_q_norm[j]) * _max_abs(mla_k_norm[j])
            o_mla = _attn_dispatch(_mla_attn_kernel, f"mla_attn_{i}", mla_bound, q, k, vt, (),
                                   2 * LANES, 2 * LANES, MLA_HEADS // 2, 1, (_mla_attn_mxu_kernel, MXU_BLOCKS_MLA))
            lam_p = jnp.stack([diff_lambda_q1[j], diff_lambda_k1[j],
                               diff_lambda_q2[j], diff_lambda_k2[j]]).astype(F32)
            sub = _row(jnp.tile(diff_subln[j], 2))
            diff_bound = (DIFF_HEAD ** 0.5 * LOG2E) * _max_abs(diff_q_norm[j]) * _max_abs(diff_k_norm[j])
            o_diff = _attn_dispatch(functools.partial(_diff_attn_kernel, lam_init), f"diff_attn_{i}",
                                    diff_bound, dq, dk, dvt, (lam_p, sub),
                                    LANES, LANES, DIFF_HEADS // 2, 1,
                                    (functools.partial(_diff_attn_mxu_kernel, lam_init), MXU_BLOCKS_DIFF))
            wo = ab_w_out[j].astype(BF16)
            consts = (wo[:MLA_HEADS * MLA_V], wo[MLA_HEADS * MLA_V:]) + ffn2
            (x,) = _token_call(_even_out_kernel, f"even_out_{i}", x, consts, (), (o_mla, o_diff),
                               ((d, F32, False),), tm_out)
        else:
            w = c_w_in[j].astype(BF16)
            wq = w[:, :C_Q_W].reshape(d, 2, 2, C_GROUPS, C_HEAD)
            wq = wq.transpose(0, 1, 3, 2, 4).reshape(d, C_Q_W)
            w_in = jnp.concatenate([wq, w[:, C_Q_W:C_Q_W + C_KV_W]], axis=1)
            w_vt = w[:, C_Q_W + C_KV_W:].T
            consts = ffn1 + (_row(mix_norm[i]), w_in, w_vt,
                             _row(jnp.tile(c_q_norm[j], 2)), _row(jnp.tile(c_k_norm[j], 2)), gs64)
            x, q, k, vt = _token_call(
                _odd_in_kernel, f"odd_in_{i}", x, consts, (cos_ax, sin_ax), (),
                ((d, F32, False), (2 * C_Q_W, FP8, False), (C_KV_HEADS * 2 * LANES, FP8, False),
                 (VT_ROWS * C_KV_HEADS, BF16, True)), tm_odd)
            gqa_bound = (C_HEAD ** 0.5 * LOG2E) * _max_abs(c_q_norm[j]) * _max_abs(c_k_norm[j])
            o = _attn_dispatch(_gqa_attn_kernel, f"gqa_attn_{i}", gqa_bound, q, k, vt, (),
                               2 * LANES, 4 * LANES, C_HEADS // 2, C_GROUPS, (_gqa_attn_mxu_kernel, MXU_BLOCKS_GQA))
            wo = c_w_out[j].astype(BF16).reshape(2, 2, C_GROUPS, C_HEAD, d).transpose(0, 2, 1, 3, 4)
            wo = wo.reshape(C_Q_W, d)
            consts = (wo,) + ffn2
            (x,) = _token_call(_odd_out_kernel, f"odd_out_{i}", x, consts, (), (o,),
                               ((d, F32, False),), tm_out)
    return x
```
